```python
import jax, jax.numpy as jnp
from jax import lax
import numpy as np

D_MODEL = 1024
BATCH = 8
SEQ = 8192
DEPTH = 4

P_DIM = 256
EPS = 1e-6
D_FF = 1408
MLA_HEADS = 4
MLA_Q_RANK = 256
MLA_KV_RANK = 128
MLA_NOPE = 128
MLA_ROPE = 64
MLA_V = 128
ROPE_THETA = 10000.0
GLA_HEADS = 4
GLA_DK = 64
GLA_DV = 128
GLA_GATE_RANK = 16
GLA_GATE_TAU = 16.0
GLA_CHUNK = 16
SB_HEADS = 8
SB_HEAD_DIM = D_MODEL // SB_HEADS
Q_BLOCK = 128

EVEN_SPLITS = (MLA_Q_RANK, MLA_KV_RANK, MLA_ROPE,
               GLA_HEADS * GLA_DK, GLA_HEADS * GLA_DK, GLA_HEADS * GLA_DV,
               GLA_GATE_RANK, GLA_HEADS * GLA_DV)
EVEN_IN = sum(EVEN_SPLITS)
EVEN_SPLIT_IDX = tuple(int(v) for v in np.cumsum(EVEN_SPLITS)[:-1])
EVEN_OUT = MLA_HEADS * MLA_V + GLA_HEADS * GLA_DV
N_EVEN = (DEPTH + 1) // 2
N_ODD = DEPTH // 2

kernel_name = "hybrid_mla_gla_stickbreaking_macaron_ple"


def rms_norm(x, w):
    xf = x.astype(jnp.float32)
    y = xf * lax.rsqrt(jnp.mean(xf * xf, axis=-1, keepdims=True) + EPS)
    return (y * w.astype(jnp.float32)).astype(x.dtype)


def swiglu_ffn(h, w_gu, w_down):
    a, b = jnp.split(h @ w_gu, 2, axis=-1)
    return (jax.nn.silu(a) * b) @ w_down


def rope_tables(positions, dim):
    half = dim // 2
    inv_freq = 1.0 / (ROPE_THETA ** (jnp.arange(half, dtype=jnp.float32) / half))
    ang = positions.astype(jnp.float32)[..., None] * inv_freq
    return jnp.cos(ang), jnp.sin(ang)


def apply_rope(t, cos, sin):
    t1, t2 = jnp.split(t.astype(jnp.float32), 2, axis=-1)
    return jnp.concatenate([t1 * cos - t2 * sin, t1 * sin + t2 * cos], axis=-1).astype(t.dtype)


def mla_attention(q_nope, q_rope, k_nope, k_rope, v):
    B, H, S, _ = q_nope.shape
    scale = (MLA_NOPE + MLA_ROPE) ** -0.5
    outs = []
    for blk in range(S // Q_BLOCK):
        lo, hi = blk * Q_BLOCK, (blk + 1) * Q_BLOCK
        s = (jnp.einsum("bhqd,bhkd->bhqk", q_nope[:, :, lo:hi], k_nope[:, :, :hi])
             + jnp.einsum("bhqr,bkr->bhqk", q_rope[:, :, lo:hi], k_rope[:, :hi])).astype(jnp.float32) * scale
        causal = jnp.arange(hi)[None, :] <= (lo + jnp.arange(Q_BLOCK))[:, None]
        w = jax.nn.softmax(jnp.where(causal, s, -jnp.inf), axis=-1)
        outs.append(jnp.einsum("bhqk,bhkd->bhqd", w.astype(v.dtype), v[:, :, :hi]))
    o = jnp.concatenate(outs, axis=2)
    return o.transpose(0, 2, 1, 3).reshape(B, S, H * v.shape[-1])


def gla_chunked(q, k, v, log_a):
    B, H, S, DK = q.shape
    DV = v.shape[-1]
    nc = S // GLA_CHUNK

    def to_chunks(t):
        return jnp.moveaxis(t.astype(jnp.float32).reshape(B, H, nc, GLA_CHUNK, t.shape[-1]), 2, 0)

    qc = to_chunks(q) * (DK ** -0.5)
    kc, vc, gc = to_chunks(k), to_chunks(v), to_chunks(log_a)
    mask = jnp.tril(jnp.ones((GLA_CHUNK, GLA_CHUNK), dtype=bool))

    def step(state, inp):
        qb, kb, vb, gb = inp
        b = jnp.cumsum(gb, axis=-2)
        diff = b[:, :, :, None, :] - b[:, :, None, :, :]
        decay = jnp.exp(jnp.where(mask[:, :, None], diff, -jnp.inf))
        att = jnp.einsum("bhijd,bhjd->bhij", qb[:, :, :, None, :] * decay, kb)
        o = att @ vb + jnp.einsum("bhid,bhde->bhie", qb * jnp.exp(b), state)
        b_last = b[:, :, -1:, :]
        state = (jnp.exp(b_last[:, :, 0, :])[..., None] * state
                 + jnp.einsum("bhjd,bhje->bhde", kb * jnp.exp(b_last - b), vb))
        return state, o

    state0 = jnp.zeros((B, H, DK, DV), jnp.float32)
    _, out = lax.scan(step, state0, (qc, kc, vc, gc))
    return jnp.moveaxis(out, 0, 2).reshape(B, H, S, DV)


def stick_breaking_attention(q, k, v):
    B, H, S, d = q.shape
    q = q * (SB_HEAD_DIM ** -0.5)
    r = jnp.arange(Q_BLOCK)
    upper_incl = (r[None, :] >= r[:, None]).astype(jnp.float32)
    outs = []
    for blk in range(S // Q_BLOCK):
        lo, hi = blk * Q_BLOCK, (blk + 1) * Q_BLOCK
        nck = blk + 1
        z = jnp.einsum("bhqd,bhkd->bhqk", q[:, :, lo:hi], k[:, :, :hi]).astype(jnp.float32)
        causal = jnp.arange(hi)[None, :] < (lo + r)[:, None]
        l = jnp.where(causal, jax.nn.log_sigmoid(-z), 0.0).reshape(B, H, Q_BLOCK, nck, Q_BLOCK)
        rc = jnp.einsum("bhqck,jk->bhqcj", l, upper_incl)
        c = jnp.arange(nck)
        strict_later = (c[None, :] > c[:, None]).astype(jnp.float32)
        suffix = jnp.einsum("bhqe,ce->bhqc", l.sum(axis=-1), strict_later)
        log_w = z + (rc + suffix[..., None]).reshape(B, H, Q_BLOCK, hi)
        a = jnp.where(causal, jnp.exp(log_w), 0.0)
        outs.append(jnp.einsum("bhqk,bhkd->bhqd", a.astype(v.dtype), v[:, :, :hi]))
    o = jnp.concatenate(outs, axis=2)
    return o.transpose(0, 2, 1, 3).reshape(B, S, H * d)


def mla_gla_mixer(h, cos, sin, w_in, q_norm, kv_norm, w_uq, w_ukv, w_gate2, b_gate, gla_norm, w_out):
    B, S, _ = h.shape
    c_q, c_kv, k_r, g_q, g_k, g_v, g_a, g_r = jnp.split(h @ w_in, EVEN_SPLIT_IDX, axis=-1)
    q = (rms_norm(c_q, q_norm) @ w_uq).reshape(B, S, MLA_HEADS, MLA_NOPE + MLA_ROPE)
    q_nope, q_rope = q[..., :MLA_NOPE], q[..., MLA_NOPE:]
    q_rope = apply_rope(q_rope, cos[:, :, None, :], sin[:, :, None, :])
    kv = (rms_norm(c_kv, kv_norm) @ w_ukv).reshape(B, S, MLA_HEADS, MLA_NOPE + MLA_V)
    k_nope, v_mla = kv[..., :MLA_NOPE], kv[..., MLA_NOPE:]
    k_rope = apply_rope(k_r, cos, sin)
    o_mla = mla_attention(q_nope.transpose(0, 2, 1, 3), q_rope.transpose(0, 2, 1, 3),
                          k_nope.transpose(0, 2, 1, 3), k_rope, v_mla.transpose(0, 2, 1, 3))
    log_a = jax.nn.log_sigmoid((g_a @ w_gate2 + b_gate).astype(jnp.float32)) / GLA_GATE_TAU
    heads_k = lambda t: t.reshape(B, S, GLA_HEADS, -1).transpose(0, 2, 1, 3)
    o_gla = gla_chunked(heads_k(g_q), heads_k(g_k), heads_k(g_v), heads_k(log_a))
    o_gla = rms_norm(o_gla.transpose(0, 2, 1, 3).astype(h.dtype), gla_norm)
    o_gla = (o_gla * jax.nn.silu(g_r.reshape(B, S, GLA_HEADS, GLA_DV))).reshape(B, S, GLA_HEADS * GLA_DV)
    return jnp.concatenate([o_mla.astype(h.dtype), o_gla], axis=-1) @ w_out


def stick_breaking_mixer(h, w_qkv, w_out):
    B, S, _ = h.shape
    qkv = (h @ w_qkv).reshape(B, S, 3, SB_HEADS, SB_HEAD_DIM).transpose(2, 0, 3, 1, 4)
    o = stick_breaking_attention(qkv[0], qkv[1], qkv[2])
    return o.astype(h.dtype) @ w_out


def setup_inputs(seed: int = 0) -> dict:
    key = jax.random.key(seed)
    ks = jax.random.split(key, 32)

    def dense(k, shape, fan_in):
        return jax.random.normal(k, shape, jnp.float32) * (fan_in ** -0.5)

    def gain(k, shape):
        return 1.0 + 0.02 * jax.random.normal(k, shape, jnp.float32)

    positions = (jnp.arange(SEQ, dtype=jnp.int32)[None, :]
                 + jax.random.randint(ks[2], (BATCH, 1), 0, 4096, dtype=jnp.int32))
    return {
        "x": jax.random.normal(ks[0], (BATCH, SEQ, D_MODEL), jnp.float32),
        "p": jax.random.normal(ks[1], (DEPTH, BATCH, SEQ, P_DIM), jnp.float32),
        "positions": positions,
        "ffn1_norm": gain(ks[3], (DEPTH, D_MODEL)),
        "ffn1_w_gu": dense(ks[4], (DEPTH, D_MODEL, 2 * D_FF), D_MODEL),
        "ffn1_w_down": dense(ks[5], (DEPTH, D_FF, D_MODEL), D_FF),
        "mix_norm": gain(ks[6], (DEPTH, D_MODEL)),
        "ffn2_norm": gain(ks[7], (DEPTH, D_MODEL)),
        "ffn2_w_gu": dense(ks[8], (DEPTH, D_MODEL, 2 * D_FF), D_MODEL),
        "ffn2_w_down": dense(ks[9], (DEPTH, D_FF, D_MODEL), D_FF),
        "ple_norm": gain(ks[10], (DEPTH, D_MODEL)),
        "ple_w_gate": dense(ks[11], (DEPTH, D_MODEL, D_MODEL), D_MODEL),
        "ple_w_proj": dense(ks[12], (DEPTH, P_DIM, D_MODEL), P_DIM),
        "ev_w_in": dense(ks[13], (N_EVEN, D_MODEL, EVEN_IN), D_MODEL),
        "ev_q_norm": gain(ks[14], (N_EVEN, MLA_Q_RANK)),
        "ev_kv_norm": gain(ks[15], (N_EVEN, MLA_KV_RANK)),
        "ev_w_uq": dense(ks[16], (N_EVEN, MLA_Q_RANK, MLA_HEADS * (MLA_NOPE + MLA_ROPE)), MLA_Q_RANK),
        "ev_w_ukv": dense(ks[17], (N_EVEN, MLA_KV_RANK, MLA_HEADS * (MLA_NOPE + MLA_V)), MLA_KV_RANK),
        "ev_w_gate2": dense(ks[18], (N_EVEN, GLA_GATE_RANK, GLA_HEADS * GLA_DK), GLA_GATE_RANK),
        "ev_b_gate": 0.01 * jax.random.normal(ks[19], (N_EVEN, GLA_HEADS * GLA_DK), jnp.float32),
        "ev_gla_norm": gain(ks[20], (N_EVEN, GLA_HEADS, GLA_DV)),
        "ev_w_out": dense(ks[21], (N_EVEN, EVEN_OUT, D_MODEL), EVEN_OUT),
        "od_w_qkv": dense(ks[22], (N_ODD, D_MODEL, 3 * D_MODEL), D_MODEL),
        "od_w_out": dense(ks[23], (N_ODD, D_MODEL, D_MODEL), D_MODEL),
        "final_norm": gain(ks[24], (D_MODEL,)),
    }


def reference(x, p, positions, ffn1_norm, ffn1_w_gu, ffn1_w_down, mix_norm, ffn2_norm, ffn2_w_gu,
              ffn2_w_down, ple_norm, ple_w_gate, ple_w_proj, ev_w_in, ev_q_norm, ev_kv_norm, ev_w_uq,
              ev_w_ukv, ev_w_gate2, ev_b_gate, ev_gla_norm, ev_w_out, od_w_qkv, od_w_out, final_norm):
    cos, sin = rope_tables(positions, MLA_ROPE)
    for i in range(DEPTH):
        x = x + 0.5 * swiglu_ffn(rms_norm(x, ffn1_norm[i]), ffn1_w_gu[i], ffn1_w_down[i])
        h = rms_norm(x, mix_norm[i])
        j = i // 2
        if i % 2 == 0:
            x = x + mla_gla_mixer(h, cos, sin, ev_w_in[j], ev_q_norm[j], ev_kv_norm[j], ev_w_uq[j],
                                  ev_w_ukv[j], ev_w_gate2[j], ev_b_gate[j], ev_gla_norm[j], ev_w_out[j])
        else:
            x = x + stick_breaking_mixer(h, od_w_qkv[j], od_w_out[j])
        x = x + 0.5 * swiglu_ffn(rms_norm(x, ffn2_norm[i]), ffn2_w_gu[i], ffn2_w_down[i])
        gate = jax.nn.sigmoid(rms_norm(x, ple_norm[i]) @ ple_w_gate[i])
        x = x + gate * (p[i] @ ple_w_proj[i])
    return rms_norm(x, final_norm)
```

```python
import functools

import numpy as np
import jax
import jax.numpy as jnp
from jax import lax
from jax.experimental import pallas as pl
from jax.experimental.pallas import tpu as pltpu

D_MODEL = 1024
P_DIM = 256
EPS = 1e-6
D_FF = 1408
MLA_HEADS = 4
MLA_Q_RANK = 256
MLA_KV_RANK = 128
MLA_NOPE = 128
MLA_ROPE = 64
MLA_V = 128
ROPE_THETA = 10000.0
GLA_HEADS = 4
GLA_DK = 64
GLA_DV = 128
GLA_GATE_RANK = 16
GLA_GATE_TAU = 16.0
SB_HEADS = 8
SB_HEAD_DIM = D_MODEL // SB_HEADS

LANES = 128
MLA_QK_PAD = 2 * LANES
GLA_QK = GLA_HEADS * GLA_DK
GLA_V = GLA_HEADS * GLA_DV

TOKEN_TILE = 512
MLA_TQ = 256
MLA_TK = 256
SB_TQ = 256
SB_TK = 128
GLA_CHUNK = 128
GLA_LEVELS = 7
VMEM_LIMIT = 56 * 1024 * 1024
MASK_VALUE = -1e30

_BF = jnp.bfloat16
_F32 = jnp.float32


def _dot(a, b):
    return jnp.dot(a, b, preferred_element_type=_F32)


def _dot_nt(a, b):
    return lax.dot_general(a, b, (((1,), (1,)), ((), ())), preferred_element_type=_F32)


def _dot_tn(a, b):
    return lax.dot_general(a, b, (((0,), (0,)), ((), ())), preferred_element_type=_F32)


def _rms(x, w):
    return x * lax.rsqrt(jnp.mean(x * x, axis=-1, keepdims=True) + EPS) * w


def _sigmoid(x):
    return 1.0 / (1.0 + jnp.exp(-x))


def _silu(x):
    return x * _sigmoid(x)


def _log_sigmoid(x):
    return jnp.minimum(x, 0.0) - jnp.log(1.0 + jnp.exp(-jnp.abs(x)))


def _ffn(x, norm_w, w_gu, w_down):
    h = _rms(x, norm_w).astype(_BF)
    gu = _dot(h, w_gu)
    act = (_silu(gu[:, :D_FF]) * gu[:, D_FF:]).astype(_BF)
    return x + 0.5 * _dot(act, w_down)


def _const_spec(shape):
    nd = len(shape)
    return pl.BlockSpec(shape, lambda *_: (0,) * nd, pipeline_mode=pl.Buffered(1))


def _params(semantics):
    return pltpu.CompilerParams(dimension_semantics=semantics, vmem_limit_bytes=VMEM_LIMIT)


def _pre_even_kernel(x_ref, cos_ref, sin_ref, n1_ref, wgu_ref, wdn_ref, nm_ref, win_ref, qn_ref, kvn_ref,
                     wuq_ref, wuqr_ref, wukv_ref, wg2_ref, bg_ref,
                     x1_ref, q_ref, k_ref, v_ref, gq_ref, gk_ref, la_ref, gv_ref, gr_ref):
    x1 = _ffn(x_ref[...], n1_ref[...], wgu_ref[...], wdn_ref[...])
    x1_ref[...] = x1
    h = _rms(x1, nm_ref[...]).astype(_BF)
    proj = _dot(h, win_ref[...])
    c_q = proj[:, 0:256]
    c_kv = proj[:, 256:384]
    k_r = proj[:, 384:512]
    k_rr = proj[:, 512:640]
    g_q = proj[:, 640:896]
    g_k = proj[:, 896:1152]
    g_v = proj[:, 1152:1664]
    g_a = proj[:, 1664:1792]
    g_r = proj[:, 1792:2304]
    cos = cos_ref[...]
    sin = sin_ref[...]

    cq_n = _rms(c_q, qn_ref[...]).astype(_BF)
    q = _dot(cq_n, wuq_ref[...])
    q_rot = _dot(cq_n, wuqr_ref[...])
    scale = (MLA_NOPE + MLA_ROPE) ** -0.5
    for hd in range(MLA_HEADS):
        lo = hd * MLA_QK_PAD
        q_ref[:, lo:lo + LANES] = (q[:, lo:lo + LANES] * scale).astype(_BF)
        q_rope = q[:, lo + LANES:lo + 2 * LANES] * cos + q_rot[:, hd * LANES:(hd + 1) * LANES] * sin
        q_ref[:, lo + LANES:lo + 2 * LANES] = (q_rope * scale).astype(_BF)
    ckv_n = _rms(c_kv, kvn_ref[...]).astype(_BF)
    kv = _dot(ckv_n, wukv_ref[...])
    k_rope = (k_r * cos + k_rr * sin).astype(_BF)
    for hd in range(MLA_HEADS):
        lo = hd * MLA_QK_PAD
        k_ref[:, lo:lo + LANES] = kv[:, hd * LANES:(hd + 1) * LANES].astype(_BF)
        k_ref[:, lo + LANES:lo + 2 * LANES] = k_rope
    v_ref[...] = kv[:, MLA_HEADS * MLA_NOPE:].astype(_BF)

    gq_ref[...] = g_q * (GLA_DK ** -0.5)
    gk_ref[...] = g_k
    gate = _dot(g_a.astype(_BF), wg2_ref[...]) + bg_ref[...]
    la_ref[...] = _log_sigmoid(gate) * (1.0 / GLA_GATE_TAU)
    gv_ref[...] = g_v.astype(_BF)
    gr_ref[...] = g_r.astype(_BF)


def _pre_odd_kernel(x_ref, n1_ref, wgu_ref, wdn_ref, nm_ref, wqkv_ref, x1_ref, q_ref, k_ref, v_ref):
    x1 = _ffn(x_ref[...], n1_ref[...], wgu_ref[...], wdn_ref[...])
    x1_ref[...] = x1
    h = _rms(x1, nm_ref[...]).astype(_BF)
    qkv = _dot(h, wqkv_ref[...])
    q_ref[...] = (qkv[:, :D_MODEL] * (SB_HEAD_DIM ** -0.5)).astype(_BF)
    k_ref[...] = qkv[:, D_MODEL:2 * D_MODEL].astype(_BF)
    v_ref[...] = qkv[:, 2 * D_MODEL:].astype(_BF)


def _row_spec(width, tm):
    return pl.BlockSpec((tm, width), lambda i: (i, 0))


def _pre_even(x, cos_t, sin_t, w):
    t = x.shape[0]
    tm = min(TOKEN_TILE, t)
    consts = [w["n1"], w["wgu"], w["wdn"], w["nm"], w["win"], w["qn"], w["kvn"], w["wuq"], w["wuqr"],
              w["wukv"], w["wg2"], w["bg"]]
    out_widths = [(D_MODEL, _F32), (MLA_HEADS * MLA_QK_PAD, _BF), (MLA_HEADS * MLA_QK_PAD, _BF),
                  (MLA_HEADS * MLA_V, _BF), (GLA_QK, _F32), (GLA_QK, _F32), (GLA_QK, _F32),
                  (GLA_V, _BF), (GLA_V, _BF)]
    return pl.pallas_call(
        _pre_even_kernel,
        grid=(t // tm,),
        in_specs=[_row_spec(D_MODEL, tm), _row_spec(LANES, tm), _row_spec(LANES, tm)]
        + [_const_spec(c.shape) for c in consts],
        out_specs=[_row_spec(wd, tm) for wd, _ in out_widths],
        out_shape=[jax.ShapeDtypeStruct((t, wd), dt) for wd, dt in out_widths],
        compiler_params=_params(("parallel",)),
        name="pre_even",
    )(x, cos_t, sin_t, *consts)


def _pre_odd(x, w):
    t = x.shape[0]
    tm = min(TOKEN_TILE, t)
    consts = [w["n1"], w["wgu"], w["wdn"], w["nm"], w["wqkv"]]
    out_widths = [(D_MODEL, _F32), (D_MODEL, _BF), (D_MODEL, _BF), (D_MODEL, _BF)]
    return pl.pallas_call(
        _pre_odd_kernel,
        grid=(t // tm,),
        in_specs=[_row_spec(D_MODEL, tm)] + [_const_spec(c.shape) for c in consts],
        out_specs=[_row_spec(wd, tm) for wd, _ in out_widths],
        out_shape=[jax.ShapeDtypeStruct((t, wd), dt) for wd, dt in out_widths],
        compiler_params=_params(("parallel",)),
        name="pre_odd",
    )(x, *consts)


def _post_kernel(n_mix, final, *refs):
    x_ref = refs[0]
    mix_refs = refs[1:1 + n_mix]
    p_ref = refs[1 + n_mix]
    wout_refs = refs[2 + n_mix:2 + 2 * n_mix]
    n2_ref, wgu_ref, wdn_ref, np_ref, wpg_ref, wpp_ref, nf_ref, out_ref = refs[2 + 2 * n_mix:]
    x = x_ref[...]
    for m_ref, w_ref in zip(mix_refs, wout_refs):
        x = x + _dot(m_ref[...], w_ref[...])
    x = _ffn(x, n2_ref[...], wgu_ref[...], wdn_ref[...])
    gate = _sigmoid(_dot(_rms(x, np_ref[...]).astype(_BF), wpg_ref[...]))
    x = x + gate * _dot(p_ref[...].astype(_BF), wpp_ref[...])
    if final:
        x = _rms(x, nf_ref[...])
    out_ref[...] = x


def _post(x, mixes, p, wouts, w, final_norm, final):
    t = x.shape[0]
    tm = min(TOKEN_TILE, t)
    consts = list(wouts) + [w["n2"], w["wgu2"], w["wdn2"], w["np"], w["wpg"], w["wpp"], final_norm]
    return pl.pallas_call(
        functools.partial(_post_kernel, len(mixes), final),
        grid=(t // tm,),
        in_specs=[_row_spec(D_MODEL, tm)] + [_row_spec(m.shape[1], tm) for m in mixes]
        + [_row_spec(P_DIM, tm)] + [_const_spec(c.shape) for c in consts],
        out_specs=_row_spec(D_MODEL, tm),
        out_shape=jax.ShapeDtypeStruct((t, D_MODEL), _F32),
        compiler_params=_params(("parallel",)),
        name="post_final" if final else "post",
    )(x, *mixes, p, *consts)


def _mla_kernel(q_ref, k_ref, v_ref, o_ref, m_ref, l_ref, acc_ref):
    qi = pl.program_id(2)
    q = q_ref[0]
    m_ref[...] = jnp.full(m_ref.shape, MASK_VALUE, _F32)
    l_ref[...] = jnp.zeros(l_ref.shape, _F32)
    acc_ref[...] = jnp.zeros(acc_ref.shape, _F32)
    reps = MLA_TK // LANES

    def step(c, masked):
        start = pl.multiple_of(c * MLA_TK, MLA_TK)
        k = k_ref[0, pl.ds(start, MLA_TK), :]
        v = v_ref[0, pl.ds(start, MLA_TK), :]
        s = _dot_nt(q, k)
        if masked:
            row = lax.broadcasted_iota(jnp.int32, s.shape, 0)
            col = lax.broadcasted_iota(jnp.int32, s.shape, 1)
            s = jnp.where(col <= row, s, MASK_VALUE)
        m_prev = m_ref[...]
        m_next = jnp.maximum(m_prev, jnp.max(s, axis=1, keepdims=True))
        p = jnp.exp(s - jnp.concatenate([m_next] * reps, axis=1))
        alpha = jnp.exp(m_prev - m_next)
        l_ref[...] = alpha * l_ref[...] + jnp.sum(p, axis=1, keepdims=True)
        acc_ref[...] = alpha * acc_ref[...] + _dot(p.astype(_BF), v)
        m_ref[...] = m_next

    def body(c, carry):
        step(c, False)
        return carry

    lax.fori_loop(0, qi * (MLA_TQ // MLA_TK), body, 0)
    step(qi, True)
    o_ref[0] = (acc_ref[...] / l_ref[...]).astype(o_ref.dtype)


def _mla(q, k, v):
    b, s, _ = q.shape
    assert MLA_TQ == MLA_TK and s % MLA_TQ == 0
    return pl.pallas_call(
        _mla_kernel,
        grid=(b, MLA_HEADS, s // MLA_TQ),
        in_specs=[pl.BlockSpec((1, MLA_TQ, MLA_QK_PAD), lambda bi, hi, qi: (bi, qi, hi)),
                  pl.BlockSpec((1, s, MLA_QK_PAD), lambda bi, hi, qi: (bi, 0, hi)),
                  pl.BlockSpec((1, s, MLA_V), lambda bi, hi, qi: (bi, 0, hi))],
        out_specs=pl.BlockSpec((1, MLA_TQ, MLA_V), lambda bi, hi, qi: (bi, qi, hi)),
        out_shape=jax.ShapeDtypeStruct((b, s, MLA_HEADS * MLA_V), _BF),
        scratch_shapes=[pltpu.VMEM((MLA_TQ, LANES), _F32)] * 3,
        compiler_params=_params(("parallel", "parallel", "arbitrary")),
        name="mla_attention",
    )(q, k, v)


def _sb_kernel(q_ref, k_ref, v_ref, tri_ref, o_ref, suf_ref, acc_ref):
    qi = pl.program_id(2)
    q = q_ref[0]
    suf_ref[...] = jnp.zeros(suf_ref.shape, _F32)
    acc_ref[...] = jnp.zeros(acc_ref.shape, _F32)
    tri = tri_ref[...]
    n_chunks = (qi + 1) * (SB_TQ // SB_TK)

    def body(i, carry):
        c = n_chunks - 1 - i
        start = pl.multiple_of(c * SB_TK, SB_TK)
        k = k_ref[0, pl.ds(start, SB_TK), :]
        v = v_ref[0, pl.ds(start, SB_TK), :]
        z = _dot_nt(q, k)
        row = lax.broadcasted_iota(jnp.int32, z.shape, 0) + qi * SB_TQ
        col = lax.broadcasted_iota(jnp.int32, z.shape, 1) + start
        causal = col < row
        lg = jnp.where(causal, _log_sigmoid(-z), 0.0)
        lg_hi = lg.astype(_BF)
        lg_lo = (lg - lg_hi.astype(_F32)).astype(_BF)
        sums = _dot(jnp.concatenate([lg_hi, lg_lo], axis=1), tri)
        suf = suf_ref[...]
        log_w = z + sums[:, :SB_TK] + suf
        a = jnp.where(causal, jnp.exp(log_w), 0.0)
        acc_ref[...] += _dot(a.astype(_BF), v)
        suf_ref[...] = suf + sums[:, SB_TK:]
        return carry

    lax.fori_loop(0, n_chunks, body, 0)
    o_ref[0] = acc_ref[...].astype(o_ref.dtype)


def _sb_tri():
    r = np.arange(SB_TK)
    upper = (r[:, None] >= r[None, :]).astype(np.float32)
    half = np.concatenate([upper, np.ones((SB_TK, SB_TK), np.float32)], axis=1)
    return jnp.asarray(np.concatenate([half, half], axis=0), dtype=_BF)


def _sb(q, k, v):
    b, s, _ = q.shape
    assert SB_TK == LANES and s % SB_TQ == 0
    tri = _sb_tri()
    return pl.pallas_call(
        _sb_kernel,
        grid=(b, SB_HEADS, s // SB_TQ),
        in_specs=[pl.BlockSpec((1, SB_TQ, SB_HEAD_DIM), lambda bi, hi, qi: (bi, qi, hi)),
                  pl.BlockSpec((1, s, SB_HEAD_DIM), lambda bi, hi, qi: (bi, 0, hi)),
                  pl.BlockSpec((1, s, SB_HEAD_DIM), lambda bi, hi, qi: (bi, 0, hi)),
                  _const_spec(tri.shape)],
        out_specs=pl.BlockSpec((1, SB_TQ, SB_HEAD_DIM), lambda bi, hi, qi: (bi, qi, hi)),
        out_shape=jax.ShapeDtypeStruct((b, s, D_MODEL), _BF),
        scratch_shapes=[pltpu.VMEM((SB_TQ, LANES), _F32)] * 2,
        compiler_params=_params(("parallel", "parallel", "arbitrary")),
        name="sb_attention",
    )(q, k, v, tri)


def _gla_constants():
    c = GLA_CHUNK
    idx = np.arange(c)
    seg = []
    masks = []
    for lvl in range(GLA_LEVELS):
        s = 1 << lvl
        blk = idx // s
        start = blk * s
        end = start + s - 1
        t = idx[None, :]
        q_side = ((blk % 2 == 1)[:, None] & (t >= start[:, None]) & (t <= idx[:, None]))
        k_side = ((blk % 2 == 0)[:, None] & (t > idx[:, None]) & (t <= end[:, None]))
        seg.append(np.concatenate([q_side, k_side], axis=0))
        pair = ((idx[:, None] // (2 * s)) == (idx[None, :] // (2 * s))) \
            & ((blk % 2 == 1)[:, None]) & ((blk % 2 == 0)[None, :])
        masks.append(np.tile(pair, (GLA_HEADS, 1)))
    t = idx[None, :]
    full = np.concatenate([t <= idx[:, None], t > idx[:, None]], axis=0)
    seg.append(full)
    seg = np.stack(seg).astype(np.float32)
    masks = np.stack(masks).astype(np.float32)
    head_of_qk = np.arange(GLA_QK) // GLA_DK
    head_of_v = np.arange(GLA_V) // GLA_DV
    head_mask = (np.arange(8)[:, None] == head_of_qk[None, :]).astype(np.float32)
    expand = (head_of_qk[:, None] == head_of_v[None, :]).astype(np.float32)
    return (jnp.asarray(seg, _BF), jnp.asarray(masks, _F32), jnp.asarray(head_mask, _F32),
            jnp.asarray(expand, _BF), jnp.asarray(expand.T, _F32))


def _gla_kernel(gq_ref, gk_ref, la_ref, gv_ref, gr_ref, seg_ref, mask_ref, hm_ref, exp_ref, bd_ref,
                gn_ref, o_ref, state_ref, att_ref):
    c = GLA_CHUNK

    @pl.when(pl.program_id(1) == 0)
    def _():
        state_ref[...] = jnp.zeros(state_ref.shape, _F32)

    q = gq_ref[0]
    k = gk_ref[0]
    la = la_ref[0]
    v = gv_ref[0]
    la_hi = la.astype(_BF)
    la_lo = (la - la_hi.astype(_F32)).astype(_BF)
    la2 = jnp.concatenate([la_hi, la_lo], axis=1)

    def seg_exp(i):
        e = _dot(seg_ref[i], la2)
        return jnp.exp(e[:, :GLA_QK] + e[:, GLA_QK:])

    hm = hm_ref[...]
    for lvl in range(GLA_LEVELS):
        w = seg_exp(lvl)
        ql = q * w[:c]
        kl = (k * w[c:]).astype(_BF)
        qs = jnp.concatenate([ql * hm[hd:hd + 1] for hd in range(GLA_HEADS)], axis=0).astype(_BF)
        a = _dot_nt(qs, kl) * mask_ref[lvl]
        if lvl == 0:
            att_ref[...] = a
        else:
            att_ref[...] += a

    w = seg_exp(GLA_LEVELS)
    state = state_ref[...]
    o = _dot_nt((q * w[:c]).astype(_BF), state.astype(_BF))
    o = o + _dot((q * k).astype(_BF), exp_ref[...]) * v.astype(_F32)
    o_intra = [_dot(att_ref[hd * c:(hd + 1) * c, :].astype(_BF), v[:, hd * GLA_DV:(hd + 1) * GLA_DV])
               for hd in range(GLA_HEADS)]
    o = o + jnp.concatenate(o_intra, axis=1)

    upd = _dot_tn(v, (k * w[c:]).astype(_BF))
    state_ref[...] = state * w[c - 1:c] + upd * bd_ref[...]

    gn = gn_ref[...]
    gr = gr_ref[0].astype(_F32)
    for hd in range(GLA_HEADS):
        sl = slice(hd * GLA_DV, (hd + 1) * GLA_DV)
        o_ref[0, :, sl] = (_rms(o[:, sl], gn[:, sl]) * _silu(gr[:, sl])).astype(o_ref.dtype)


def _gla(gq, gk, la, gv, gr, gla_norm):
    b, s, _ = gq.shape
    c = GLA_CHUNK
    assert s % c == 0 and (1 << GLA_LEVELS) == c
    consts = list(_gla_constants()) + [gla_norm]
    tok = lambda wd: pl.BlockSpec((1, c, wd), lambda bi, ci: (bi, ci, 0))
    return pl.pallas_call(
        _gla_kernel,
        grid=(b, s // c),
        in_specs=[tok(GLA_QK), tok(GLA_QK), tok(GLA_QK), tok(GLA_V), tok(GLA_V)]
        + [_const_spec(cn.shape) for cn in consts],
        out_specs=tok(GLA_V),
        out_shape=jax.ShapeDtypeStruct((b, s, GLA_V), _BF),
        scratch_shapes=[pltpu.VMEM((GLA_V, GLA_QK), _F32), pltpu.VMEM((GLA_HEADS * c, c), _F32)],
        compiler_params=_params(("parallel", "arbitrary")),
        name="gla_chunked",
    )(gq, gk, la, gv, gr, *consts)


def _rotate_half_cols(w):
    half = w.shape[1] // 2
    return jnp.concatenate([-w[:, half:], w[:, :half]], axis=1)


def _pad_cols(w, width):
    return jnp.pad(w, ((0, 0), (0, width - w.shape[1])))


def _prep_even_weights(w_in, w_uq, w_ukv, w_gate2):
    splits = np.cumsum([MLA_Q_RANK, MLA_KV_RANK, MLA_ROPE, GLA_QK, GLA_QK, GLA_V, GLA_GATE_RANK])
    c_q, c_kv, k_r, g_q, g_k, g_v, g_a, g_r = jnp.split(w_in, splits, axis=1)
    win = jnp.concatenate([c_q, c_kv, _pad_cols(k_r, LANES), _pad_cols(_rotate_half_cols(k_r), LANES),
                           g_q, g_k, g_v, _pad_cols(g_a, LANES), g_r], axis=1).astype(_BF)
    uq = w_uq.reshape(MLA_Q_RANK, MLA_HEADS, MLA_NOPE + MLA_ROPE)
    uq_rope = uq[:, :, MLA_NOPE:]
    rot = jnp.concatenate([-uq_rope[:, :, MLA_ROPE // 2:], uq_rope[:, :, :MLA_ROPE // 2]], axis=2)
    wuq = jnp.pad(uq, ((0, 0), (0, 0), (0, MLA_QK_PAD - uq.shape[2]))).reshape(MLA_Q_RANK, -1).astype(_BF)
    wuqr = jnp.pad(rot, ((0, 0), (0, 0), (0, LANES - MLA_ROPE))).reshape(MLA_Q_RANK, -1).astype(_BF)
    ukv = w_ukv.reshape(MLA_KV_RANK, MLA_HEADS, MLA_NOPE + MLA_V)
    wukv = jnp.concatenate([ukv[:, :, :MLA_NOPE].reshape(MLA_KV_RANK, -1),
                            ukv[:, :, MLA_NOPE:].reshape(MLA_KV_RANK, -1)], axis=1).astype(_BF)
    wg2 = jnp.pad(w_gate2, ((0, LANES - GLA_GATE_RANK), (0, 0))).astype(_BF)
    return win, wuq, wuqr, wukv, wg2


def _rope_tables(positions):
    half = MLA_ROPE // 2
    inv_freq = 1.0 / (ROPE_THETA ** (jnp.arange(half, dtype=_F32) / half))
    ang = positions.astype(_F32)[..., None] * inv_freq
    cos, sin = jnp.cos(ang), jnp.sin(ang)
    pad = jnp.zeros(cos.shape[:-1] + (LANES - MLA_ROPE,), _F32)
    return (jnp.concatenate([cos, cos, pad], axis=-1).reshape(-1, LANES),
            jnp.concatenate([sin, sin, pad], axis=-1).reshape(-1, LANES))


def kernel(x, p, positions, ffn1_norm, ffn1_w_gu, ffn1_w_down, mix_norm, ffn2_norm, ffn2_w_gu, ffn2_w_down, ple_norm, ple_w_gate, ple_w_proj, ev_w_in, ev_q_norm, ev_kv_norm, ev_w_uq, ev_w_ukv, ev_w_gate2, ev_b_gate, ev_gla_norm, ev_w_out, od_w_qkv, od_w_out, final_norm):
    b, s, d = x.shape
    depth = p.shape[0]
    t = b * s
    cos_t, sin_t = _rope_tables(positions)
    row = lambda a: a.reshape(1, -1).astype(_F32)
    xt = x.reshape(t, d)
    fin = row(final_norm)
    for i in range(depth):
        j = i // 2
        common = dict(n1=row(ffn1_norm[i]), wgu=ffn1_w_gu[i].astype(_BF), wdn=ffn1_w_down[i].astype(_BF),
                      nm=row(mix_norm[i]), n2=row(ffn2_norm[i]), wgu2=ffn2_w_gu[i].astype(_BF),
                      wdn2=ffn2_w_down[i].astype(_BF), np=row(ple_norm[i]), wpg=ple_w_gate[i].astype(_BF),
                      wpp=ple_w_proj[i].astype(_BF))
        if i % 2 == 0:
            win, wuq, wuqr, wukv, wg2 = _prep_even_weights(ev_w_in[j], ev_w_uq[j], ev_w_ukv[j], ev_w_gate2[j])
            w = dict(common, win=win, wuq=wuq, wuqr=wuqr, wukv=wukv, wg2=wg2, qn=row(ev_q_norm[j]),
                     kvn=row(ev_kv_norm[j]), bg=row(ev_b_gate[j]))
            x1, q, k, v, gq, gk, la, gv, gr = _pre_even(xt, cos_t, sin_t, w)
            seq = lambda a: a.reshape(b, s, a.shape[-1])
            o_mla = _mla(seq(q), seq(k), seq(v))
            o_gla = _gla(seq(gq), seq(gk), seq(la), seq(gv), seq(gr), row(ev_gla_norm[j]))
            mixes = [o_mla.reshape(t, -1), o_gla.reshape(t, -1)]
            w_out = ev_w_out[j].astype(_BF)
            wouts = [w_out[:MLA_HEADS * MLA_V], w_out[MLA_HEADS * MLA_V:]]
        else:
            w = dict(common, wqkv=od_w_qkv[j].astype(_BF))
            x1, q, k, v = _pre_odd(xt, w)
            seq = lambda a: a.reshape(b, s, a.shape[-1])
            o_sb = _sb(seq(q), seq(k), seq(v))
            mixes = [o_sb.reshape(t, -1)]
            wouts = [od_w_out[j].astype(_BF)]
        xt = _post(x1, mixes, p[i].reshape(t, -1), wouts, w, fin, i == depth - 1)
    return xt.reshape(b, s, d)
```

```python
import functools

import numpy as np
import jax
import jax.numpy as jnp
from jax import lax
from jax.experimental import pallas as pl
from jax.experimental.pallas import tpu as pltpu

D_MODEL = 1024
P_DIM = 256
EPS = 1e-6
D_FF = 1408
MLA_HEADS = 4
MLA_Q_RANK = 256
MLA_KV_RANK = 128
MLA_NOPE = 128
MLA_ROPE = 64
MLA_V = 128
ROPE_THETA = 10000.0
GLA_HEADS = 4
GLA_DK = 64
GLA_DV = 128
GLA_GATE_RANK = 16
GLA_GATE_TAU = 16.0
SB_HEADS = 8
SB_HEAD_DIM = D_MODEL // SB_HEADS

LANES = 128
MLA_QK_PAD = 2 * LANES
GLA_QK = GLA_HEADS * GLA_DK
GLA_V = GLA_HEADS * GLA_DV

TOKEN_TILE = 512
MLA_TQ = 256
MLA_TK = 512
MLA_HEADS_PER_STEP = 2
SB_TQ = 256
SB_TK = 256
SB_EXIT_LOG_WEIGHT = -106.0
SB_BOUND_SLACK = 1.001
GLA_CHUNK = 128
GLA_LEVELS = 7
VMEM_LIMIT = 56 * 1024 * 1024
MASK_VALUE = -1e30
LOG2_E = 1.4426950408889634

_BF = jnp.bfloat16
_F32 = jnp.float32


def _dot(a, b):
    return jnp.dot(a, b, preferred_element_type=_F32)


def _dot_nt(a, b):
    return lax.dot_general(a, b, (((1,), (1,)), ((), ())), preferred_element_type=_F32)


def _dot_tn(a, b):
    return lax.dot_general(a, b, (((0,), (0,)), ((), ())), preferred_element_type=_F32)


def _rms(x, w):
    return x * lax.rsqrt(jnp.mean(x * x, axis=-1, keepdims=True) + EPS) * w


def _sigmoid(x):
    return 1.0 / (1.0 + jnp.exp(-x))


def _silu(x):
    return x * _sigmoid(x)


def _log_sigmoid(x):
    return jnp.minimum(x, 0.0) - jnp.log(1.0 + jnp.exp(-jnp.abs(x)))


def _ffn(x, norm_w, w_gu, w_down):
    h = _rms(x, norm_w).astype(_BF)
    gu = _dot(h, w_gu)
    act = (_silu(gu[:, :D_FF]) * gu[:, D_FF:]).astype(_BF)
    return x + 0.5 * _dot(act, w_down)


def _const_spec(shape):
    nd = len(shape)
    return pl.BlockSpec(shape, lambda *_: (0,) * nd, pipeline_mode=pl.Buffered(1))


def _params(semantics):
    return pltpu.CompilerParams(dimension_semantics=semantics, vmem_limit_bytes=VMEM_LIMIT)


def _pre_even_kernel(x_ref, cos_ref, sin_ref, n1_ref, wgu_ref, wdn_ref, nm_ref, win_ref, qn_ref, kvn_ref,
                     wuq_ref, wuqr_ref, wukv_ref, wg2_ref, bg_ref,
                     x1_ref, q_ref, k_ref, v_ref, gq_ref, gk_ref, la_ref, gv_ref, gr_ref):
    x1 = _ffn(x_ref[...], n1_ref[...], wgu_ref[...], wdn_ref[...])
    x1_ref[...] = x1
    h = _rms(x1, nm_ref[...]).astype(_BF)
    proj = _dot(h, win_ref[...])
    c_q = proj[:, 0:256]
    c_kv = proj[:, 256:384]
    k_r = proj[:, 384:512]
    k_rr = proj[:, 512:640]
    g_q = proj[:, 640:896]
    g_k = proj[:, 896:1152]
    g_v = proj[:, 1152:1664]
    g_a = proj[:, 1664:1792]
    g_r = proj[:, 1792:2304]
    cos = cos_ref[...]
    sin = sin_ref[...]

    cq_n = _rms(c_q, qn_ref[...]).astype(_BF)
    q = _dot(cq_n, wuq_ref[...])
    q_rot = _dot(cq_n, wuqr_ref[...])
    scale = (MLA_NOPE + MLA_ROPE) ** -0.5 * LOG2_E
    for hd in range(MLA_HEADS):
        lo = hd * MLA_QK_PAD
        q_ref[:, lo:lo + LANES] = (q[:, lo:lo + LANES] * scale).astype(_BF)
        q_rope = q[:, lo + LANES:lo + 2 * LANES] * cos + q_rot[:, hd * LANES:(hd + 1) * LANES] * sin
        q_ref[:, lo + LANES:lo + 2 * LANES] = (q_rope * scale).astype(_BF)
    ckv_n = _rms(c_kv, kvn_ref[...]).astype(_BF)
    kv = _dot(ckv_n, wukv_ref[...])
    k_rope = (k_r * cos + k_rr * sin).astype(_BF)
    for hd in range(MLA_HEADS):
        lo = hd * MLA_QK_PAD
        k_ref[:, lo:lo + LANES] = kv[:, hd * LANES:(hd + 1) * LANES].astype(_BF)
        k_ref[:, lo + LANES:lo + 2 * LANES] = k_rope
    v_ref[...] = kv[:, MLA_HEADS * MLA_NOPE:].astype(_BF)

    gq_ref[...] = g_q * (GLA_DK ** -0.5)
    gk_ref[...] = g_k
    gate = _dot(g_a.astype(_BF), wg2_ref[...]) + bg_ref[...]
    la_ref[...] = _log_sigmoid(gate) * (1.0 / GLA_GATE_TAU)
    gv_ref[...] = g_v.astype(_BF)
    gr_ref[...] = g_r.astype(_BF)


def _pre_odd_kernel(x_ref, n1_ref, wgu_ref, wdn_ref, nm_ref, wqkv_ref, x1_ref, q_ref, k_ref, v_ref):
    x1 = _ffn(x_ref[...], n1_ref[...], wgu_ref[...], wdn_ref[...])
    x1_ref[...] = x1
    h = _rms(x1, nm_ref[...]).astype(_BF)
    qkv = _dot(h, wqkv_ref[...])
    q_ref[...] = (qkv[:, :D_MODEL] * (SB_HEAD_DIM ** -0.5)).astype(_BF)
    k_ref[...] = qkv[:, D_MODEL:2 * D_MODEL].astype(_BF)
    v_ref[...] = qkv[:, 2 * D_MODEL:].astype(_BF)


def _row_spec(width, tm):
    return pl.BlockSpec((tm, width), lambda i: (i, 0))


def _pre_even(x, cos_t, sin_t, w):
    t = x.shape[0]
    tm = min(TOKEN_TILE, t)
    consts = [w["n1"], w["wgu"], w["wdn"], w["nm"], w["win"], w["qn"], w["kvn"], w["wuq"], w["wuqr"],
              w["wukv"], w["wg2"], w["bg"]]
    out_widths = [(D_MODEL, _F32), (MLA_HEADS * MLA_QK_PAD, _BF), (MLA_HEADS * MLA_QK_PAD, _BF),
                  (MLA_HEADS * MLA_V, _BF), (GLA_QK, _F32), (GLA_QK, _F32), (GLA_QK, _F32),
                  (GLA_V, _BF), (GLA_V, _BF)]
    return pl.pallas_call(
        _pre_even_kernel,
        grid=(t // tm,),
        in_specs=[_row_spec(D_MODEL, tm), _row_spec(LANES, tm), _row_spec(LANES, tm)]
        + [_const_spec(c.shape) for c in consts],
        out_specs=[_row_spec(wd, tm) for wd, _ in out_widths],
        out_shape=[jax.ShapeDtypeStruct((t, wd), dt) for wd, dt in out_widths],
        compiler_params=_params(("parallel",)),
        name="pre_even",
    )(x, cos_t, sin_t, *consts)


def _pre_odd(x, w):
    t = x.shape[0]
    tm = min(TOKEN_TILE, t)
    consts = [w["n1"], w["wgu"], w["wdn"], w["nm"], w["wqkv"]]
    out_widths = [(D_MODEL, _F32), (D_MODEL, _BF), (D_MODEL, _BF), (D_MODEL, _BF)]
    return pl.pallas_call(
        _pre_odd_kernel,
        grid=(t // tm,),
        in_specs=[_row_spec(D_MODEL, tm)] + [_const_spec(c.shape) for c in consts],
        out_specs=[_row_spec(wd, tm) for wd, _ in out_widths],
        out_shape=[jax.ShapeDtypeStruct((t, wd), dt) for wd, dt in out_widths],
        compiler_params=_params(("parallel",)),
        name="pre_odd",
    )(x, *consts)


def _post_kernel(n_mix, final, *refs):
    x_ref = refs[0]
    mix_refs = refs[1:1 + n_mix]
    p_ref = refs[1 + n_mix]
    wout_refs = refs[2 + n_mix:2 + 2 * n_mix]
    n2_ref, wgu_ref, wdn_ref, np_ref, wpg_ref, wpp_ref, nf_ref, out_ref = refs[2 + 2 * n_mix:]
    x = x_ref[...]
    for m_ref, w_ref in zip(mix_refs, wout_refs):
        x = x + _dot(m_ref[...], w_ref[...])
    x = _ffn(x, n2_ref[...], wgu_ref[...], wdn_ref[...])
    gate = _sigmoid(_dot(_rms(x, np_ref[...]).astype(_BF), wpg_ref[...]))
    x = x + gate * _dot(p_ref[...].astype(_BF), wpp_ref[...])
    if final:
        x = _rms(x, nf_ref[...])
    out_ref[...] = x


def _post(x, mixes, p, wouts, w, final_norm, final):
    t = x.shape[0]
    tm = min(TOKEN_TILE, t)
    consts = list(wouts) + [w["n2"], w["wgu2"], w["wdn2"], w["np"], w["wpg"], w["wpp"], final_norm]
    return pl.pallas_call(
        functools.partial(_post_kernel, len(mixes), final),
        grid=(t // tm,),
        in_specs=[_row_spec(D_MODEL, tm)] + [_row_spec(m.shape[1], tm) for m in mixes]
        + [_row_spec(P_DIM, tm)] + [_const_spec(c.shape) for c in consts],
        out_specs=_row_spec(D_MODEL, tm),
        out_shape=jax.ShapeDtypeStruct((t, D_MODEL), _F32),
        compiler_params=_params(("parallel",)),
        name="post_final" if final else "post",
    )(x, *mixes, p, *consts)


def _mla_kernel(q_ref, k_ref, v_ref, o_ref, m_ref, l_ref, acc_ref):
    qi = pl.program_id(2)
    m_ref[...] = jnp.full(m_ref.shape, MASK_VALUE, _F32)
    l_ref[...] = jnp.zeros(l_ref.shape, _F32)
    acc_ref[...] = jnp.zeros(acc_ref.shape, _F32)
    reps = MLA_TK // LANES

    def step(c, masked):
        start = pl.multiple_of(c * MLA_TK, MLA_TK)
        for hd in range(MLA_HEADS_PER_STEP):
            q = q_ref[0, :, hd * MLA_QK_PAD:(hd + 1) * MLA_QK_PAD]
            k = k_ref[0, pl.ds(start, MLA_TK), hd * MLA_QK_PAD:(hd + 1) * MLA_QK_PAD]
            v = v_ref[0, pl.ds(start, MLA_TK), hd * MLA_V:(hd + 1) * MLA_V]
            s = _dot_nt(q, k)
            if masked:
                row = lax.broadcasted_iota(jnp.int32, s.shape, 0) + qi * MLA_TQ
                col = lax.broadcasted_iota(jnp.int32, s.shape, 1) + start
                s = jnp.where(col <= row, s, MASK_VALUE)
            m_prev = m_ref[hd]
            m_next = jnp.maximum(m_prev, jnp.max(s, axis=1, keepdims=True))
            p = jnp.exp2(s - jnp.concatenate([m_next] * reps, axis=1))
            alpha = jnp.exp2(m_prev - m_next)
            l_ref[hd] = alpha * l_ref[hd] + jnp.sum(p, axis=1, keepdims=True)
            acc_ref[hd] = alpha * acc_ref[hd] + _dot(p.astype(_BF), v)
            m_ref[hd] = m_next

    def body(c, carry):
        step(c, False)
        return carry

    n_full = (qi * MLA_TQ) // MLA_TK
    lax.fori_loop(0, n_full, body, 0)
    step(n_full, True)
    for hd in range(MLA_HEADS_PER_STEP):
        o_ref[0, :, hd * MLA_V:(hd + 1) * MLA_V] = (acc_ref[hd] / l_ref[hd]).astype(o_ref.dtype)


def _mla(q, k, v):
    b, s, _ = q.shape
    hb = MLA_HEADS_PER_STEP
    assert MLA_TK % MLA_TQ == 0 and s % MLA_TK == 0 and MLA_HEADS % hb == 0
    return pl.pallas_call(
        _mla_kernel,
        grid=(b, MLA_HEADS // hb, s // MLA_TQ),
        in_specs=[pl.BlockSpec((1, MLA_TQ, hb * MLA_QK_PAD), lambda bi, hi, qi: (bi, qi, hi)),
                  pl.BlockSpec((1, s, hb * MLA_QK_PAD), lambda bi, hi, qi: (bi, 0, hi)),
                  pl.BlockSpec((1, s, hb * MLA_V), lambda bi, hi, qi: (bi, 0, hi))],
        out_specs=pl.BlockSpec((1, MLA_TQ, hb * MLA_V), lambda bi, hi, qi: (bi, qi, hi)),
        out_shape=jax.ShapeDtypeStruct((b, s, MLA_HEADS * MLA_V), _BF),
        scratch_shapes=[pltpu.VMEM((hb, MLA_TQ, LANES), _F32)] * 3,
        compiler_params=_params(("parallel", "parallel", "arbitrary")),
        name="mla_attention",
    )(q, k, v)


def _sb_kernel(q_ref, k_ref, v_ref, tri_ref, o_ref, suf_ref, acc_ref, zb_ref, kmax_ref):
    qi = pl.program_id(2)
    n_key_chunks = k_ref.shape[1] // SB_TK

    @pl.when(qi == 0)
    def _():
        def key_norm(i, mx):
            kc = k_ref[0, pl.ds(pl.multiple_of(i * SB_TK, SB_TK), SB_TK), :].astype(_F32)
            ssq = jnp.sum(kc * kc, axis=1, keepdims=True)
            return jnp.maximum(mx, jnp.max(ssq, axis=0, keepdims=True))
        mx = lax.fori_loop(0, n_key_chunks, key_norm, jnp.zeros((1, 1), _F32))
        kmax_ref[...] = jnp.broadcast_to(mx, kmax_ref.shape)

    q = q_ref[0]
    qf = q.astype(_F32)
    q_sq = jnp.sum(qf * qf, axis=1, keepdims=True)
    zb_ref[...] = jnp.sqrt(q_sq * kmax_ref[0:1, :]) * SB_BOUND_SLACK
    suf_ref[...] = jnp.zeros(suf_ref.shape, _F32)
    acc_ref[...] = jnp.zeros(acc_ref.shape, _F32)
    tri = tri_ref[...]

    def chunk(c, masked):
        start = pl.multiple_of(c * SB_TK, SB_TK)
        k = k_ref[0, pl.ds(start, SB_TK), :]
        v = v_ref[0, pl.ds(start, SB_TK), :]
        z = _dot_nt(q, k)
        t = jnp.maximum(z, 0.0) + jnp.log(1.0 + jnp.exp(-jnp.abs(z)))
        if masked:
            causal = (lax.broadcasted_iota(jnp.int32, z.shape, 1)
                      < lax.broadcasted_iota(jnp.int32, z.shape, 0))
            t = jnp.where(causal, t, 0.0)
        suf = suf_ref[...]
        log_w = []
        for half in reversed(range(SB_TK // LANES)):
            th = t[:, half * LANES:(half + 1) * LANES]
            t_hi = th.astype(_BF)
            t_lo = (th - t_hi.astype(_F32)).astype(_BF)
            sums = _dot(jnp.concatenate([t_hi, t_lo], axis=1), tri)
            log_w.append(z[:, half * LANES:(half + 1) * LANES] - sums[:, :LANES] - suf)
            suf = suf + sums[:, LANES:]
        a = jnp.exp(jnp.concatenate(log_w[::-1], axis=1))
        if masked:
            a = jnp.where(causal, a, 0.0)
        acc_ref[...] += _dot(a.astype(_BF), v)
        suf_ref[...] = suf
        return (jnp.max(zb_ref[...] - suf) >= SB_EXIT_LOG_WEIGHT).astype(jnp.int32)

    live = chunk(qi, True)

    def cond(carry):
        c, live = carry
        return jnp.logical_and(c >= 0, live > 0)

    def body(carry):
        c, _ = carry
        return c - 1, chunk(c, False)

    lax.while_loop(cond, body, (qi - 1, live))
    o_ref[0] = acc_ref[...].astype(o_ref.dtype)


def _sb_tri():
    r = np.arange(LANES)
    upper = (r[:, None] >= r[None, :]).astype(np.float32)
    half = np.concatenate([upper, np.ones((LANES, LANES), np.float32)], axis=1)
    return jnp.asarray(np.concatenate([half, half], axis=0), dtype=_BF)


def _sb(q, k, v):
    b, s, _ = q.shape
    assert SB_TQ == SB_TK and SB_TK % LANES == 0 and s % SB_TQ == 0
    tri = _sb_tri()
    return pl.pallas_call(
        _sb_kernel,
        grid=(b, SB_HEADS, s // SB_TQ),
        in_specs=[pl.BlockSpec((1, SB_TQ, SB_HEAD_DIM), lambda bi, hi, qi: (bi, qi, hi)),
                  pl.BlockSpec((1, s, SB_HEAD_DIM), lambda bi, hi, qi: (bi, 0, hi)),
                  pl.BlockSpec((1, s, SB_HEAD_DIM), lambda bi, hi, qi: (bi, 0, hi)),
                  _const_spec(tri.shape)],
        out_specs=pl.BlockSpec((1, SB_TQ, SB_HEAD_DIM), lambda bi, hi, qi: (bi, qi, hi)),
        out_shape=jax.ShapeDtypeStruct((b, s, D_MODEL), _BF),
        scratch_shapes=[pltpu.VMEM((SB_TQ, LANES), _F32)] * 3 + [pltpu.VMEM((8, LANES), _F32)],
        compiler_params=_params(("parallel", "parallel", "arbitrary")),
        name="sb_attention",
    )(q, k, v, tri)


def _gla_constants():
    c = GLA_CHUNK
    idx = np.arange(c)
    seg = []
    masks = []
    for lvl in range(GLA_LEVELS):
        s = 1 << lvl
        blk = idx // s
        start = blk * s
        end = start + s - 1
        t = idx[None, :]
        q_side = ((blk % 2 == 1)[:, None] & (t >= start[:, None]) & (t <= idx[:, None]))
        k_side = ((blk % 2 == 0)[:, None] & (t > idx[:, None]) & (t <= end[:, None]))
        seg.append(np.concatenate([q_side, k_side], axis=0))
        pair = ((idx[:, None] // (2 * s)) == (idx[None, :] // (2 * s))) \
            & ((blk % 2 == 1)[:, None]) & ((blk % 2 == 0)[None, :])
        masks.append(np.tile(pair, (GLA_HEADS, 1)))
    t = idx[None, :]
    full = np.concatenate([t <= idx[:, None], t > idx[:, None]], axis=0)
    seg.append(full)
    seg = np.stack(seg).astype(np.float32)
    masks = np.stack(masks).astype(np.float32)
    head_of_qk = np.arange(GLA_QK) // GLA_DK
    head_of_v = np.arange(GLA_V) // GLA_DV
    head_mask = (np.arange(8)[:, None] == head_of_qk[None, :]).astype(np.float32)
    expand = (head_of_qk[:, None] == head_of_v[None, :]).astype(np.float32)
    return (jnp.asarray(seg, _BF), jnp.asarray(masks, _F32), jnp.asarray(head_mask, _F32),
            jnp.asarray(expand, _BF), jnp.asarray(expand.T, _F32))


def _gla_kernel(gq_ref, gk_ref, la_ref, gv_ref, gr_ref, seg_ref, mask_ref, hm_ref, exp_ref, bd_ref,
                gn_ref, o_ref, state_ref, att_ref):
    c = GLA_CHUNK

    @pl.when(pl.program_id(1) == 0)
    def _():
        state_ref[...] = jnp.zeros(state_ref.shape, _F32)

    q = gq_ref[0]
    k = gk_ref[0]
    la = la_ref[0]
    v = gv_ref[0]
    la_hi = la.astype(_BF)
    la_lo = (la - la_hi.astype(_F32)).astype(_BF)
    la2 = jnp.concatenate([la_hi, la_lo], axis=1)

    def seg_exp(i):
        e = _dot(seg_ref[i], la2)
        return jnp.exp(e[:, :GLA_QK] + e[:, GLA_QK:])

    hm = hm_ref[...]
    for lvl in range(GLA_LEVELS):
        w = seg_exp(lvl)
        ql = q * w[:c]
        kl = (k * w[c:]).astype(_BF)
        qs = jnp.concatenate([ql * hm[hd:hd + 1] for hd in range(GLA_HEADS)], axis=0).astype(_BF)
        a = _dot_nt(qs, kl) * mask_ref[lvl]
        if lvl == 0:
            att_ref[...] = a
        else:
            att_ref[...] += a

    w = seg_exp(GLA_LEVELS)
    state = state_ref[...]
    o = _dot_nt((q * w[:c]).astype(_BF), state.astype(_BF))
    o = o + _dot((q * k).astype(_BF), exp_ref[...]) * v.astype(_F32)
    o_intra = [_dot(att_ref[hd * c:(hd + 1) * c, :].astype(_BF), v[:, hd * GLA_DV:(hd + 1) * GLA_DV])
               for hd in range(GLA_HEADS)]
    o = o + jnp.concatenate(o_intra, axis=1)

    upd = _dot_tn(v, (k * w[c:]).astype(_BF))
    state_ref[...] = state * w[c - 1:c] + upd * bd_ref[...]

    gn = gn_ref[...]
    gr = gr_ref[0].astype(_F32)
    for hd in range(GLA_HEADS):
        sl = slice(hd * GLA_DV, (hd + 1) * GLA_DV)
        o_ref[0, :, sl] = (_rms(o[:, sl], gn[:, sl]) * _silu(gr[:, sl])).astype(o_ref.dtype)


def _gla(gq, gk, la, gv, gr, gla_norm):
    b, s, _ = gq.shape
    c = GLA_CHUNK
    assert s % c == 0 and (1 << GLA_LEVELS) == c
    consts = list(_gla_constants()) + [gla_norm]
    tok = lambda wd: pl.BlockSpec((1, c, wd), lambda bi, ci: (bi, ci, 0))
    return pl.pallas_call(
        _gla_kernel,
        grid=(b, s // c),
        in_specs=[tok(GLA_QK), tok(GLA_QK), tok(GLA_QK), tok(GLA_V), tok(GLA_V)]
        + [_const_spec(cn.shape) for cn in consts],
        out_specs=tok(GLA_V),
        out_shape=jax.ShapeDtypeStruct((b, s, GLA_V), _BF),
        scratch_shapes=[pltpu.VMEM((GLA_V, GLA_QK), _F32), pltpu.VMEM((GLA_HEADS * c, c), _F32)],
        compiler_params=_params(("parallel", "arbitrary")),
        name="gla_chunked",
    )(gq, gk, la, gv, gr, *consts)


def _rotate_half_cols(w):
    half = w.shape[1] // 2
    return jnp.concatenate([-w[:, half:], w[:, :half]], axis=1)


def _pad_cols(w, width):
    return jnp.pad(w, ((0, 0), (0, width - w.shape[1])))


def _prep_even_weights(w_in, w_uq, w_ukv, w_gate2):
    splits = np.cumsum([MLA_Q_RANK, MLA_KV_RANK, MLA_ROPE, GLA_QK, GLA_QK, GLA_V, GLA_GATE_RANK])
    c_q, c_kv, k_r, g_q, g_k, g_v, g_a, g_r = jnp.split(w_in, splits, axis=1)
    win = jnp.concatenate([c_q, c_kv, _pad_cols(k_r, LANES), _pad_cols(_rotate_half_cols(k_r), LANES),
                           g_q, g_k, g_v, _pad_cols(g_a, LANES), g_r], axis=1).astype(_BF)
    uq = w_uq.reshape(MLA_Q_RANK, MLA_HEADS, MLA_NOPE + MLA_ROPE)
    uq_rope = uq[:, :, MLA_NOPE:]
    rot = jnp.concatenate([-uq_rope[:, :, MLA_ROPE // 2:], uq_rope[:, :, :MLA_ROPE // 2]], axis=2)
    wuq = jnp.pad(uq, ((0, 0), (0, 0), (0, MLA_QK_PAD - uq.shape[2]))).reshape(MLA_Q_RANK, -1).astype(_BF)
    wuqr = jnp.pad(rot, ((0, 0), (0, 0), (0, LANES - MLA_ROPE))).reshape(MLA_Q_RANK, -1).astype(_BF)
    ukv = w_ukv.reshape(MLA_KV_RANK, MLA_HEADS, MLA_NOPE + MLA_V)
    wukv = jnp.concatenate([ukv[:, :, :MLA_NOPE].reshape(MLA_KV_RANK, -1),
                            ukv[:, :, MLA_NOPE:].reshape(MLA_KV_RANK, -1)], axis=1).astype(_BF)
    wg2 = jnp.pad(w_gate2, ((0, LANES - GLA_GATE_RANK), (0, 0))).astype(_BF)
    return win, wuq, wuqr, wukv, wg2


def _rope_tables(positions):
    half = MLA_ROPE // 2
    inv_freq = 1.0 / (ROPE_THETA ** (jnp.arange(half, dtype=_F32) / half))
    ang = positions.astype(_F32)[..., None] * inv_freq
    cos, sin = jnp.cos(ang), jnp.sin(ang)
    pad = jnp.zeros(cos.shape[:-1] + (LANES - MLA_ROPE,), _F32)
    return (jnp.concatenate([cos, cos, pad], axis=-1).reshape(-1, LANES),
            jnp.concatenate([sin, sin, pad], axis=-1).reshape(-1, LANES))


def kernel(x, p, positions, ffn1_norm, ffn1_w_gu, ffn1_w_down, mix_norm, ffn2_norm, ffn2_w_gu, ffn2_w_down, ple_norm, ple_w_gate, ple_w_proj, ev_w_in, ev_q_norm, ev_kv_norm, ev_w_uq, ev_w_ukv, ev_w_gate2, ev_b_gate, ev_gla_norm, ev_w_out, od_w_qkv, od_w_out, final_norm):
    b, s, d = x.shape
    depth = p.shape[0]
    t = b * s
    cos_t, sin_t = _rope_tables(positions)
    row = lambda a: a.reshape(1, -1).astype(_F32)
    xt = x.reshape(t, d)
    fin = row(final_norm)
    for i in range(depth):
        j = i // 2
        common = dict(n1=row(ffn1_norm[i]), wgu=ffn1_w_gu[i].astype(_BF), wdn=ffn1_w_down[i].astype(_BF),
                      nm=row(mix_norm[i]), n2=row(ffn2_norm[i]), wgu2=ffn2_w_gu[i].astype(_BF),
                      wdn2=ffn2_w_down[i].astype(_BF), np=row(ple_norm[i]), wpg=ple_w_gate[i].astype(_BF),
                      wpp=ple_w_proj[i].astype(_BF))
        if i % 2 == 0:
            win, wuq, wuqr, wukv, wg2 = _prep_even_weights(ev_w_in[j], ev_w_uq[j], ev_w_ukv[j], ev_w_gate2[j])
            w = dict(common, win=win, wuq=wuq, wuqr=wuqr, wukv=wukv, wg2=wg2, qn=row(ev_q_norm[j]),
                     kvn=row(ev_kv_norm[j]), bg=row(ev_b_gate[j]))
            x1, q, k, v, gq, gk, la, gv, gr = _pre_even(xt, cos_t, sin_t, w)
            seq = lambda a: a.reshape(b, s, a.shape[-1])
            o_mla = _mla(seq(q), seq(k), seq(v))
            o_gla = _gla(seq(gq), seq(gk), seq(la), seq(gv), seq(gr), row(ev_gla_norm[j]))
            mixes = [o_mla.reshape(t, -1), o_gla.reshape(t, -1)]
            w_out = ev_w_out[j].astype(_BF)
            wouts = [w_out[:MLA_HEADS * MLA_V], w_out[MLA_HEADS * MLA_V:]]
        else:
            w = dict(common, wqkv=od_w_qkv[j].astype(_BF))
            x1, q, k, v = _pre_odd(xt, w)
            seq = lambda a: a.reshape(b, s, a.shape[-1])
            o_sb = _sb(seq(q), seq(k), seq(v))
            mixes = [o_sb.reshape(t, -1)]
            wouts = [od_w_out[j].astype(_BF)]
        xt = _post(x1, mixes, p[i].reshape(t, -1), wouts, w, fin, i == depth - 1)
    return xt.reshape(b, s, d)
```

```python
import functools

import numpy as np
import jax
import jax.numpy as jnp
from jax import lax
from jax.experimental import pallas as pl
from jax.experimental.pallas import tpu as pltpu

D_MODEL = 1024
P_DIM = 256
EPS = 1e-6
D_FF = 1408
MLA_HEADS = 4
MLA_Q_RANK = 256
MLA_KV_RANK = 128
MLA_NOPE = 128
MLA_ROPE = 64
MLA_V = 128
ROPE_THETA = 10000.0
GLA_HEADS = 4
GLA_DK = 64
GLA_DV = 128
GLA_GATE_RANK = 16
GLA_GATE_TAU = 16.0
SB_HEADS = 8
SB_HEAD_DIM = D_MODEL // SB_HEADS

LANES = 128
MLA_QK_PAD = 2 * LANES
GLA_QK = GLA_HEADS * GLA_DK
GLA_V = GLA_HEADS * GLA_DV

TOKEN_TILE = 512
MLA_TQ = 512
MLA_TK = 512
MLA_HEADS_PER_STEP = 4
SB_TQ = 256
SB_TK = 256
SB_HEADS_PER_STEP = 4
SB_EXIT_LOG_WEIGHT = -106.0
SB_BOUND_SLACK = 1.001
GLA_CHUNK = 128
GLA_LEVELS = 7
VMEM_LIMIT = 56 * 1024 * 1024
MASK_VALUE = -1e30
LOG2_E = 1.4426950408889634

_BF = jnp.bfloat16
_F32 = jnp.float32


def _dot(a, b):
    return jnp.dot(a, b, preferred_element_type=_F32)


def _dot_nt(a, b):
    return lax.dot_general(a, b, (((1,), (1,)), ((), ())), preferred_element_type=_F32)


def _dot_tn(a, b):
    return lax.dot_general(a, b, (((0,), (0,)), ((), ())), preferred_element_type=_F32)


def _rms(x, w):
    return x * lax.rsqrt(jnp.mean(x * x, axis=-1, keepdims=True) + EPS) * w


def _sigmoid(x):
    return 1.0 / (1.0 + jnp.exp(-x))


def _silu(x):
    return x * _sigmoid(x)


def _log_sigmoid(x):
    return jnp.minimum(x, 0.0) - jnp.log(1.0 + jnp.exp(-jnp.abs(x)))


def _ffn(x, norm_w, w_gu, w_down):
    h = _rms(x, norm_w).astype(_BF)
    gu = _dot(h, w_gu)
    act = (_silu(gu[:, :D_FF]) * gu[:, D_FF:]).astype(_BF)
    return x + 0.5 * _dot(act, w_down)


def _const_spec(shape):
    nd = len(shape)
    return pl.BlockSpec(shape, lambda *_: (0,) * nd, pipeline_mode=pl.Buffered(1))


def _params(semantics):
    return pltpu.CompilerParams(dimension_semantics=semantics, vmem_limit_bytes=VMEM_LIMIT)


def _pre_even_kernel(x_ref, cos_ref, sin_ref, n1_ref, wgu_ref, wdn_ref, nm_ref, win_ref, qn_ref, kvn_ref,
                     wuq_ref, wuqr_ref, wukt_ref, wuv_ref, wg2_ref, bg_ref,
                     x1_ref, q_ref, kt_ref, v_ref, gq_ref, gk_ref, la_ref, gv_ref, gr_ref):
    x1 = _ffn(x_ref[...], n1_ref[...], wgu_ref[...], wdn_ref[...])
    x1_ref[...] = x1
    h = _rms(x1, nm_ref[...]).astype(_BF)
    proj = _dot(h, win_ref[...])
    c_q = proj[:, 0:256]
    c_kv = proj[:, 256:384]
    k_r = proj[:, 384:512]
    k_rr = proj[:, 512:640]
    g_q = proj[:, 640:896]
    g_k = proj[:, 896:1152]
    g_v = proj[:, 1152:1664]
    g_a = proj[:, 1664:1792]
    g_r = proj[:, 1792:2304]
    cos = cos_ref[...]
    sin = sin_ref[...]

    cq_n = _rms(c_q, qn_ref[...]).astype(_BF)
    q = _dot(cq_n, wuq_ref[...])
    q_rot = _dot(cq_n, wuqr_ref[...])
    scale = (MLA_NOPE + MLA_ROPE) ** -0.5 * LOG2_E
    for hd in range(MLA_HEADS):
        lo = hd * MLA_QK_PAD
        q_ref[:, lo:lo + LANES] = (q[:, lo:lo + LANES] * scale).astype(_BF)
        q_rope = q[:, lo + LANES:lo + 2 * LANES] * cos + q_rot[:, hd * LANES:(hd + 1) * LANES] * sin
        q_ref[:, lo + LANES:lo + 2 * LANES] = (q_rope * scale).astype(_BF)
    ckv_n = _rms(c_kv, kvn_ref[...]).astype(_BF)
    k_nope_t = _dot_nt(wukt_ref[...], ckv_n)
    k_rope_t = (k_r * cos + k_rr * sin).T.astype(_BF)
    for hd in range(MLA_HEADS):
        lo = hd * MLA_QK_PAD
        kt_ref[0, lo:lo + LANES, :] = k_nope_t[hd * LANES:(hd + 1) * LANES].astype(_BF)
        kt_ref[0, lo + LANES:lo + 2 * LANES, :] = k_rope_t
    v_ref[...] = _dot(ckv_n, wuv_ref[...]).astype(_BF)

    gq_ref[...] = g_q * (GLA_DK ** -0.5)
    gk_ref[...] = g_k
    gate = _dot(g_a.astype(_BF), wg2_ref[...]) + bg_ref[...]
    la_ref[...] = _log_sigmoid(gate) * (1.0 / GLA_GATE_TAU)
    gv_ref[...] = g_v.astype(_BF)
    gr_ref[...] = g_r.astype(_BF)


def _pre_odd_kernel(x_ref, n1_ref, wgu_ref, wdn_ref, nm_ref, wq_ref, wkt_ref, wv_ref,
                    x1_ref, q_ref, kt_ref, v_ref):
    x1 = _ffn(x_ref[...], n1_ref[...], wgu_ref[...], wdn_ref[...])
    x1_ref[...] = x1
    h = _rms(x1, nm_ref[...]).astype(_BF)
    q_ref[...] = (_dot(h, wq_ref[...]) * (SB_HEAD_DIM ** -0.5)).astype(_BF)
    kt = _dot_nt(wkt_ref[...], h)
    for j in range(kt_ref.shape[0]):
        kt_ref[j] = kt[:, j * SB_TK:(j + 1) * SB_TK].astype(_BF)
    v_ref[...] = _dot(h, wv_ref[...]).astype(_BF)


def _row_spec(width, tm):
    return pl.BlockSpec((tm, width), lambda i: (i, 0))


def _pre_even(x, cos_t, sin_t, w):
    t = x.shape[0]
    tm = min(TOKEN_TILE, t)
    consts = [w["n1"], w["wgu"], w["wdn"], w["nm"], w["win"], w["qn"], w["kvn"], w["wuq"], w["wuqr"],
              w["wukt"], w["wuv"], w["wg2"], w["bg"]]
    assert tm == MLA_TK
    kt_rows = MLA_HEADS * MLA_QK_PAD
    out_widths = [(D_MODEL, _F32), (MLA_HEADS * MLA_QK_PAD, _BF), None,
                  (MLA_HEADS * MLA_V, _BF), (GLA_QK, _F32), (GLA_QK, _F32), (GLA_QK, _F32),
                  (GLA_V, _BF), (GLA_V, _BF)]
    out_specs = [pl.BlockSpec((1, kt_rows, tm), lambda i: (i, 0, 0)) if o is None else _row_spec(o[0], tm)
                 for o in out_widths]
    out_shape = [jax.ShapeDtypeStruct((t // tm, kt_rows, tm), _BF) if o is None
                 else jax.ShapeDtypeStruct((t, o[0]), o[1]) for o in out_widths]
    return pl.pallas_call(
        _pre_even_kernel,
        grid=(t // tm,),
        in_specs=[_row_spec(D_MODEL, tm), _row_spec(LANES, tm), _row_spec(LANES, tm)]
        + [_const_spec(c.shape) for c in consts],
        out_specs=out_specs,
        out_shape=out_shape,
        compiler_params=_params(("parallel",)),
        name="pre_even",
    )(x, cos_t, sin_t, *consts)


def _pre_odd(x, w):
    t = x.shape[0]
    tm = min(TOKEN_TILE, t)
    consts = [w["n1"], w["wgu"], w["wdn"], w["nm"], w["wq"], w["wkt"], w["wv"]]
    assert tm % SB_TK == 0
    out_widths = [(D_MODEL, _F32), (D_MODEL, _BF), None, (D_MODEL, _BF)]
    out_specs = [pl.BlockSpec((tm // SB_TK, D_MODEL, SB_TK), lambda i: (i, 0, 0)) if o is None
                 else _row_spec(o[0], tm) for o in out_widths]
    out_shape = [jax.ShapeDtypeStruct((t // SB_TK, D_MODEL, SB_TK), _BF) if o is None
                 else jax.ShapeDtypeStruct((t, o[0]), o[1]) for o in out_widths]
    return pl.pallas_call(
        _pre_odd_kernel,
        grid=(t // tm,),
        in_specs=[_row_spec(D_MODEL, tm)] + [_const_spec(c.shape) for c in consts],
        out_specs=out_specs,
        out_shape=out_shape,
        compiler_params=_params(("parallel",)),
        name="pre_odd",
    )(x, *consts)


def _post_kernel(n_mix, final, *refs):
    x_ref = refs[0]
    mix_refs = refs[1:1 + n_mix]
    p_ref = refs[1 + n_mix]
    wout_refs = refs[2 + n_mix:2 + 2 * n_mix]
    n2_ref, wgu_ref, wdn_ref, np_ref, wpg_ref, wpp_ref, nf_ref, out_ref = refs[2 + 2 * n_mix:]
    x = x_ref[...]
    for m_ref, w_ref in zip(mix_refs, wout_refs):
        x = x + _dot(m_ref[...], w_ref[...])
    x = _ffn(x, n2_ref[...], wgu_ref[...], wdn_ref[...])
    gate = _sigmoid(_dot(_rms(x, np_ref[...]).astype(_BF), wpg_ref[...]))
    x = x + gate * _dot(p_ref[...].astype(_BF), wpp_ref[...])
    if final:
        x = _rms(x, nf_ref[...])
    out_ref[...] = x


def _post(x, mixes, p, wouts, w, final_norm, final):
    t = x.shape[0]
    tm = min(TOKEN_TILE, t)
    consts = list(wouts) + [w["n2"], w["wgu2"], w["wdn2"], w["np"], w["wpg"], w["wpp"], final_norm]
    return pl.pallas_call(
        functools.partial(_post_kernel, len(mixes), final),
        grid=(t // tm,),
        in_specs=[_row_spec(D_MODEL, tm)] + [_row_spec(m.shape[1], tm) for m in mixes]
        + [_row_spec(P_DIM, tm)] + [_const_spec(c.shape) for c in consts],
        out_specs=_row_spec(D_MODEL, tm),
        out_shape=jax.ShapeDtypeStruct((t, D_MODEL), _F32),
        compiler_params=_params(("parallel",)),
        name="post_final" if final else "post",
    )(x, *mixes, p, *consts)


def _mla_kernel(q_ref, kt_ref, v_ref, o_ref, m_ref, l_ref, acc_ref):
    qi = pl.program_id(2)
    m_ref[...] = jnp.full(m_ref.shape, MASK_VALUE, _F32)
    l_ref[...] = jnp.zeros(l_ref.shape, _F32)
    acc_ref[...] = jnp.zeros(acc_ref.shape, _F32)
    reps = MLA_TK // LANES

    def step(c, masked):
        start = pl.multiple_of(c * MLA_TK, MLA_TK)
        for hd in range(MLA_HEADS_PER_STEP):
            q = q_ref[0, :, hd * MLA_QK_PAD:(hd + 1) * MLA_QK_PAD]
            kt = kt_ref[c, hd * MLA_QK_PAD:(hd + 1) * MLA_QK_PAD, :]
            v = v_ref[0, pl.ds(start, MLA_TK), hd * MLA_V:(hd + 1) * MLA_V]
            s = _dot(q, kt)
            if masked:
                row = lax.broadcasted_iota(jnp.int32, s.shape, 0) + qi * MLA_TQ
                col = lax.broadcasted_iota(jnp.int32, s.shape, 1) + start
                s = jnp.where(col <= row, s, MASK_VALUE)
            m_prev = m_ref[hd]
            m_next = jnp.maximum(m_prev, jnp.max(s, axis=1, keepdims=True))
            p = jnp.exp2(s - jnp.concatenate([m_next] * reps, axis=1))
            alpha = jnp.exp2(m_prev - m_next)
            l_ref[hd] = alpha * l_ref[hd] + jnp.sum(p, axis=1, keepdims=True)
            acc_ref[hd] = alpha * acc_ref[hd] + _dot(p.astype(_BF), v)
            m_ref[hd] = m_next

    def body(c, carry):
        step(c, False)
        return carry

    n_full = (qi * MLA_TQ) // MLA_TK
    lax.fori_loop(0, n_full, body, 0)
    step(n_full, True)
    for hd in range(MLA_HEADS_PER_STEP):
        o_ref[0, :, hd * MLA_V:(hd + 1) * MLA_V] = (acc_ref[hd] / l_ref[hd]).astype(o_ref.dtype)


def _mla(q, kt, v):
    b, s, _ = q.shape
    hb = MLA_HEADS_PER_STEP
    n_chunks = s // MLA_TK
    assert MLA_TK % MLA_TQ == 0 and s % MLA_TK == 0 and MLA_HEADS % hb == 0
    resident = dict(pipeline_mode=pl.Buffered(1))
    return pl.pallas_call(
        _mla_kernel,
        grid=(b, MLA_HEADS // hb, s // MLA_TQ),
        in_specs=[pl.BlockSpec((1, MLA_TQ, hb * MLA_QK_PAD), lambda bi, hi, qi: (bi, qi, hi)),
                  pl.BlockSpec((n_chunks, hb * MLA_QK_PAD, MLA_TK), lambda bi, hi, qi: (bi, hi, 0), **resident),
                  pl.BlockSpec((1, s, hb * MLA_V), lambda bi, hi, qi: (bi, 0, hi), **resident)],
        out_specs=pl.BlockSpec((1, MLA_TQ, hb * MLA_V), lambda bi, hi, qi: (bi, qi, hi)),
        out_shape=jax.ShapeDtypeStruct((b, s, MLA_HEADS * MLA_V), _BF),
        scratch_shapes=[pltpu.VMEM((hb, MLA_TQ, LANES), _F32)] * 3,
        compiler_params=_params(("parallel", "parallel", "arbitrary")),
        name="mla_attention",
    )(q, kt, v)


def _sb_kernel(q_ref, kt_ref, v_ref, tri_ref, o_ref, suf_ref, acc_ref, zb_ref, kmax_ref):
    qi = pl.program_id(2)
    n_key_chunks = kt_ref.shape[0]
    heads = range(SB_HEADS_PER_STEP)
    lanes_of = lambda hd: slice(hd * SB_HEAD_DIM, (hd + 1) * SB_HEAD_DIM)

    @pl.when(qi == 0)
    def _():
        def key_norm(i, mx):
            kc = kt_ref[i].astype(_F32)
            kc = kc * kc
            ssq = [jnp.sum(kc[lanes_of(hd)], axis=0, keepdims=True) for hd in heads]
            return tuple(jnp.maximum(m, s) for m, s in zip(mx, ssq))
        mx = lax.fori_loop(0, n_key_chunks, key_norm, tuple(jnp.zeros((1, SB_TK), _F32) for _ in heads))
        for hd in heads:
            kmax_ref[hd] = jnp.broadcast_to(jnp.max(mx[hd], axis=1, keepdims=True), kmax_ref.shape[1:])

    for hd in heads:
        qf = q_ref[0, :, lanes_of(hd)].astype(_F32)
        q_sq = jnp.sum(qf * qf, axis=1, keepdims=True)
        zb_ref[hd] = jnp.sqrt(q_sq * kmax_ref[hd, 0:1, :]) * SB_BOUND_SLACK
    suf_ref[...] = jnp.zeros(suf_ref.shape, _F32)
    acc_ref[...] = jnp.zeros(acc_ref.shape, _F32)
    tri = tri_ref[...]

    def chunk(c, masked):
        start = pl.multiple_of(c * SB_TK, SB_TK)
        slack = None
        for hd in heads:
            q = q_ref[0, :, lanes_of(hd)]
            kt = kt_ref[c, lanes_of(hd), :]
            v = v_ref[0, pl.ds(start, SB_TK), lanes_of(hd)]
            z = _dot(q, kt)
            t = jnp.maximum(z, 0.0) + jnp.log(1.0 + jnp.exp(-jnp.abs(z)))
            if masked:
                causal = (lax.broadcasted_iota(jnp.int32, z.shape, 1)
                          < lax.broadcasted_iota(jnp.int32, z.shape, 0))
                t = jnp.where(causal, t, 0.0)
            suf = suf_ref[hd]
            log_w = []
            for half in reversed(range(SB_TK // LANES)):
                th = t[:, half * LANES:(half + 1) * LANES]
                t_hi = th.astype(_BF)
                t_lo = (th - t_hi.astype(_F32)).astype(_BF)
                sums = _dot(jnp.concatenate([t_hi, t_lo], axis=1), tri)
                log_w.append(z[:, half * LANES:(half + 1) * LANES] - sums[:, :LANES] - suf)
                suf = suf + sums[:, LANES:]
            a = jnp.exp(jnp.concatenate(log_w[::-1], axis=1))
            if masked:
                a = jnp.where(causal, a, 0.0)
            acc_ref[hd] += _dot(a.astype(_BF), v)
            suf_ref[hd] = suf
            head_slack = zb_ref[hd] - suf
            slack = head_slack if slack is None else jnp.maximum(slack, head_slack)
        return (jnp.max(slack) >= SB_EXIT_LOG_WEIGHT).astype(jnp.int32)

    live = chunk(qi, True)

    def cond(carry):
        c, live = carry
        return jnp.logical_and(c >= 0, live > 0)

    def body(carry):
        c, _ = carry
        return c - 1, chunk(c, False)

    lax.while_loop(cond, body, (qi - 1, live))
    for hd in heads:
        o_ref[0, :, lanes_of(hd)] = acc_ref[hd].astype(o_ref.dtype)


def _sb_tri():
    r = np.arange(LANES)
    upper = (r[:, None] >= r[None, :]).astype(np.float32)
    half = np.concatenate([upper, np.ones((LANES, LANES), np.float32)], axis=1)
    return jnp.asarray(np.concatenate([half, half], axis=0), dtype=_BF)


def _sb(q, kt, v):
    b, s, _ = q.shape
    hb = SB_HEADS_PER_STEP
    assert SB_TQ == SB_TK and SB_TK % LANES == 0 and s % SB_TQ == 0 and SB_HEADS % hb == 0
    tri = _sb_tri()
    width = hb * SB_HEAD_DIM
    resident = dict(pipeline_mode=pl.Buffered(1))
    return pl.pallas_call(
        _sb_kernel,
        grid=(b, SB_HEADS // hb, s // SB_TQ),
        in_specs=[pl.BlockSpec((1, SB_TQ, width), lambda bi, hi, qi: (bi, qi, hi)),
                  pl.BlockSpec((s // SB_TK, width, SB_TK), lambda bi, hi, qi: (bi, hi, 0), **resident),
                  pl.BlockSpec((1, s, width), lambda bi, hi, qi: (bi, 0, hi), **resident),
                  _const_spec(tri.shape)],
        out_specs=pl.BlockSpec((1, SB_TQ, width), lambda bi, hi, qi: (bi, qi, hi)),
        out_shape=jax.ShapeDtypeStruct((b, s, D_MODEL), _BF),
        scratch_shapes=[pltpu.VMEM((hb, SB_TQ, LANES), _F32)] * 3 + [pltpu.VMEM((hb, 8, LANES), _F32)],
        compiler_params=_params(("parallel", "parallel", "arbitrary")),
        name="sb_attention",
    )(q, kt, v, tri)


def _gla_constants():
    c = GLA_CHUNK
    idx = np.arange(c)
    seg = []
    masks = []
    for lvl in range(GLA_LEVELS):
        s = 1 << lvl
        blk = idx // s
        start = blk * s
        end = start + s - 1
        t = idx[None, :]
        q_side = ((blk % 2 == 1)[:, None] & (t >= start[:, None]) & (t <= idx[:, None]))
        k_side = ((blk % 2 == 0)[:, None] & (t > idx[:, None]) & (t <= end[:, None]))
        seg.append(np.concatenate([q_side, k_side], axis=0))
        pair = ((idx[:, None] // (2 * s)) == (idx[None, :] // (2 * s))) \
            & ((blk % 2 == 1)[:, None]) & ((blk % 2 == 0)[None, :])
        masks.append(np.tile(pair, (GLA_HEADS, 1)))
    t = idx[None, :]
    full = np.concatenate([t <= idx[:, None], t > idx[:, None]], axis=0)
    seg.append(full)
    seg = np.stack(seg).astype(np.float32)
    masks = np.stack(masks).astype(np.float32)
    head_of_qk = np.arange(GLA_QK) // GLA_DK
    head_of_v = np.arange(GLA_V) // GLA_DV
    head_mask = (np.arange(8)[:, None] == head_of_qk[None, :]).astype(np.float32)
    expand = (head_of_qk[:, None] == head_of_v[None, :]).astype(np.float32)
    return (jnp.asarray(seg, _BF), jnp.asarray(masks, _F32), jnp.asarray(head_mask, _F32),
            jnp.asarray(expand, _BF), jnp.asarray(expand.T, _F32))


def _gla_kernel(gq_ref, gk_ref, la_ref, gv_ref, gr_ref, seg_ref, mask_ref, hm_ref, exp_ref, bd_ref,
                gn_ref, o_ref, state_ref, att_ref):
    c = GLA_CHUNK

    @pl.when(pl.program_id(1) == 0)
    def _():
        state_ref[...] = jnp.zeros(state_ref.shape, _F32)

    q = gq_ref[0]
    k = gk_ref[0]
    la = la_ref[0]
    v = gv_ref[0]
    la_hi = la.astype(_BF)
    la_lo = (la - la_hi.astype(_F32)).astype(_BF)
    la2 = jnp.concatenate([la_hi, la_lo], axis=1)

    def seg_exp(i):
        e = _dot(seg_ref[i], la2)
        return jnp.exp(e[:, :GLA_QK] + e[:, GLA_QK:])

    hm = hm_ref[...]
    for lvl in range(GLA_LEVELS):
        w = seg_exp(lvl)
        ql = q * w[:c]
        kl = (k * w[c:]).astype(_BF)
        qs = jnp.concatenate([ql * hm[hd:hd + 1] for hd in range(GLA_HEADS)], axis=0).astype(_BF)
        a = _dot_nt(qs, kl) * mask_ref[lvl]
        if lvl == 0:
            att_ref[...] = a
        else:
            att_ref[...] += a

    w = seg_exp(GLA_LEVELS)
    state = state_ref[...]
    o = _dot_nt((q * w[:c]).astype(_BF), state.astype(_BF))
    o = o + _dot((q * k).astype(_BF), exp_ref[...]) * v.astype(_F32)
    o_intra = [_dot(att_ref[hd * c:(hd + 1) * c, :].astype(_BF), v[:, hd * GLA_DV:(hd + 1) * GLA_DV])
               for hd in range(GLA_HEADS)]
    o = o + jnp.concatenate(o_intra, axis=1)

    upd = _dot_tn(v, (k * w[c:]).astype(_BF))
    state_ref[...] = state * w[c - 1:c] + upd * bd_ref[...]

    gn = gn_ref[...]
    gr = gr_ref[0].astype(_F32)
    for hd in range(GLA_HEADS):
        sl = slice(hd * GLA_DV, (hd + 1) * GLA_DV)
        o_ref[0, :, sl] = (_rms(o[:, sl], gn[:, sl]) * _silu(gr[:, sl])).astype(o_ref.dtype)


def _gla(gq, gk, la, gv, gr, gla_norm):
    b, s, _ = gq.shape
    c = GLA_CHUNK
    assert s % c == 0 and (1 << GLA_LEVELS) == c
    consts = list(_gla_constants()) + [gla_norm]
    tok = lambda wd: pl.BlockSpec((1, c, wd), lambda bi, ci: (bi, ci, 0))
    return pl.pallas_call(
        _gla_kernel,
        grid=(b, s // c),
        in_specs=[tok(GLA_QK), tok(GLA_QK), tok(GLA_QK), tok(GLA_V), tok(GLA_V)]
        + [_const_spec(cn.shape) for cn in consts],
        out_specs=tok(GLA_V),
        out_shape=jax.ShapeDtypeStruct((b, s, GLA_V), _BF),
        scratch_shapes=[pltpu.VMEM((GLA_V, GLA_QK), _F32), pltpu.VMEM((GLA_HEADS * c, c), _F32)],
        compiler_params=_params(("parallel", "arbitrary")),
        name="gla_chunked",
    )(gq, gk, la, gv, gr, *consts)


def _rotate_half_cols(w):
    half = w.shape[1] // 2
    return jnp.concatenate([-w[:, half:], w[:, :half]], axis=1)


def _pad_cols(w, width):
    return jnp.pad(w, ((0, 0), (0, width - w.shape[1])))


def _prep_even_weights(w_in, w_uq, w_ukv, w_gate2):
    splits = np.cumsum([MLA_Q_RANK, MLA_KV_RANK, MLA_ROPE, GLA_QK, GLA_QK, GLA_V, GLA_GATE_RANK])
    c_q, c_kv, k_r, g_q, g_k, g_v, g_a, g_r = jnp.split(w_in, splits, axis=1)
    win = jnp.concatenate([c_q, c_kv, _pad_cols(k_r, LANES), _pad_cols(_rotate_half_cols(k_r), LANES),
                           g_q, g_k, g_v, _pad_cols(g_a, LANES), g_r], axis=1).astype(_BF)
    uq = w_uq.reshape(MLA_Q_RANK, MLA_HEADS, MLA_NOPE + MLA_ROPE)
    uq_rope = uq[:, :, MLA_NOPE:]
    rot = jnp.concatenate([-uq_rope[:, :, MLA_ROPE // 2:], uq_rope[:, :, :MLA_ROPE // 2]], axis=2)
    wuq = jnp.pad(uq, ((0, 0), (0, 0), (0, MLA_QK_PAD - uq.shape[2]))).reshape(MLA_Q_RANK, -1).astype(_BF)
    wuqr = jnp.pad(rot, ((0, 0), (0, 0), (0, LANES - MLA_ROPE))).reshape(MLA_Q_RANK, -1).astype(_BF)
    ukv = w_ukv.reshape(MLA_KV_RANK, MLA_HEADS, MLA_NOPE + MLA_V)
    wukt = ukv[:, :, :MLA_NOPE].reshape(MLA_KV_RANK, -1).T.astype(_BF)
    wuv = ukv[:, :, MLA_NOPE:].reshape(MLA_KV_RANK, -1).astype(_BF)
    wg2 = jnp.pad(w_gate2, ((0, LANES - GLA_GATE_RANK), (0, 0))).astype(_BF)
    return dict(win=win, wuq=wuq, wuqr=wuqr, wukt=wukt, wuv=wuv, wg2=wg2)


def _rope_tables(positions):
    half = MLA_ROPE // 2
    inv_freq = 1.0 / (ROPE_THETA ** (jnp.arange(half, dtype=_F32) / half))
    ang = positions.astype(_F32)[..., None] * inv_freq
    cos, sin = jnp.cos(ang), jnp.sin(ang)
    pad = jnp.zeros(cos.shape[:-1] + (LANES - MLA_ROPE,), _F32)
    return (jnp.concatenate([cos, cos, pad], axis=-1).reshape(-1, LANES),
            jnp.concatenate([sin, sin, pad], axis=-1).reshape(-1, LANES))


def kernel(x, p, positions, ffn1_norm, ffn1_w_gu, ffn1_w_down, mix_norm, ffn2_norm, ffn2_w_gu, ffn2_w_down, ple_norm, ple_w_gate, ple_w_proj, ev_w_in, ev_q_norm, ev_kv_norm, ev_w_uq, ev_w_ukv, ev_w_gate2, ev_b_gate, ev_gla_norm, ev_w_out, od_w_qkv, od_w_out, final_norm):
    b, s, d = x.shape
    depth = p.shape[0]
    t = b * s
    cos_t, sin_t = _rope_tables(positions)
    row = lambda a: a.reshape(1, -1).astype(_F32)
    xt = x.reshape(t, d)
    fin = row(final_norm)
    for i in range(depth):
        j = i // 2
        common = dict(n1=row(ffn1_norm[i]), wgu=ffn1_w_gu[i].astype(_BF), wdn=ffn1_w_down[i].astype(_BF),
                      nm=row(mix_norm[i]), n2=row(ffn2_norm[i]), wgu2=ffn2_w_gu[i].astype(_BF),
                      wdn2=ffn2_w_down[i].astype(_BF), np=row(ple_norm[i]), wpg=ple_w_gate[i].astype(_BF),
                      wpp=ple_w_proj[i].astype(_BF))
        if i % 2 == 0:
            w = dict(common, **_prep_even_weights(ev_w_in[j], ev_w_uq[j], ev_w_ukv[j], ev_w_gate2[j]),
                     qn=row(ev_q_norm[j]), kvn=row(ev_kv_norm[j]), bg=row(ev_b_gate[j]))
            x1, q, kt, v, gq, gk, la, gv, gr = _pre_even(xt, cos_t, sin_t, w)
            seq = lambda a: a.reshape(b, s, a.shape[-1])
            o_mla = _mla(seq(q), kt, seq(v))
            o_gla = _gla(seq(gq), seq(gk), seq(la), seq(gv), seq(gr), row(ev_gla_norm[j]))
            mixes = [o_mla.reshape(t, -1), o_gla.reshape(t, -1)]
            w_out = ev_w_out[j].astype(_BF)
            wouts = [w_out[:MLA_HEADS * MLA_V], w_out[MLA_HEADS * MLA_V:]]
        else:
            wqkv = od_w_qkv[j].astype(_BF)
            w = dict(common, wq=wqkv[:, :D_MODEL], wkt=wqkv[:, D_MODEL:2 * D_MODEL].T, wv=wqkv[:, 2 * D_MODEL:])
            x1, q, kt, v = _pre_odd(xt, w)
            seq = lambda a: a.reshape(b, s, a.shape[-1])
            o_sb = _sb(seq(q), kt, seq(v))
            mixes = [o_sb.reshape(t, -1)]
            wouts = [od_w_out[j].astype(_BF)]
        xt = _post(x1, mixes, p[i].reshape(t, -1), wouts, w, fin, i == depth - 1)
    return xt.reshape(b, s, d)
```

```python
import functools

import numpy as np
import jax
import jax.numpy as jnp
from jax import lax
from jax.experimental import pallas as pl
from jax.experimental.pallas import tpu as pltpu

D_MODEL = 1024
P_DIM = 256
EPS = 1e-6
D_FF = 1408
MLA_HEADS = 4
MLA_Q_RANK = 256
MLA_KV_RANK = 128
MLA_NOPE = 128
MLA_ROPE = 64
MLA_V = 128
ROPE_THETA = 10000.0
GLA_HEADS = 4
GLA_DK = 64
GLA_DV = 128
GLA_GATE_RANK = 16
GLA_GATE_TAU = 16.0
SB_HEADS = 8
SB_HEAD_DIM = D_MODEL // SB_HEADS

LANES = 128
MLA_QK_PAD = 2 * LANES
GLA_QK = GLA_HEADS * GLA_DK
GLA_V = GLA_HEADS * GLA_DV

TOKEN_TILE = 512
MLA_TQ = 512
MLA_TK = 512
MLA_HEADS_PER_STEP = 4
SB_TQ = 256
SB_TK = 256
SB_HEADS_PER_STEP = 4
SB_EXIT_LOG_WEIGHT = -106.0
SB_BOUND_SLACK = 1.001
GLA_CHUNK = 128
GLA_CHUNKS_PER_STEP = 2
GLA_LEVELS = 7
VMEM_LIMIT = 56 * 1024 * 1024
MASK_VALUE = -1e30
LOG2_E = 1.4426950408889634

_BF = jnp.bfloat16
_F32 = jnp.float32


def _dot(a, b):
    return jnp.dot(a, b, preferred_element_type=_F32)


def _dot_nt(a, b):
    return lax.dot_general(a, b, (((1,), (1,)), ((), ())), preferred_element_type=_F32)


def _dot_tn(a, b):
    return lax.dot_general(a, b, (((0,), (0,)), ((), ())), preferred_element_type=_F32)


def _rms(x, w):
    return x * lax.rsqrt(jnp.mean(x * x, axis=-1, keepdims=True) + EPS) * w


def _sigmoid(x):
    return 1.0 / (1.0 + jnp.exp(-x))


def _silu(x):
    return x * _sigmoid(x)


def _log_sigmoid(x):
    return jnp.minimum(x, 0.0) - jnp.log(1.0 + jnp.exp(-jnp.abs(x)))


def _ffn(x, norm_w, w_gu, w_down):
    h = _rms(x, norm_w).astype(_BF)
    gu = _dot(h, w_gu)
    act = (_silu(gu[:, :D_FF]) * gu[:, D_FF:]).astype(_BF)
    return x + 0.5 * _dot(act, w_down)


def _const_spec(shape):
    nd = len(shape)
    return pl.BlockSpec(shape, lambda *_: (0,) * nd, pipeline_mode=pl.Buffered(1))


def _params(semantics):
    return pltpu.CompilerParams(dimension_semantics=semantics, vmem_limit_bytes=VMEM_LIMIT)


def _pre_even_kernel(x_ref, pos_ref, freq_ref, n1_ref, wgu_ref, wdn_ref, nm_ref, win_ref, qn_ref, kvn_ref,
                     wuq_ref, wuqr_ref, wukt_ref, wuv_ref, wg2_ref, bg_ref,
                     x1_ref, q_ref, kt_ref, v_ref, gq_ref, gk_ref, la_ref, gv_ref, gr_ref):
    x1 = _ffn(x_ref[...], n1_ref[...], wgu_ref[...], wdn_ref[...])
    x1_ref[...] = x1
    h = _rms(x1, nm_ref[...]).astype(_BF)
    proj = _dot(h, win_ref[...])
    c_q = proj[:, 0:256]
    c_kv = proj[:, 256:384]
    k_r = proj[:, 384:512]
    k_rr = proj[:, 512:640]
    g_q = proj[:, 640:896]
    g_k = proj[:, 896:1152]
    g_v = proj[:, 1152:1664]
    g_a = proj[:, 1664:1792]
    g_r = proj[:, 1792:2304]
    ang = pos_ref[...].astype(_F32) * freq_ref[0:1, :]
    cos = jnp.cos(ang) * freq_ref[1:2, :]
    sin = jnp.sin(ang)

    cq_n = _rms(c_q, qn_ref[...]).astype(_BF)
    q = _dot(cq_n, wuq_ref[...])
    q_rot = _dot(cq_n, wuqr_ref[...])
    scale = (MLA_NOPE + MLA_ROPE) ** -0.5 * LOG2_E
    for hd in range(MLA_HEADS):
        lo = hd * MLA_QK_PAD
        q_ref[:, lo:lo + LANES] = (q[:, lo:lo + LANES] * scale).astype(_BF)
        q_rope = q[:, lo + LANES:lo + 2 * LANES] * cos + q_rot[:, hd * LANES:(hd + 1) * LANES] * sin
        q_ref[:, lo + LANES:lo + 2 * LANES] = (q_rope * scale).astype(_BF)
    ckv_n = _rms(c_kv, kvn_ref[...]).astype(_BF)
    k_nope_t = _dot_nt(wukt_ref[...], ckv_n)
    k_rope_t = (k_r * cos + k_rr * sin).T.astype(_BF)
    for hd in range(MLA_HEADS):
        lo = hd * MLA_QK_PAD
        kt_ref[0, lo:lo + LANES, :] = k_nope_t[hd * LANES:(hd + 1) * LANES].astype(_BF)
        kt_ref[0, lo + LANES:lo + 2 * LANES, :] = k_rope_t
    v_ref[...] = _dot(ckv_n, wuv_ref[...]).astype(_BF)

    gq_ref[...] = g_q * (GLA_DK ** -0.5)
    gk_ref[...] = g_k
    gate = _dot(g_a.astype(_BF), wg2_ref[...]) + bg_ref[...]
    la_ref[...] = _log_sigmoid(gate) * (1.0 / GLA_GATE_TAU)
    gv_ref[...] = g_v.astype(_BF)
    gr_ref[...] = g_r.astype(_BF)


def _pre_odd_kernel(x_ref, n1_ref, wgu_ref, wdn_ref, nm_ref, wq_ref, wk_ref, wv_ref,
                    x1_ref, q_ref, kt_ref, v_ref):
    x1 = _ffn(x_ref[...], n1_ref[...], wgu_ref[...], wdn_ref[...])
    x1_ref[...] = x1
    h = _rms(x1, nm_ref[...]).astype(_BF)
    q_ref[...] = (_dot(h, wq_ref[...]) * (SB_HEAD_DIM ** -0.5)).astype(_BF)
    kt = _dot(h, wk_ref[...]).T
    for j in range(kt_ref.shape[0]):
        kt_ref[j] = kt[:, j * SB_TK:(j + 1) * SB_TK].astype(_BF)
    v_ref[...] = _dot(h, wv_ref[...]).astype(_BF)


def _row_spec(width, tm):
    return pl.BlockSpec((tm, width), lambda i: (i, 0))


def _pre_even(x, pos, freq, w):
    t = x.shape[0]
    tm = min(TOKEN_TILE, t)
    consts = [w["n1"], w["wgu"], w["wdn"], w["nm"], w["win"], w["qn"], w["kvn"], w["wuq"], w["wuqr"],
              w["wukt"], w["wuv"], w["wg2"], w["bg"]]
    assert tm == MLA_TK
    kt_rows = MLA_HEADS * MLA_QK_PAD
    out_widths = [(D_MODEL, _F32), (MLA_HEADS * MLA_QK_PAD, _BF), None,
                  (MLA_HEADS * MLA_V, _BF), (GLA_QK, _F32), (GLA_QK, _F32), (GLA_QK, _F32),
                  (GLA_V, _BF), (GLA_V, _BF)]
    out_specs = [pl.BlockSpec((1, kt_rows, tm), lambda i: (i, 0, 0)) if o is None else _row_spec(o[0], tm)
                 for o in out_widths]
    out_shape = [jax.ShapeDtypeStruct((t // tm, kt_rows, tm), _BF) if o is None
                 else jax.ShapeDtypeStruct((t, o[0]), o[1]) for o in out_widths]
    return pl.pallas_call(
        _pre_even_kernel,
        grid=(t // tm,),
        in_specs=[_row_spec(D_MODEL, tm), _row_spec(1, tm), _const_spec(freq.shape)]
        + [_const_spec(c.shape) for c in consts],
        out_specs=out_specs,
        out_shape=out_shape,
        compiler_params=_params(("parallel",)),
        name="pre_even",
    )(x, pos, freq, *consts)


def _pre_odd(x, w):
    t = x.shape[0]
    tm = min(TOKEN_TILE, t)
    consts = [w["n1"], w["wgu"], w["wdn"], w["nm"], w["wq"], w["wk"], w["wv"]]
    assert tm % SB_TK == 0
    out_widths = [(D_MODEL, _F32), (D_MODEL, _BF), None, (D_MODEL, _BF)]
    out_specs = [pl.BlockSpec((tm // SB_TK, D_MODEL, SB_TK), lambda i: (i, 0, 0)) if o is None
                 else _row_spec(o[0], tm) for o in out_widths]
    out_shape = [jax.ShapeDtypeStruct((t // SB_TK, D_MODEL, SB_TK), _BF) if o is None
                 else jax.ShapeDtypeStruct((t, o[0]), o[1]) for o in out_widths]
    return pl.pallas_call(
        _pre_odd_kernel,
        grid=(t // tm,),
        in_specs=[_row_spec(D_MODEL, tm)] + [_const_spec(c.shape) for c in consts],
        out_specs=out_specs,
        out_shape=out_shape,
        compiler_params=_params(("parallel",)),
        name="pre_odd",
    )(x, *consts)


def _post_kernel(n_mix, final, *refs):
    x_ref = refs[0]
    mix_refs = refs[1:1 + n_mix]
    p_ref = refs[1 + n_mix]
    wout_refs = refs[2 + n_mix:2 + 2 * n_mix]
    n2_ref, wgu_ref, wdn_ref, np_ref, wpg_ref, wpp_ref, nf_ref, out_ref = refs[2 + 2 * n_mix:]
    x = x_ref[...]
    for m_ref, w_ref in zip(mix_refs, wout_refs):
        x = x + _dot(m_ref[...], w_ref[...])
    x = _ffn(x, n2_ref[...], wgu_ref[...], wdn_ref[...])
    gate = _sigmoid(_dot(_rms(x, np_ref[...]).astype(_BF), wpg_ref[...]))
    x = x + gate * _dot(p_ref[...].astype(_BF), wpp_ref[...])
    if final:
        x = _rms(x, nf_ref[...])
    out_ref[...] = x


def _post(x, mixes, p, wouts, w, final_norm, final):
    t = x.shape[0]
    tm = min(TOKEN_TILE, t)
    consts = list(wouts) + [w["n2"], w["wgu2"], w["wdn2"], w["np"], w["wpg"], w["wpp"], final_norm]
    return pl.pallas_call(
        functools.partial(_post_kernel, len(mixes), final),
        grid=(t // tm,),
        in_specs=[_row_spec(D_MODEL, tm)] + [_row_spec(m.shape[1], tm) for m in mixes]
        + [_row_spec(P_DIM, tm)] + [_const_spec(c.shape) for c in consts],
        out_specs=_row_spec(D_MODEL, tm),
        out_shape=jax.ShapeDtypeStruct((t, D_MODEL), _F32),
        compiler_params=_params(("parallel",)),
        name="post_final" if final else "post",
    )(x, *mixes, p, *consts)


def _mla_kernel(q_ref, kt_ref, v_ref, o_ref, m_ref, l_ref, acc_ref):
    qi = pl.program_id(2)
    m_ref[...] = jnp.full(m_ref.shape, MASK_VALUE, _F32)
    l_ref[...] = jnp.zeros(l_ref.shape, _F32)
    acc_ref[...] = jnp.zeros(acc_ref.shape, _F32)
    reps = MLA_TK // LANES

    def step(c, masked):
        start = pl.multiple_of(c * MLA_TK, MLA_TK)
        for hd in range(MLA_HEADS_PER_STEP):
            q = q_ref[0, :, hd * MLA_QK_PAD:(hd + 1) * MLA_QK_PAD]
            kt = kt_ref[c, hd * MLA_QK_PAD:(hd + 1) * MLA_QK_PAD, :]
            v = v_ref[0, pl.ds(start, MLA_TK), hd * MLA_V:(hd + 1) * MLA_V]
            s = _dot(q, kt)
            if masked:
                row = lax.broadcasted_iota(jnp.int32, s.shape, 0) + qi * MLA_TQ
                col = lax.broadcasted_iota(jnp.int32, s.shape, 1) + start
                s = jnp.where(col <= row, s, MASK_VALUE)
            m_prev = m_ref[hd]
            m_next = jnp.maximum(m_prev, jnp.max(s, axis=1, keepdims=True))
            p = jnp.exp2(s - jnp.concatenate([m_next] * reps, axis=1))
            alpha = jnp.exp2(m_prev - m_next)
            l_ref[hd] = alpha * l_ref[hd] + jnp.sum(p, axis=1, keepdims=True)
            acc_ref[hd] = alpha * acc_ref[hd] + _dot(p.astype(_BF), v)
            m_ref[hd] = m_next

    def body(c, carry):
        step(c, False)
        return carry

    n_full = (qi * MLA_TQ) // MLA_TK
    lax.fori_loop(0, n_full, body, 0)
    step(n_full, True)
    for hd in range(MLA_HEADS_PER_STEP):
        o_ref[0, :, hd * MLA_V:(hd + 1) * MLA_V] = (acc_ref[hd] / l_ref[hd]).astype(o_ref.dtype)


def _mla(q, kt, v):
    b, s, _ = q.shape
    hb = MLA_HEADS_PER_STEP
    n_chunks = s // MLA_TK
    assert MLA_TK % MLA_TQ == 0 and s % MLA_TK == 0 and MLA_HEADS % hb == 0
    resident = dict(pipeline_mode=pl.Buffered(1))
    return pl.pallas_call(
        _mla_kernel,
        grid=(b, MLA_HEADS // hb, s // MLA_TQ),
        in_specs=[pl.BlockSpec((1, MLA_TQ, hb * MLA_QK_PAD), lambda bi, hi, qi: (bi, qi, hi)),
                  pl.BlockSpec((n_chunks, hb * MLA_QK_PAD, MLA_TK), lambda bi, hi, qi: (bi, hi, 0), **resident),
                  pl.BlockSpec((1, s, hb * MLA_V), lambda bi, hi, qi: (bi, 0, hi), **resident)],
        out_specs=pl.BlockSpec((1, MLA_TQ, hb * MLA_V), lambda bi, hi, qi: (bi, qi, hi)),
        out_shape=jax.ShapeDtypeStruct((b, s, MLA_HEADS * MLA_V), _BF),
        scratch_shapes=[pltpu.VMEM((hb, MLA_TQ, LANES), _F32)] * 3,
        compiler_params=_params(("parallel", "parallel", "arbitrary")),
        name="mla_attention",
    )(q, kt, v)


def _sb_kernel(q_ref, kt_ref, v_ref, tri_ref, o_ref, suf_ref, acc_ref, zb_ref, kmax_ref):
    qi = pl.program_id(2)
    n_key_chunks = kt_ref.shape[0]
    heads = range(SB_HEADS_PER_STEP)
    lanes_of = lambda hd: slice(hd * SB_HEAD_DIM, (hd + 1) * SB_HEAD_DIM)

    @pl.when(qi == 0)
    def _():
        def key_norm(i, mx):
            kc = kt_ref[i].astype(_F32)
            kc = kc * kc
            ssq = [jnp.sum(kc[lanes_of(hd)], axis=0, keepdims=True) for hd in heads]
            return tuple(jnp.maximum(m, s) for m, s in zip(mx, ssq))
        mx = lax.fori_loop(0, n_key_chunks, key_norm, tuple(jnp.zeros((1, SB_TK), _F32) for _ in heads))
        for hd in heads:
            kmax_ref[hd] = jnp.broadcast_to(jnp.max(mx[hd], axis=1, keepdims=True), kmax_ref.shape[1:])

    for hd in heads:
        qf = q_ref[0, :, lanes_of(hd)].astype(_F32)
        q_sq = jnp.sum(qf * qf, axis=1, keepdims=True)
        zb_ref[hd] = jnp.sqrt(q_sq * kmax_ref[hd, 0:1, :]) * SB_BOUND_SLACK
    suf_ref[...] = jnp.zeros(suf_ref.shape, _F32)
    acc_ref[...] = jnp.zeros(acc_ref.shape, _F32)
    tri = tri_ref[...]

    def chunk(c, masked):
        start = pl.multiple_of(c * SB_TK, SB_TK)
        slack = None
        for hd in heads:
            q = q_ref[0, :, lanes_of(hd)]
            kt = kt_ref[c, lanes_of(hd), :]
            v = v_ref[0, pl.ds(start, SB_TK), lanes_of(hd)]
            z = _dot(q, kt)
            t = jnp.maximum(z, 0.0) + jnp.log(1.0 + jnp.exp(-jnp.abs(z)))
            if masked:
                causal = (lax.broadcasted_iota(jnp.int32, z.shape, 1)
                          < lax.broadcasted_iota(jnp.int32, z.shape, 0))
                t = jnp.where(causal, t, 0.0)
            suf = suf_ref[hd]
            log_w = []
            for half in reversed(range(SB_TK // LANES)):
                th = t[:, half * LANES:(half + 1) * LANES]
                t_hi = th.astype(_BF)
                t_lo = (th - t_hi.astype(_F32)).astype(_BF)
                sums = _dot(jnp.concatenate([t_hi, t_lo], axis=1), tri)
                log_w.append(z[:, half * LANES:(half + 1) * LANES] - sums[:, :LANES] - suf)
                suf = suf + sums[:, LANES:]
            a = jnp.exp(jnp.concatenate(log_w[::-1], axis=1))
            if masked:
                a = jnp.where(causal, a, 0.0)
            acc_ref[hd] += _dot(a.astype(_BF), v)
            suf_ref[hd] = suf
            head_slack = zb_ref[hd] - suf
            slack = head_slack if slack is None else jnp.maximum(slack, head_slack)
        return (jnp.max(slack) >= SB_EXIT_LOG_WEIGHT).astype(jnp.int32)

    live = chunk(qi, True)

    def cond(carry):
        c, live = carry
        return jnp.logical_and(c >= 0, live > 0)

    def body(carry):
        c, _ = carry
        return c - 1, chunk(c, False)

    lax.while_loop(cond, body, (qi - 1, live))
    for hd in heads:
        o_ref[0, :, lanes_of(hd)] = acc_ref[hd].astype(o_ref.dtype)


def _sb_tri():
    r = np.arange(LANES)
    upper = (r[:, None] >= r[None, :]).astype(np.float32)
    half = np.concatenate([upper, np.ones((LANES, LANES), np.float32)], axis=1)
    return jnp.asarray(np.concatenate([half, half], axis=0), dtype=_BF)


def _sb(q, kt, v):
    b, s, _ = q.shape
    hb = SB_HEADS_PER_STEP
    assert SB_TQ == SB_TK and SB_TK % LANES == 0 and s % SB_TQ == 0 and SB_HEADS % hb == 0
    tri = _sb_tri()
    width = hb * SB_HEAD_DIM
    return pl.pallas_call(
        _sb_kernel,
        grid=(b, SB_HEADS // hb, s // SB_TQ),
        in_specs=[pl.BlockSpec((1, SB_TQ, width), lambda bi, hi, qi: (bi, qi, hi)),
                  pl.BlockSpec((s // SB_TK, width, SB_TK), lambda bi, hi, qi: (bi, hi, 0)),
                  pl.BlockSpec((1, s, width), lambda bi, hi, qi: (bi, 0, hi)),
                  _const_spec(tri.shape)],
        out_specs=pl.BlockSpec((1, SB_TQ, width), lambda bi, hi, qi: (bi, qi, hi)),
        out_shape=jax.ShapeDtypeStruct((b, s, D_MODEL), _BF),
        scratch_shapes=[pltpu.VMEM((hb, SB_TQ, LANES), _F32)] * 3 + [pltpu.VMEM((hb, 8, LANES), _F32)],
        compiler_params=_params(("parallel", "parallel", "arbitrary")),
        name="sb_attention",
    )(q, kt, v, tri)


def _gla_constants():
    c = GLA_CHUNK
    idx = np.arange(c)
    seg = []
    masks = []
    for lvl in range(GLA_LEVELS):
        s = 1 << lvl
        blk = idx // s
        start = blk * s
        end = start + s - 1
        t = idx[None, :]
        q_side = ((blk % 2 == 1)[:, None] & (t >= start[:, None]) & (t <= idx[:, None]))
        k_side = ((blk % 2 == 0)[:, None] & (t > idx[:, None]) & (t <= end[:, None]))
        seg.append(np.where((blk % 2 == 1)[:, None], q_side, k_side))
        pair = ((idx[:, None] // (2 * s)) == (idx[None, :] // (2 * s))) \
            & ((blk % 2 == 1)[:, None]) & ((blk % 2 == 0)[None, :])
        masks.append(np.tile(pair, (GLA_HEADS, 1)))
    t = idx[None, :]
    full = np.concatenate([t <= idx[:, None], t > idx[:, None]], axis=0)
    seg = np.stack(seg).astype(np.float32)
    masks = np.stack(masks).astype(np.float32)
    head_of_qk = np.arange(GLA_QK) // GLA_DK
    head_of_v = np.arange(GLA_V) // GLA_DV
    head_mask = (np.arange(8)[:, None] == head_of_qk[None, :]).astype(np.float32)
    expand = (head_of_qk[:, None] == head_of_v[None, :]).astype(np.float32)
    return (jnp.asarray(seg, _BF), jnp.asarray(full, _BF), jnp.asarray(masks, _F32), jnp.asarray(head_mask, _F32),
            jnp.asarray(expand, _BF), jnp.asarray(expand.T, _F32))


def _gla_kernel(gq_ref, gk_ref, la_ref, gv_ref, gr_ref, seg_ref, full_ref, mask_ref, hm_ref, exp_ref, bd_ref,
                gn_ref, o_ref, state_ref):
    c = GLA_CHUNK

    @pl.when(pl.program_id(1) == 0)
    def _():
        state_ref[...] = jnp.zeros(state_ref.shape, _F32)

    hm = hm_ref[...]
    gn = gn_ref[...]

    def intra_chunk(rows):
        q = gq_ref[0, rows, :]
        k = gk_ref[0, rows, :]
        la = la_ref[0, rows, :]
        v = gv_ref[0, rows, :]
        la_hi = la.astype(_BF)
        la_lo = (la - la_hi.astype(_F32)).astype(_BF)
        la2 = jnp.concatenate([la_hi, la_lo], axis=1)

        def seg_exp(seg):
            e = _dot(seg, la2)
            return jnp.exp(e[:, :GLA_QK] + e[:, GLA_QK:])

        att = None
        for lvl in range(GLA_LEVELS):
            w = seg_exp(seg_ref[lvl])
            ql = q * w
            kl = (k * w).astype(_BF)
            qs = jnp.concatenate([ql * hm[hd:hd + 1] for hd in range(GLA_HEADS)], axis=0).astype(_BF)
            a = _dot_nt(qs, kl) * mask_ref[lvl]
            att = a if att is None else att + a
        w = seg_exp(full_ref[...])
        o = _dot((q * k).astype(_BF), exp_ref[...]) * v.astype(_F32)
        o_intra = [_dot(att[hd * c:(hd + 1) * c, :].astype(_BF), v[:, hd * GLA_DV:(hd + 1) * GLA_DV])
                   for hd in range(GLA_HEADS)]
        o = o + jnp.concatenate(o_intra, axis=1)
        return o, (q * w[:c]).astype(_BF), (k * w[c:]).astype(_BF), w[c - 1:c], v

    chunks = [intra_chunk(slice(i * c, (i + 1) * c)) for i in range(GLA_CHUNKS_PER_STEP)]
    state = state_ref[...]
    for i, (o, q_state, k_state, decay, v) in enumerate(chunks):
        o = o + _dot_nt(q_state, state.astype(_BF))
        state = state * decay + _dot_tn(v, k_state) * bd_ref[...]
        gr = gr_ref[0, i * c:(i + 1) * c, :].astype(_F32)
        for hd in range(GLA_HEADS):
            sl = slice(hd * GLA_DV, (hd + 1) * GLA_DV)
            o_ref[0, i * c:(i + 1) * c, sl] = (_rms(o[:, sl], gn[:, sl]) * _silu(gr[:, sl])).astype(o_ref.dtype)
    state_ref[...] = state


def _gla(gq, gk, la, gv, gr, gla_norm):
    b, s, _ = gq.shape
    rows = GLA_CHUNK * GLA_CHUNKS_PER_STEP
    assert s % rows == 0 and (1 << GLA_LEVELS) == GLA_CHUNK
    consts = list(_gla_constants()) + [gla_norm]
    tok = lambda wd: pl.BlockSpec((1, rows, wd), lambda bi, ci: (bi, ci, 0))
    return pl.pallas_call(
        _gla_kernel,
        grid=(b, s // rows),
        in_specs=[tok(GLA_QK), tok(GLA_QK), tok(GLA_QK), tok(GLA_V), tok(GLA_V)]
        + [_const_spec(cn.shape) for cn in consts],
        out_specs=tok(GLA_V),
        out_shape=jax.ShapeDtypeStruct((b, s, GLA_V), _BF),
        scratch_shapes=[pltpu.VMEM((GLA_V, GLA_QK), _F32)],
        compiler_params=_params(("parallel", "arbitrary")),
        name="gla_chunked",
    )(gq, gk, la, gv, gr, *consts)


def _rotate_half_cols(w):
    half = w.shape[1] // 2
    return jnp.concatenate([-w[:, half:], w[:, :half]], axis=1)


def _pad_cols(w, width):
    return jnp.pad(w, ((0, 0), (0, width - w.shape[1])))


def _prep_even_weights(w_in, w_uq, w_ukv, w_gate2):
    splits = np.cumsum([MLA_Q_RANK, MLA_KV_RANK, MLA_ROPE, GLA_QK, GLA_QK, GLA_V, GLA_GATE_RANK])
    c_q, c_kv, k_r, g_q, g_k, g_v, g_a, g_r = jnp.split(w_in, splits, axis=1)
    win = jnp.concatenate([c_q, c_kv, _pad_cols(k_r, LANES), _pad_cols(_rotate_half_cols(k_r), LANES),
                           g_q, g_k, g_v, _pad_cols(g_a, LANES), g_r], axis=1).astype(_BF)
    uq = w_uq.reshape(MLA_Q_RANK, MLA_HEADS, MLA_NOPE + MLA_ROPE)
    uq_rope = uq[:, :, MLA_NOPE:]
    rot = jnp.concatenate([-uq_rope[:, :, MLA_ROPE // 2:], uq_rope[:, :, :MLA_ROPE // 2]], axis=2)
    wuq = jnp.pad(uq, ((0, 0), (0, 0), (0, MLA_QK_PAD - uq.shape[2]))).reshape(MLA_Q_RANK, -1).astype(_BF)
    wuqr = jnp.pad(rot, ((0, 0), (0, 0), (0, LANES - MLA_ROPE))).reshape(MLA_Q_RANK, -1).astype(_BF)
    ukv = w_ukv.reshape(MLA_KV_RANK, MLA_HEADS, MLA_NOPE + MLA_V)
    wukt = ukv[:, :, :MLA_NOPE].reshape(MLA_KV_RANK, -1).T.astype(_BF)
    wuv = ukv[:, :, MLA_NOPE:].reshape(MLA_KV_RANK, -1).astype(_BF)
    wg2 = jnp.pad(w_gate2, ((0, LANES - GLA_GATE_RANK), (0, 0))).astype(_BF)
    return dict(win=win, wuq=wuq, wuqr=wuqr, wukt=wukt, wuv=wuv, wg2=wg2)


def _rope_frequencies():
    half = MLA_ROPE // 2
    inv_freq = 1.0 / (ROPE_THETA ** (jnp.arange(half, dtype=_F32) / half))
    pad = jnp.zeros((LANES - MLA_ROPE,), _F32)
    rows = jnp.stack([jnp.concatenate([inv_freq, inv_freq, pad]),
                      jnp.concatenate([jnp.ones((MLA_ROPE,), _F32), pad])])
    return jnp.pad(rows, ((0, 8 - rows.shape[0]), (0, 0)))


def kernel(x, p, positions, ffn1_norm, ffn1_w_gu, ffn1_w_down, mix_norm, ffn2_norm, ffn2_w_gu, ffn2_w_down, ple_norm, ple_w_gate, ple_w_proj, ev_w_in, ev_q_norm, ev_kv_norm, ev_w_uq, ev_w_ukv, ev_w_gate2, ev_b_gate, ev_gla_norm, ev_w_out, od_w_qkv, od_w_out, final_norm):
    b, s, d = x.shape
    depth = p.shape[0]
    t = b * s
    pos = positions.reshape(t, 1)
    freq = _rope_frequencies()
    row = lambda a: a.reshape(1, -1).astype(_F32)
    xt = x.reshape(t, d)
    fin = row(final_norm)
    for i in range(depth):
        j = i // 2
        common = dict(n1=row(ffn1_norm[i]), wgu=ffn1_w_gu[i].astype(_BF), wdn=ffn1_w_down[i].astype(_BF),
                      nm=row(mix_norm[i]), n2=row(ffn2_norm[i]), wgu2=ffn2_w_gu[i].astype(_BF),
                      wdn2=ffn2_w_down[i].astype(_BF), np=row(ple_norm[i]), wpg=ple_w_gate[i].astype(_BF),
                      wpp=ple_w_proj[i].astype(_BF))
        if i % 2 == 0:
            w = dict(common, **_prep_even_weights(ev_w_in[j], ev_w_uq[j], ev_w_ukv[j], ev_w_gate2[j]),
                     qn=row(ev_q_norm[j]), kvn=row(ev_kv_norm[j]), bg=row(ev_b_gate[j]))
            x1, q, kt, v, gq, gk, la, gv, gr = _pre_even(xt, pos, freq, w)
            seq = lambda a: a.reshape(b, s, a.shape[-1])
            o_mla = _mla(seq(q), kt, seq(v))
            o_gla = _gla(seq(gq), seq(gk), seq(la), seq(gv), seq(gr), row(ev_gla_norm[j]))
            mixes = [o_mla.reshape(t, -1), o_gla.reshape(t, -1)]
            w_out = ev_w_out[j].astype(_BF)
            wouts = [w_out[:MLA_HEADS * MLA_V], w_out[MLA_HEADS * MLA_V:]]
        else:
            wqkv = od_w_qkv[j].astype(_BF)
            w = dict(common, wq=wqkv[:, :D_MODEL], wk=wqkv[:, D_MODEL:2 * D_MODEL], wv=wqkv[:, 2 * D_MODEL:])
            x1, q, kt, v = _pre_odd(xt, w)
            seq = lambda a: a.reshape(b, s, a.shape[-1])
            o_sb = _sb(seq(q), kt, seq(v))
            mixes = [o_sb.reshape(t, -1)]
            wouts = [od_w_out[j].astype(_BF)]
        xt = _post(x1, mixes, p[i].reshape(t, -1), wouts, w, fin, i == depth - 1)
    return xt.reshape(b, s, d)
```

```python
import functools

import numpy as np
import jax
import jax.numpy as jnp
from jax import lax
from jax.experimental import pallas as pl
from jax.experimental.pallas import tpu as pltpu

D_MODEL = 1024
P_DIM = 256
EPS = 1e-6
D_FF = 1408
MLA_HEADS = 4
MLA_Q_RANK = 256
MLA_KV_RANK = 128
MLA_NOPE = 128
MLA_ROPE = 64
MLA_V = 128
ROPE_THETA = 10000.0
GLA_HEADS = 4
GLA_DK = 64
GLA_DV = 128
GLA_GATE_RANK = 16
GLA_GATE_TAU = 16.0
SB_HEADS = 8
SB_HEAD_DIM = D_MODEL // SB_HEADS

LANES = 128
MLA_QK_PAD = 2 * LANES
GLA_QK = GLA_HEADS * GLA_DK
GLA_V = GLA_HEADS * GLA_DV

TOKEN_TILE = 512
MLA_TQ = 512
MLA_TK = 512
MLA_HEADS_PER_STEP = 4
SB_TQ = 256
SB_TK = 256
SB_HEADS_PER_STEP = 4
SB_EXIT_LOG_WEIGHT = -106.0
SB_BOUND_SLACK = 1.001
GLA_CHUNK = 128
GLA_CHUNKS_PER_STEP = 2
GLA_LEVELS = 7
VMEM_LIMIT = 56 * 1024 * 1024
MASK_VALUE = -1e30
MLA_SAFE_SCORE_BOUND = 60.0
MLA_BOUND_SLACK = 1.001
MLA_QK_REAL = 192
LOG2_E = 1.4426950408889634

_BF = jnp.bfloat16
_F32 = jnp.float32


def _dot(a, b):
    return jnp.dot(a, b, preferred_element_type=_F32)


def _dot_nt(a, b):
    return lax.dot_general(a, b, (((1,), (1,)), ((), ())), preferred_element_type=_F32)


def _dot_tn(a, b):
    return lax.dot_general(a, b, (((0,), (0,)), ((), ())), preferred_element_type=_F32)


def _rms(x, w):
    return x * lax.rsqrt(jnp.mean(x * x, axis=-1, keepdims=True) + EPS) * w


def _sigmoid(x):
    return 1.0 / (1.0 + jnp.exp(-x))


def _silu(x):
    return x * _sigmoid(x)


def _log_sigmoid(x):
    return jnp.minimum(x, 0.0) - jnp.log(1.0 + jnp.exp(-jnp.abs(x)))


def _ffn(x, norm_w, w_gu, w_down):
    h = _rms(x, norm_w).astype(_BF)
    gu = _dot(h, w_gu)
    act = (_silu(gu[:, :D_FF]) * gu[:, D_FF:]).astype(_BF)
    return x + 0.5 * _dot(act, w_down)


def _const_spec(shape):
    nd = len(shape)
    return pl.BlockSpec(shape, lambda *_: (0,) * nd, pipeline_mode=pl.Buffered(1))


def _params(semantics):
    return pltpu.CompilerParams(dimension_semantics=semantics, vmem_limit_bytes=VMEM_LIMIT)


def _pre_even_kernel(x_ref, cos_ref, sin_ref, n1_ref, wgu_ref, wdn_ref, nm_ref, win_ref, qn_ref, kvn_ref,
                     wuq_ref, wuqr_ref, wukt_ref, wuv_ref, wg2_ref, bg_ref,
                     x1_ref, q_ref, kt_ref, v_ref, gq_ref, gk_ref, la_ref, gv_ref, gr_ref):
    x1 = _ffn(x_ref[...], n1_ref[...], wgu_ref[...], wdn_ref[...])
    x1_ref[...] = x1
    h = _rms(x1, nm_ref[...]).astype(_BF)
    proj = _dot(h, win_ref[...])
    c_q = proj[:, 0:256]
    c_kv = proj[:, 256:384]
    k_r = proj[:, 384:512]
    k_rr = proj[:, 512:640]
    g_q = proj[:, 640:896]
    g_k = proj[:, 896:1152]
    g_v = proj[:, 1152:1664]
    g_a = proj[:, 1664:1792]
    g_r = proj[:, 1792:2304]
    cos = cos_ref[...]
    sin = sin_ref[...]

    cq_n = _rms(c_q, qn_ref[...]).astype(_BF)
    q = _dot(cq_n, wuq_ref[...])
    q_rot = _dot(cq_n, wuqr_ref[...])
    scale = (MLA_NOPE + MLA_ROPE) ** -0.5 * LOG2_E
    for hd in range(MLA_HEADS):
        lo = hd * MLA_QK_PAD
        q_ref[:, lo:lo + LANES] = (q[:, lo:lo + LANES] * scale).astype(_BF)
        q_rope = q[:, lo + LANES:lo + 2 * LANES] * cos + q_rot[:, hd * LANES:(hd + 1) * LANES] * sin
        q_ref[:, lo + LANES:lo + 2 * LANES] = (q_rope * scale).astype(_BF)
    ckv_n = _rms(c_kv, kvn_ref[...]).astype(_BF)
    k_nope_t = _dot_nt(wukt_ref[...], ckv_n)
    k_rope_t = (k_r * cos + k_rr * sin).T
    pad_row = lax.broadcasted_iota(jnp.int32, k_rope_t.shape, 0) == MLA_ROPE
    k_rope_t = jnp.where(pad_row, 1.0, k_rope_t).astype(_BF)
    for hd in range(MLA_HEADS):
        lo = hd * MLA_QK_PAD
        kt_ref[0, lo:lo + LANES, :] = k_nope_t[hd * LANES:(hd + 1) * LANES].astype(_BF)
        kt_ref[0, lo + LANES:lo + 2 * LANES, :] = k_rope_t
    v_ref[...] = _dot(ckv_n, wuv_ref[...]).astype(_BF)

    gq_ref[...] = g_q * (GLA_DK ** -0.5)
    gk_ref[...] = g_k
    gate = _dot(g_a.astype(_BF), wg2_ref[...]) + bg_ref[...]
    la_ref[...] = _log_sigmoid(gate) * (1.0 / GLA_GATE_TAU)
    gv_ref[...] = g_v.astype(_BF)
    gr_ref[...] = g_r.astype(_BF)


def _pre_odd_kernel(x_ref, n1_ref, wgu_ref, wdn_ref, nm_ref, wq_ref, wk_ref, wv_ref,
                    x1_ref, q_ref, kt_ref, v_ref):
    x1 = _ffn(x_ref[...], n1_ref[...], wgu_ref[...], wdn_ref[...])
    x1_ref[...] = x1
    h = _rms(x1, nm_ref[...]).astype(_BF)
    q_ref[...] = (_dot(h, wq_ref[...]) * (SB_HEAD_DIM ** -0.5)).astype(_BF)
    kt = _dot(h, wk_ref[...]).T
    for j in range(kt_ref.shape[0]):
        kt_ref[j] = kt[:, j * SB_TK:(j + 1) * SB_TK].astype(_BF)
    v_ref[...] = _dot(h, wv_ref[...]).astype(_BF)


def _row_spec(width, tm):
    return pl.BlockSpec((tm, width), lambda i: (i, 0))


def _pre_even(x, cos_t, sin_t, w):
    t = x.shape[0]
    tm = min(TOKEN_TILE, t)
    consts = [w["n1"], w["wgu"], w["wdn"], w["nm"], w["win"], w["qn"], w["kvn"], w["wuq"], w["wuqr"],
              w["wukt"], w["wuv"], w["wg2"], w["bg"]]
    assert tm == MLA_TK
    kt_rows = MLA_HEADS * MLA_QK_PAD
    out_widths = [(D_MODEL, _F32), (MLA_HEADS * MLA_QK_PAD, _BF), None,
                  (MLA_HEADS * MLA_V, _BF), (GLA_QK, _F32), (GLA_QK, _F32), (GLA_QK, _F32),
                  (GLA_V, _BF), (GLA_V, _BF)]
    out_specs = [pl.BlockSpec((1, kt_rows, tm), lambda i: (i, 0, 0)) if o is None else _row_spec(o[0], tm)
                 for o in out_widths]
    out_shape = [jax.ShapeDtypeStruct((t // tm, kt_rows, tm), _BF) if o is None
                 else jax.ShapeDtypeStruct((t, o[0]), o[1]) for o in out_widths]
    return pl.pallas_call(
        _pre_even_kernel,
        grid=(t // tm,),
        in_specs=[_row_spec(D_MODEL, tm), _row_spec(LANES, tm), _row_spec(LANES, tm)]
        + [_const_spec(c.shape) for c in consts],
        out_specs=out_specs,
        out_shape=out_shape,
        compiler_params=_params(("parallel",)),
        name="pre_even",
    )(x, cos_t, sin_t, *consts)


def _pre_odd(x, w):
    t = x.shape[0]
    tm = min(TOKEN_TILE, t)
    consts = [w["n1"], w["wgu"], w["wdn"], w["nm"], w["wq"], w["wk"], w["wv"]]
    assert tm % SB_TK == 0
    out_widths = [(D_MODEL, _F32), (D_MODEL, _BF), None, (D_MODEL, _BF)]
    out_specs = [pl.BlockSpec((tm // SB_TK, D_MODEL, SB_TK), lambda i: (i, 0, 0)) if o is None
                 else _row_spec(o[0], tm) for o in out_widths]
    out_shape = [jax.ShapeDtypeStruct((t // SB_TK, D_MODEL, SB_TK), _BF) if o is None
                 else jax.ShapeDtypeStruct((t, o[0]), o[1]) for o in out_widths]
    return pl.pallas_call(
        _pre_odd_kernel,
        grid=(t // tm,),
        in_specs=[_row_spec(D_MODEL, tm)] + [_const_spec(c.shape) for c in consts],
        out_specs=out_specs,
        out_shape=out_shape,
        compiler_params=_params(("parallel",)),
        name="pre_odd",
    )(x, *consts)


def _post_kernel(n_mix, final, *refs):
    x_ref = refs[0]
    mix_refs = refs[1:1 + n_mix]
    p_ref = refs[1 + n_mix]
    wout_refs = refs[2 + n_mix:2 + 2 * n_mix]
    n2_ref, wgu_ref, wdn_ref, np_ref, wpg_ref, wpp_ref, nf_ref, out_ref = refs[2 + 2 * n_mix:]
    x = x_ref[...]
    for m_ref, w_ref in zip(mix_refs, wout_refs):
        x = x + _dot(m_ref[...], w_ref[...])
    x = _ffn(x, n2_ref[...], wgu_ref[...], wdn_ref[...])
    gate = _sigmoid(_dot(_rms(x, np_ref[...]).astype(_BF), wpg_ref[...]))
    x = x + gate * _dot(p_ref[...].astype(_BF), wpp_ref[...])
    if final:
        x = _rms(x, nf_ref[...])
    out_ref[...] = x


def _post(x, mixes, p, layer, wouts, w, final_norm, final):
    t = x.shape[0]
    tm = min(TOKEN_TILE, t)
    consts = list(wouts) + [w["n2"], w["wgu2"], w["wdn2"], w["np"], w["wpg"], w["wpp"], final_norm]
    return pl.pallas_call(
        functools.partial(_post_kernel, len(mixes), final),
        grid=(t // tm,),
        in_specs=[_row_spec(D_MODEL, tm)] + [_row_spec(m.shape[1], tm) for m in mixes]
        + [pl.BlockSpec((None, tm, P_DIM), lambda i: (layer, i, 0))] + [_const_spec(c.shape) for c in consts],
        out_specs=_row_spec(D_MODEL, tm),
        out_shape=jax.ShapeDtypeStruct((t, D_MODEL), _F32),
        compiler_params=_params(("parallel",)),
        name="post_final" if final else "post",
    )(x, *mixes, p, *consts)


def _mla_kernel(q_ref, kt_ref, v_ref, o_ref, m_ref, l_ref, acc_ref, qs_ref, kmax_ref):
    qi = pl.program_id(2)
    heads = range(MLA_HEADS_PER_STEP)
    slot = lambda hd: slice(hd * MLA_QK_PAD, (hd + 1) * MLA_QK_PAD)
    vslot = lambda hd: slice(hd * MLA_V, (hd + 1) * MLA_V)
    reps = MLA_TK // LANES

    @pl.when(qi == 0)
    def _():
        def key_norm(i, mx):
            kc = kt_ref[i].astype(_F32)
            kc = kc * kc
            ssq = [jnp.sum(kc[hd * MLA_QK_PAD:hd * MLA_QK_PAD + MLA_QK_REAL], axis=0, keepdims=True)
                   for hd in heads]
            return tuple(jnp.maximum(m, s) for m, s in zip(mx, ssq))
        mx = lax.fori_loop(0, kt_ref.shape[0], key_norm, tuple(jnp.zeros((1, MLA_TK), _F32) for _ in heads))
        for hd in heads:
            kmax_ref[hd] = jnp.broadcast_to(jnp.max(mx[hd], axis=1, keepdims=True), kmax_ref.shape[1:])

    bounds = []
    for hd in heads:
        qf = q_ref[0, :, slot(hd)].astype(_F32)
        q_sq = jnp.sum(qf * qf, axis=1, keepdims=True)
        bounds.append(jnp.sqrt(q_sq * kmax_ref[hd, 0:1, :]) * MLA_BOUND_SLACK)
    worst = bounds[0]
    for bnd in bounds[1:]:
        worst = jnp.maximum(worst, bnd)
    fast = jnp.max(worst) <= MLA_SAFE_SCORE_BOUND

    l_ref[...] = jnp.zeros(l_ref.shape, _F32)
    acc_ref[...] = jnp.zeros(acc_ref.shape, _F32)
    n_full = (qi * MLA_TQ) // MLA_TK

    def causal_mask(s, start):
        row = lax.broadcasted_iota(jnp.int32, s.shape, 0) + qi * MLA_TQ
        col = lax.broadcasted_iota(jnp.int32, s.shape, 1) + start
        return jnp.where(col <= row, s, MASK_VALUE)

    def run(step):
        def body(c, carry):
            step(c, False)
            return carry
        lax.fori_loop(0, n_full, body, 0)
        step(n_full, True)

    @pl.when(fast)
    def _():
        for hd in heads:
            lane = lax.broadcasted_iota(jnp.int32, (MLA_TQ, MLA_QK_PAD), 1)
            shift = jnp.concatenate([-bounds[hd]] * (MLA_QK_PAD // LANES), axis=1).astype(_BF)
            qs_ref[hd] = jnp.where(lane == MLA_QK_REAL, shift, q_ref[0, :, slot(hd)])

        def step(c, masked):
            start = pl.multiple_of(c * MLA_TK, MLA_TK)
            for hd in heads:
                s = _dot(qs_ref[hd], kt_ref[c, slot(hd), :])
                if masked:
                    s = causal_mask(s, start)
                p = jnp.exp2(s)
                part = p[:, :LANES]
                for r in range(1, reps):
                    part = part + p[:, r * LANES:(r + 1) * LANES]
                l_ref[hd] += part
                acc_ref[hd] += _dot(p.astype(_BF), v_ref[0, pl.ds(start, MLA_TK), vslot(hd)])

        run(step)
        for hd in heads:
            denom = jnp.sum(l_ref[hd], axis=1, keepdims=True)
            o_ref[0, :, vslot(hd)] = (acc_ref[hd] / denom).astype(o_ref.dtype)

    @pl.when(jnp.logical_not(fast))
    def _():
        m_ref[...] = jnp.full(m_ref.shape, MASK_VALUE, _F32)

        def step(c, masked):
            start = pl.multiple_of(c * MLA_TK, MLA_TK)
            for hd in heads:
                s = _dot(q_ref[0, :, slot(hd)], kt_ref[c, slot(hd), :])
                if masked:
                    s = causal_mask(s, start)
                m_prev = m_ref[hd]
                m_next = jnp.maximum(m_prev, jnp.max(s, axis=1, keepdims=True))
                p = jnp.exp2(s - jnp.concatenate([m_next] * reps, axis=1))
                alpha = jnp.exp2(m_prev - m_next)
                l_ref[hd] = alpha * l_ref[hd] + jnp.sum(p, axis=1, keepdims=True)
                acc_ref[hd] = alpha * acc_ref[hd] + _dot(p.astype(_BF), v_ref[0, pl.ds(start, MLA_TK), vslot(hd)])
                m_ref[hd] = m_next

        run(step)
        for hd in heads:
            o_ref[0, :, vslot(hd)] = (acc_ref[hd] / l_ref[hd]).astype(o_ref.dtype)


def _mla(q, kt, v):
    b, s, _ = q.shape
    hb = MLA_HEADS_PER_STEP
    n_chunks = s // MLA_TK
    assert MLA_TK % MLA_TQ == 0 and s % MLA_TK == 0 and MLA_HEADS % hb == 0
    resident = dict(pipeline_mode=pl.Buffered(1))
    return pl.pallas_call(
        _mla_kernel,
        grid=(b, MLA_HEADS // hb, s // MLA_TQ),
        in_specs=[pl.BlockSpec((1, MLA_TQ, hb * MLA_QK_PAD), lambda bi, hi, qi: (bi, qi, hi)),
                  pl.BlockSpec((n_chunks, hb * MLA_QK_PAD, MLA_TK), lambda bi, hi, qi: (bi, hi, 0), **resident),
                  pl.BlockSpec((1, s, hb * MLA_V), lambda bi, hi, qi: (bi, 0, hi), **resident)],
        out_specs=pl.BlockSpec((1, MLA_TQ, hb * MLA_V), lambda bi, hi, qi: (bi, qi, hi)),
        out_shape=jax.ShapeDtypeStruct((b, s, MLA_HEADS * MLA_V), _BF),
        scratch_shapes=[pltpu.VMEM((hb, MLA_TQ, LANES), _F32)] * 3
        + [pltpu.VMEM((hb, MLA_TQ, MLA_QK_PAD), _BF), pltpu.VMEM((hb, 8, LANES), _F32)],
        compiler_params=_params(("parallel", "parallel", "arbitrary")),
        name="mla_attention",
    )(q, kt, v)


def _sb_kernel(q_ref, kt_ref, v_ref, tri_ref, o_ref, suf_ref, acc_ref, zb_ref, kmax_ref):
    qi = pl.program_id(2)
    n_key_chunks = kt_ref.shape[0]
    heads = range(SB_HEADS_PER_STEP)
    lanes_of = lambda hd: slice(hd * SB_HEAD_DIM, (hd + 1) * SB_HEAD_DIM)

    @pl.when(qi == 0)
    def _():
        def key_norm(i, mx):
            kc = kt_ref[i].astype(_F32)
            kc = kc * kc
            ssq = [jnp.sum(kc[lanes_of(hd)], axis=0, keepdims=True) for hd in heads]
            return tuple(jnp.maximum(m, s) for m, s in zip(mx, ssq))
        mx = lax.fori_loop(0, n_key_chunks, key_norm, tuple(jnp.zeros((1, SB_TK), _F32) for _ in heads))
        for hd in heads:
            kmax_ref[hd] = jnp.broadcast_to(jnp.max(mx[hd], axis=1, keepdims=True), kmax_ref.shape[1:])

    for hd in heads:
        qf = q_ref[0, :, lanes_of(hd)].astype(_F32)
        q_sq = jnp.sum(qf * qf, axis=1, keepdims=True)
        zb_ref[hd] = jnp.sqrt(q_sq * kmax_ref[hd, 0:1, :]) * SB_BOUND_SLACK
    suf_ref[...] = jnp.zeros(suf_ref.shape, _F32)
    acc_ref[...] = jnp.zeros(acc_ref.shape, _F32)
    tri = tri_ref[...]

    def chunk(c, masked):
        start = pl.multiple_of(c * SB_TK, SB_TK)
        slack = None
        for hd in heads:
            q = q_ref[0, :, lanes_of(hd)]
            kt = kt_ref[c, lanes_of(hd), :]
            v = v_ref[0, pl.ds(start, SB_TK), lanes_of(hd)]
            z = _dot(q, kt)
            t = jnp.maximum(z, 0.0) + jnp.log(1.0 + jnp.exp(-jnp.abs(z)))
            if masked:
                causal = (lax.broadcasted_iota(jnp.int32, z.shape, 1)
                          < lax.broadcasted_iota(jnp.int32, z.shape, 0))
                t = jnp.where(causal, t, 0.0)
            suf = suf_ref[hd]
            log_w = []
            for half in reversed(range(SB_TK // LANES)):
                th = t[:, half * LANES:(half + 1) * LANES]
                t_hi = th.astype(_BF)
                t_lo = (th - t_hi.astype(_F32)).astype(_BF)
                sums = _dot(jnp.concatenate([t_hi, t_lo], axis=1), tri)
                log_w.append(z[:, half * LANES:(half + 1) * LANES] - sums[:, :LANES] - suf)
                suf = suf + sums[:, LANES:]
            a = jnp.exp(jnp.concatenate(log_w[::-1], axis=1))
            if masked:
                a = jnp.where(causal, a, 0.0)
            acc_ref[hd] += _dot(a.astype(_BF), v)
            suf_ref[hd] = suf
            head_slack = zb_ref[hd] - suf
            slack = head_slack if slack is None else jnp.maximum(slack, head_slack)
        return (jnp.max(slack) >= SB_EXIT_LOG_WEIGHT).astype(jnp.int32)

    live = chunk(qi, True)

    def cond(carry):
        c, live = carry
        return jnp.logical_and(c >= 0, live > 0)

    def body(carry):
        c, _ = carry
        return c - 1, chunk(c, False)

    lax.while_loop(cond, body, (qi - 1, live))
    for hd in heads:
        o_ref[0, :, lanes_of(hd)] = acc_ref[hd].astype(o_ref.dtype)


def _sb_tri():
    r = np.arange(LANES)
    upper = (r[:, None] >= r[None, :]).astype(np.float32)
    half = np.concatenate([upper, np.ones((LANES, LANES), np.float32)], axis=1)
    return jnp.asarray(np.concatenate([half, half], axis=0), dtype=_BF)


def _sb(q, kt, v):
    b, s, _ = q.shape
    hb = SB_HEADS_PER_STEP
    assert SB_TQ == SB_TK and SB_TK % LANES == 0 and s % SB_TQ == 0 and SB_HEADS % hb == 0
    tri = _sb_tri()
    width = hb * SB_HEAD_DIM
    return pl.pallas_call(
        _sb_kernel,
        grid=(b, SB_HEADS // hb, s // SB_TQ),
        in_specs=[pl.BlockSpec((1, SB_TQ, width), lambda bi, hi, qi: (bi, qi, hi)),
                  pl.BlockSpec((s // SB_TK, width, SB_TK), lambda bi, hi, qi: (bi, hi, 0)),
                  pl.BlockSpec((1, s, width), lambda bi, hi, qi: (bi, 0, hi)),
                  _const_spec(tri.shape)],
        out_specs=pl.BlockSpec((1, SB_TQ, width), lambda bi, hi, qi: (bi, qi, hi)),
        out_shape=jax.ShapeDtypeStruct((b, s, D_MODEL), _BF),
        scratch_shapes=[pltpu.VMEM((hb, SB_TQ, LANES), _F32)] * 3 + [pltpu.VMEM((hb, 8, LANES), _F32)],
        compiler_params=_params(("parallel", "parallel", "arbitrary")),
        name="sb_attention",
    )(q, kt, v, tri)


def _gla_constants():
    c = GLA_CHUNK
    idx = np.arange(c)
    seg = []
    masks = []
    for lvl in range(GLA_LEVELS):
        s = 1 << lvl
        blk = idx // s
        start = blk * s
        end = start + s - 1
        t = idx[None, :]
        q_side = ((blk % 2 == 1)[:, None] & (t >= start[:, None]) & (t <= idx[:, None]))
        k_side = ((blk % 2 == 0)[:, None] & (t > idx[:, None]) & (t <= end[:, None]))
        seg.append(np.where((blk % 2 == 1)[:, None], q_side, k_side))
        pair = ((idx[:, None] // (2 * s)) == (idx[None, :] // (2 * s))) \
            & ((blk % 2 == 1)[:, None]) & ((blk % 2 == 0)[None, :])
        masks.append(np.tile(pair, (GLA_HEADS, 1)))
    t = idx[None, :]
    full = np.concatenate([t <= idx[:, None], t > idx[:, None]], axis=0)
    seg = np.stack(seg).astype(np.float32)
    masks = np.stack(masks).astype(np.float32)
    head_of_qk = np.arange(GLA_QK) // GLA_DK
    head_of_v = np.arange(GLA_V) // GLA_DV
    head_mask = (np.arange(8)[:, None] == head_of_qk[None, :]).astype(np.float32)
    expand = (head_of_qk[:, None] == head_of_v[None, :]).astype(np.float32)
    return (jnp.asarray(seg, _BF), jnp.asarray(full, _BF), jnp.asarray(masks, _F32), jnp.asarray(head_mask, _F32),
            jnp.asarray(expand, _BF), jnp.asarray(expand.T, _F32))


def _gla_kernel(gq_ref, gk_ref, la_ref, gv_ref, gr_ref, seg_ref, full_ref, mask_ref, hm_ref, exp_ref, bd_ref,
                gn_ref, o_ref, state_ref):
    c = GLA_CHUNK

    @pl.when(pl.program_id(1) == 0)
    def _():
        state_ref[...] = jnp.zeros(state_ref.shape, _F32)

    hm = hm_ref[...]
    gn = gn_ref[...]

    def intra_chunk(rows):
        q = gq_ref[0, rows, :]
        k = gk_ref[0, rows, :]
        la = la_ref[0, rows, :]
        v = gv_ref[0, rows, :]
        la_hi = la.astype(_BF)
        la_lo = (la - la_hi.astype(_F32)).astype(_BF)
        la2 = jnp.concatenate([la_hi, la_lo], axis=1)

        def seg_exp(seg):
            e = _dot(seg, la2)
            return jnp.exp(e[:, :GLA_QK] + e[:, GLA_QK:])

        att = None
        for lvl in range(GLA_LEVELS):
            w = seg_exp(seg_ref[lvl])
            ql = q * w
            kl = (k * w).astype(_BF)
            qs = jnp.concatenate([ql * hm[hd:hd + 1] for hd in range(GLA_HEADS)], axis=0).astype(_BF)
            a = _dot_nt(qs, kl) * mask_ref[lvl]
            att = a if att is None else att + a
        w = seg_exp(full_ref[...])
        o = _dot((q * k).astype(_BF), exp_ref[...]) * v.astype(_F32)
        o_intra = [_dot(att[hd * c:(hd + 1) * c, :].astype(_BF), v[:, hd * GLA_DV:(hd + 1) * GLA_DV])
                   for hd in range(GLA_HEADS)]
        o = o + jnp.concatenate(o_intra, axis=1)
        return o, (q * w[:c]).astype(_BF), (k * w[c:]).astype(_BF), w[c - 1:c], v

    chunks = [intra_chunk(slice(i * c, (i + 1) * c)) for i in range(GLA_CHUNKS_PER_STEP)]
    state = state_ref[...]
    for i, (o, q_state, k_state, decay, v) in enumerate(chunks):
        o = o + _dot_nt(q_state, state.astype(_BF))
        state = state * decay + _dot_tn(v, k_state) * bd_ref[...]
        gr = gr_ref[0, i * c:(i + 1) * c, :].astype(_F32)
        for hd in range(GLA_HEADS):
            sl = slice(hd * GLA_DV, (hd + 1) * GLA_DV)
            o_ref[0, i * c:(i + 1) * c, sl] = (_rms(o[:, sl], gn[:, sl]) * _silu(gr[:, sl])).astype(o_ref.dtype)
    state_ref[...] = state


def _gla(gq, gk, la, gv, gr, gla_norm):
    b, s, _ = gq.shape
    rows = GLA_CHUNK * GLA_CHUNKS_PER_STEP
    assert s % rows == 0 and (1 << GLA_LEVELS) == GLA_CHUNK
    consts = list(_gla_constants()) + [gla_norm]
    tok = lambda wd: pl.BlockSpec((1, rows, wd), lambda bi, ci: (bi, ci, 0))
    return pl.pallas_call(
        _gla_kernel,
        grid=(b, s // rows),
        in_specs=[tok(GLA_QK), tok(GLA_QK), tok(GLA_QK), tok(GLA_V), tok(GLA_V)]
        + [_const_spec(cn.shape) for cn in consts],
        out_specs=tok(GLA_V),
        out_shape=jax.ShapeDtypeStruct((b, s, GLA_V), _BF),
        scratch_shapes=[pltpu.VMEM((GLA_V, GLA_QK), _F32)],
        compiler_params=_params(("parallel", "arbitrary")),
        name="gla_chunked",
    )(gq, gk, la, gv, gr, *consts)


def _rotate_half_cols(w):
    half = w.shape[1] // 2
    return jnp.concatenate([-w[:, half:], w[:, :half]], axis=1)


def _pad_cols(w, width):
    return jnp.pad(w, ((0, 0), (0, width - w.shape[1])))


def _prep_even_weights(w_in, w_uq, w_ukv, w_gate2):
    splits = np.cumsum([MLA_Q_RANK, MLA_KV_RANK, MLA_ROPE, GLA_QK, GLA_QK, GLA_V, GLA_GATE_RANK])
    c_q, c_kv, k_r, g_q, g_k, g_v, g_a, g_r = jnp.split(w_in, splits, axis=1)
    win = jnp.concatenate([c_q, c_kv, _pad_cols(k_r, LANES), _pad_cols(_rotate_half_cols(k_r), LANES),
                           g_q, g_k, g_v, _pad_cols(g_a, LANES), g_r], axis=1).astype(_BF)
    uq = w_uq.reshape(MLA_Q_RANK, MLA_HEADS, MLA_NOPE + MLA_ROPE)
    uq_rope = uq[:, :, MLA_NOPE:]
    rot = jnp.concatenate([-uq_rope[:, :, MLA_ROPE // 2:], uq_rope[:, :, :MLA_ROPE // 2]], axis=2)
    wuq = jnp.pad(uq, ((0, 0), (0, 0), (0, MLA_QK_PAD - uq.shape[2]))).reshape(MLA_Q_RANK, -1).astype(_BF)
    wuqr = jnp.pad(rot, ((0, 0), (0, 0), (0, LANES - MLA_ROPE))).reshape(MLA_Q_RANK, -1).astype(_BF)
    ukv = w_ukv.reshape(MLA_KV_RANK, MLA_HEADS, MLA_NOPE + MLA_V)
    wukt = ukv[:, :, :MLA_NOPE].reshape(MLA_KV_RANK, -1).T.astype(_BF)
    wuv = ukv[:, :, MLA_NOPE:].reshape(MLA_KV_RANK, -1).astype(_BF)
    wg2 = jnp.pad(w_gate2, ((0, LANES - GLA_GATE_RANK), (0, 0))).astype(_BF)
    return dict(win=win, wuq=wuq, wuqr=wuqr, wukt=wukt, wuv=wuv, wg2=wg2)


def _rope_tables(positions):
    half = MLA_ROPE // 2
    inv_freq = 1.0 / (ROPE_THETA ** (jnp.arange(half, dtype=_F32) / half))
    ang = positions.astype(_F32)[..., None] * inv_freq
    cos, sin = jnp.cos(ang), jnp.sin(ang)
    pad = jnp.zeros(cos.shape[:-1] + (LANES - MLA_ROPE,), _F32)
    return (jnp.concatenate([cos, cos, pad], axis=-1).reshape(-1, LANES),
            jnp.concatenate([sin, sin, pad], axis=-1).reshape(-1, LANES))


def kernel(x, p, positions, ffn1_norm, ffn1_w_gu, ffn1_w_down, mix_norm, ffn2_norm, ffn2_w_gu, ffn2_w_down, ple_norm, ple_w_gate, ple_w_proj, ev_w_in, ev_q_norm, ev_kv_norm, ev_w_uq, ev_w_ukv, ev_w_gate2, ev_b_gate, ev_gla_norm, ev_w_out, od_w_qkv, od_w_out, final_norm):
    b, s, d = x.shape
    depth = p.shape[0]
    t = b * s
    cos_t, sin_t = _rope_tables(positions)
    row = lambda a: a.reshape(1, -1).astype(_F32)
    xt = x.reshape(t, d)
    p3 = p.reshape(depth, t, p.shape[-1])
    fin = row(final_norm)
    for i in range(depth):
        j = i // 2
        common = dict(n1=row(ffn1_norm[i]), wgu=ffn1_w_gu[i].astype(_BF), wdn=ffn1_w_down[i].astype(_BF),
                      nm=row(mix_norm[i]), n2=row(ffn2_norm[i]), wgu2=ffn2_w_gu[i].astype(_BF),
                      wdn2=ffn2_w_down[i].astype(_BF), np=row(ple_norm[i]), wpg=ple_w_gate[i].astype(_BF),
                      wpp=ple_w_proj[i].astype(_BF))
        if i % 2 == 0:
            w = dict(common, **_prep_even_weights(ev_w_in[j], ev_w_uq[j], ev_w_ukv[j], ev_w_gate2[j]),
                     qn=row(ev_q_norm[j]), kvn=row(ev_kv_norm[j]), bg=row(ev_b_gate[j]))
            x1, q, kt, v, gq, gk, la, gv, gr = _pre_even(xt, cos_t, sin_t, w)
            seq = lambda a: a.reshape(b, s, a.shape[-1])
            o_mla = _mla(seq(q), kt, seq(v))
            o_gla = _gla(seq(gq), seq(gk), seq(la), seq(gv), seq(gr), row(ev_gla_norm[j]))
            mixes = [o_mla.reshape(t, -1), o_gla.reshape(t, -1)]
            w_out = ev_w_out[j].astype(_BF)
            wouts = [w_out[:MLA_HEADS * MLA_V], w_out[MLA_HEADS * MLA_V:]]
        else:
            wqkv = od_w_qkv[j].astype(_BF)
            w = dict(common, wq=wqkv[:, :D_MODEL], wk=wqkv[:, D_MODEL:2 * D_MODEL], wv=wqkv[:, 2 * D_MODEL:])
            x1, q, kt, v = _pre_odd(xt, w)
            seq = lambda a: a.reshape(b, s, a.shape[-1])
            o_sb = _sb(seq(q), kt, seq(v))
            mixes = [o_sb.reshape(t, -1)]
            wouts = [od_w_out[j].astype(_BF)]
        xt = _post(x1, mixes, p3, i, wouts, w, fin, i == depth - 1)
    return xt.reshape(b, s, d)
```

```python
import functools

import numpy as np
import jax
import jax.numpy as jnp
from jax import lax
from jax.experimental import pallas as pl
from jax.experimental.pallas import tpu as pltpu

D_MODEL = 1024
P_DIM = 256
EPS = 1e-6
D_FF = 1408
MLA_HEADS = 4
MLA_Q_RANK = 256
MLA_KV_RANK = 128
MLA_NOPE = 128
MLA_ROPE = 64
MLA_V = 128
ROPE_THETA = 10000.0
GLA_HEADS = 4
GLA_DK = 64
GLA_DV = 128
GLA_GATE_RANK = 16
GLA_GATE_TAU = 16.0
SB_HEADS = 8
SB_HEAD_DIM = D_MODEL // SB_HEADS

LANES = 128
MLA_QK_PAD = 2 * LANES
GLA_QK = GLA_HEADS * GLA_DK
GLA_V = GLA_HEADS * GLA_DV

TOKEN_TILE = 512
MLA_TQ = 512
MLA_TK = 512
MLA_HEADS_PER_STEP = 4
SB_TQ = 256
SB_TK = 256
SB_HEADS_PER_STEP = 4
SB_EXIT_LOG_WEIGHT = -106.0
SB_BOUND_SLACK = 1.001
GLA_CHUNK = 128
GLA_CHUNKS_PER_STEP = 2
GLA_LEVELS = 7
VMEM_LIMIT = 56 * 1024 * 1024
MASK_VALUE = -1e30
MLA_SAFE_SCORE_BOUND = 60.0
MLA_BOUND_SLACK = 1.001
MLA_QK_REAL = 192
LOG2_E = 1.4426950408889634

_BF = jnp.bfloat16
_F32 = jnp.float32


def _dot(a, b):
    return jnp.dot(a, b, preferred_element_type=_F32)


def _dot_nt(a, b):
    return lax.dot_general(a, b, (((1,), (1,)), ((), ())), preferred_element_type=_F32)


def _dot_tn(a, b):
    return lax.dot_general(a, b, (((0,), (0,)), ((), ())), preferred_element_type=_F32)


def _rms(x, w):
    return x * lax.rsqrt(jnp.mean(x * x, axis=-1, keepdims=True) + EPS) * w


def _sigmoid(x):
    return 1.0 / (1.0 + jnp.exp(-x))


def _silu(x):
    return x * _sigmoid(x)


def _log_sigmoid(x):
    return jnp.minimum(x, 0.0) - jnp.log(1.0 + jnp.exp(-jnp.abs(x)))


def _row_sq_norm_bound(x):
    return _dot(x * x, jnp.ones((x.shape[1], LANES), x.dtype)) * (1.0 + 2.0 ** -8)


def _ffn(x, norm_w, w_gu, w_down):
    h = _rms(x, norm_w).astype(_BF)
    gu = _dot(h, w_gu)
    act = (_silu(gu[:, :D_FF]) * gu[:, D_FF:]).astype(_BF)
    return x + 0.5 * _dot(act, w_down)


def _const_spec(shape):
    nd = len(shape)
    return pl.BlockSpec(shape, lambda *_: (0,) * nd, pipeline_mode=pl.Buffered(1))


def _params(semantics):
    return pltpu.CompilerParams(dimension_semantics=semantics, vmem_limit_bytes=VMEM_LIMIT)


def _pre_even_kernel(x_ref, cos_ref, sin_ref, n1_ref, wgu_ref, wdn_ref, nm_ref, win_ref, qn_ref, kvn_ref,
                     wuq_ref, wuqr_ref, wukt_ref, wuv_ref, wg2_ref, bg_ref,
                     x1_ref, q_ref, kt_ref, v_ref, gq_ref, gk_ref, la_ref, gv_ref, gr_ref):
    x1 = _ffn(x_ref[...], n1_ref[...], wgu_ref[...], wdn_ref[...])
    x1_ref[...] = x1
    h = _rms(x1, nm_ref[...]).astype(_BF)
    proj = _dot(h, win_ref[...])
    c_q = proj[:, 0:256]
    c_kv = proj[:, 256:384]
    k_r = proj[:, 384:512]
    k_rr = proj[:, 512:640]
    g_q = proj[:, 640:896]
    g_k = proj[:, 896:1152]
    g_v = proj[:, 1152:1664]
    g_a = proj[:, 1664:1792]
    g_r = proj[:, 1792:2304]
    cos = cos_ref[...]
    sin = sin_ref[...]

    cq_n = _rms(c_q, qn_ref[...]).astype(_BF)
    q = _dot(cq_n, wuq_ref[...])
    q_rot = _dot(cq_n, wuqr_ref[...])
    scale = (MLA_NOPE + MLA_ROPE) ** -0.5 * LOG2_E
    for hd in range(MLA_HEADS):
        lo = hd * MLA_QK_PAD
        q_ref[:, lo:lo + LANES] = (q[:, lo:lo + LANES] * scale).astype(_BF)
        q_rope = q[:, lo + LANES:lo + 2 * LANES] * cos + q_rot[:, hd * LANES:(hd + 1) * LANES] * sin
        q_ref[:, lo + LANES:lo + 2 * LANES] = (q_rope * scale).astype(_BF)
    ckv_n = _rms(c_kv, kvn_ref[...]).astype(_BF)
    k_nope_t = _dot_nt(wukt_ref[...], ckv_n)
    k_rope_t = (k_r * cos + k_rr * sin).T
    pad_row = lax.broadcasted_iota(jnp.int32, k_rope_t.shape, 0) == MLA_ROPE
    k_rope_t = jnp.where(pad_row, 1.0, k_rope_t).astype(_BF)
    for hd in range(MLA_HEADS):
        lo = hd * MLA_QK_PAD
        kt_ref[0, lo:lo + LANES, :] = k_nope_t[hd * LANES:(hd + 1) * LANES].astype(_BF)
        kt_ref[0, lo + LANES:lo + 2 * LANES, :] = k_rope_t
    v_ref[...] = _dot(ckv_n, wuv_ref[...]).astype(_BF)

    gq_ref[...] = g_q * (GLA_DK ** -0.5)
    gk_ref[...] = g_k
    gate = _dot(g_a.astype(_BF), wg2_ref[...]) + bg_ref[...]
    la_ref[...] = _log_sigmoid(gate) * (1.0 / GLA_GATE_TAU)
    gv_ref[...] = g_v.astype(_BF)
    gr_ref[...] = g_r.astype(_BF)


def _pre_odd_kernel(x_ref, n1_ref, wgu_ref, wdn_ref, nm_ref, wq_ref, wk_ref, wv_ref,
                    x1_ref, q_ref, kt_ref, v_ref):
    x1 = _ffn(x_ref[...], n1_ref[...], wgu_ref[...], wdn_ref[...])
    x1_ref[...] = x1
    h = _rms(x1, nm_ref[...]).astype(_BF)
    q_ref[...] = (_dot(h, wq_ref[...]) * (SB_HEAD_DIM ** -0.5)).astype(_BF)
    kt = _dot(h, wk_ref[...]).T
    for j in range(kt_ref.shape[0]):
        kt_ref[j] = kt[:, j * SB_TK:(j + 1) * SB_TK].astype(_BF)
    v_ref[...] = _dot(h, wv_ref[...]).astype(_BF)


def _row_spec(width, tm):
    return pl.BlockSpec((tm, width), lambda i: (i, 0))


def _pre_even(x, cos_t, sin_t, w):
    t = x.shape[0]
    tm = min(TOKEN_TILE, t)
    consts = [w["n1"], w["wgu"], w["wdn"], w["nm"], w["win"], w["qn"], w["kvn"], w["wuq"], w["wuqr"],
              w["wukt"], w["wuv"], w["wg2"], w["bg"]]
    assert tm == MLA_TK
    kt_rows = MLA_HEADS * MLA_QK_PAD
    out_widths = [(D_MODEL, _F32), (MLA_HEADS * MLA_QK_PAD, _BF), None,
                  (MLA_HEADS * MLA_V, _BF), (GLA_QK, _F32), (GLA_QK, _F32), (GLA_QK, _F32),
                  (GLA_V, _BF), (GLA_V, _BF)]
    out_specs = [pl.BlockSpec((1, kt_rows, tm), lambda i: (i, 0, 0)) if o is None else _row_spec(o[0], tm)
                 for o in out_widths]
    out_shape = [jax.ShapeDtypeStruct((t // tm, kt_rows, tm), _BF) if o is None
                 else jax.ShapeDtypeStruct((t, o[0]), o[1]) for o in out_widths]
    return pl.pallas_call(
        _pre_even_kernel,
        grid=(t // tm,),
        in_specs=[_row_spec(D_MODEL, tm), _row_spec(LANES, tm), _row_spec(LANES, tm)]
        + [_const_spec(c.shape) for c in consts],
        out_specs=out_specs,
        out_shape=out_shape,
        compiler_params=_params(("parallel",)),
        name="pre_even",
    )(x, cos_t, sin_t, *consts)


def _pre_odd(x, w):
    t = x.shape[0]
    tm = min(TOKEN_TILE, t)
    consts = [w["n1"], w["wgu"], w["wdn"], w["nm"], w["wq"], w["wk"], w["wv"]]
    assert tm % SB_TK == 0
    out_widths = [(D_MODEL, _F32), (D_MODEL, _BF), None, (D_MODEL, _BF)]
    out_specs = [pl.BlockSpec((tm // SB_TK, D_MODEL, SB_TK), lambda i: (i, 0, 0)) if o is None
                 else _row_spec(o[0], tm) for o in out_widths]
    out_shape = [jax.ShapeDtypeStruct((t // SB_TK, D_MODEL, SB_TK), _BF) if o is None
                 else jax.ShapeDtypeStruct((t, o[0]), o[1]) for o in out_widths]
    return pl.pallas_call(
        _pre_odd_kernel,
        grid=(t // tm,),
        in_specs=[_row_spec(D_MODEL, tm)] + [_const_spec(c.shape) for c in consts],
        out_specs=out_specs,
        out_shape=out_shape,
        compiler_params=_params(("parallel",)),
        name="pre_odd",
    )(x, *consts)


def _post_kernel(n_mix, final, *refs):
    x_ref = refs[0]
    mix_refs = refs[1:1 + n_mix]
    p_ref = refs[1 + n_mix]
    wout_refs = refs[2 + n_mix:2 + 2 * n_mix]
    n2_ref, wgu_ref, wdn_ref, np_ref, wpg_ref, wpp_ref, nf_ref, out_ref = refs[2 + 2 * n_mix:]
    x = x_ref[...]
    for m_ref, w_ref in zip(mix_refs, wout_refs):
        x = x + _dot(m_ref[...], w_ref[...])
    x = _ffn(x, n2_ref[...], wgu_ref[...], wdn_ref[...])
    gate = _sigmoid(_dot(_rms(x, np_ref[...]).astype(_BF), wpg_ref[...]))
    x = x + gate * _dot(p_ref[...].astype(_BF), wpp_ref[...])
    if final:
        x = _rms(x, nf_ref[...])
    out_ref[...] = x


def _post(x, mixes, p, layer, wouts, w, final_norm, final):
    t = x.shape[0]
    tm = min(TOKEN_TILE, t)
    consts = list(wouts) + [w["n2"], w["wgu2"], w["wdn2"], w["np"], w["wpg"], w["wpp"], final_norm]
    return pl.pallas_call(
        functools.partial(_post_kernel, len(mixes), final),
        grid=(t // tm,),
        in_specs=[_row_spec(D_MODEL, tm)] + [_row_spec(m.shape[1], tm) for m in mixes]
        + [pl.BlockSpec((None, tm, P_DIM), lambda i: (layer, i, 0))] + [_const_spec(c.shape) for c in consts],
        out_specs=_row_spec(D_MODEL, tm),
        out_shape=jax.ShapeDtypeStruct((t, D_MODEL), _F32),
        compiler_params=_params(("parallel",)),
        name="post_final" if final else "post",
    )(x, *mixes, p, *consts)


def _mla_kernel(q_ref, kt_ref, v_ref, o_ref, m_ref, l_ref, acc_ref, qs_ref, kmax_ref):
    qi = pl.program_id(2)
    heads = range(MLA_HEADS_PER_STEP)
    slot = lambda hd: slice(hd * MLA_QK_PAD, (hd + 1) * MLA_QK_PAD)
    vslot = lambda hd: slice(hd * MLA_V, (hd + 1) * MLA_V)
    reps = MLA_TK // LANES

    @pl.when(qi == 0)
    def _():
        def key_norm(i, mx):
            kc = kt_ref[i].astype(_F32)
            kc = kc * kc
            ssq = [jnp.sum(kc[hd * MLA_QK_PAD:hd * MLA_QK_PAD + MLA_QK_REAL], axis=0, keepdims=True)
                   for hd in heads]
            return tuple(jnp.maximum(m, s) for m, s in zip(mx, ssq))
        mx = lax.fori_loop(0, kt_ref.shape[0], key_norm, tuple(jnp.zeros((1, MLA_TK), _F32) for _ in heads))
        for hd in heads:
            kmax_ref[hd] = jnp.broadcast_to(jnp.max(mx[hd], axis=1, keepdims=True), kmax_ref.shape[1:])

    bounds = []
    for hd in heads:
        q_sq = jnp.max(_row_sq_norm_bound(q_ref[0, :, slot(hd)]), axis=0, keepdims=True)
        bounds.append(jnp.sqrt(q_sq * kmax_ref[hd, 0:1, :]) * MLA_BOUND_SLACK)
    worst = bounds[0]
    for bnd in bounds[1:]:
        worst = jnp.maximum(worst, bnd)
    fast = jnp.max(worst) <= MLA_SAFE_SCORE_BOUND

    l_ref[...] = jnp.zeros(l_ref.shape, _F32)
    acc_ref[...] = jnp.zeros(acc_ref.shape, _F32)
    n_full = (qi * MLA_TQ) // MLA_TK

    def causal_mask(s, start):
        row = lax.broadcasted_iota(jnp.int32, s.shape, 0) + qi * MLA_TQ
        col = lax.broadcasted_iota(jnp.int32, s.shape, 1) + start
        return jnp.where(col <= row, s, MASK_VALUE)

    def run(step):
        def body(c, carry):
            step(c, False)
            return carry
        lax.fori_loop(0, n_full, body, 0)
        step(n_full, True)

    @pl.when(fast)
    def _():
        for hd in heads:
            lane = lax.broadcasted_iota(jnp.int32, (MLA_TQ, MLA_QK_PAD), 1)
            shift = jnp.broadcast_to(jnp.concatenate([-bounds[hd]] * (MLA_QK_PAD // LANES), axis=1),
                                     (MLA_TQ, MLA_QK_PAD)).astype(_BF)
            qs_ref[hd] = jnp.where(lane == MLA_QK_REAL, shift, q_ref[0, :, slot(hd)])

        def step(c, masked):
            start = pl.multiple_of(c * MLA_TK, MLA_TK)
            for hd in heads:
                s = _dot(qs_ref[hd], kt_ref[c, slot(hd), :])
                if masked:
                    s = causal_mask(s, start)
                p = jnp.exp2(s)
                part = p[:, :LANES]
                for r in range(1, reps):
                    part = part + p[:, r * LANES:(r + 1) * LANES]
                l_ref[hd] += part
                acc_ref[hd] += _dot(p.astype(_BF), v_ref[0, pl.ds(start, MLA_TK), vslot(hd)])

        run(step)
        for hd in heads:
            denom = jnp.sum(l_ref[hd], axis=1, keepdims=True)
            o_ref[0, :, vslot(hd)] = (acc_ref[hd] / denom).astype(o_ref.dtype)

    @pl.when(jnp.logical_not(fast))
    def _():
        m_ref[...] = jnp.full(m_ref.shape, MASK_VALUE, _F32)

        def step(c, masked):
            start = pl.multiple_of(c * MLA_TK, MLA_TK)
            for hd in heads:
                s = _dot(q_ref[0, :, slot(hd)], kt_ref[c, slot(hd), :])
                if masked:
                    s = causal_mask(s, start)
                m_prev = m_ref[hd]
                m_next = jnp.maximum(m_prev, jnp.max(s, axis=1, keepdims=True))
                p = jnp.exp2(s - jnp.concatenate([m_next] * reps, axis=1))
                alpha = jnp.exp2(m_prev - m_next)
                l_ref[hd] = alpha * l_ref[hd] + jnp.sum(p, axis=1, keepdims=True)
                acc_ref[hd] = alpha * acc_ref[hd] + _dot(p.astype(_BF), v_ref[0, pl.ds(start, MLA_TK), vslot(hd)])
                m_ref[hd] = m_next

        run(step)
        for hd in heads:
            o_ref[0, :, vslot(hd)] = (acc_ref[hd] / l_ref[hd]).astype(o_ref.dtype)


def _mla(q, kt, v):
    b, s, _ = q.shape
    hb = MLA_HEADS_PER_STEP
    n_chunks = s // MLA_TK
    assert MLA_TK % MLA_TQ == 0 and s % MLA_TK == 0 and MLA_HEADS % hb == 0
    resident = dict(pipeline_mode=pl.Buffered(1))
    return pl.pallas_call(
        _mla_kernel,
        grid=(b, MLA_HEADS // hb, s // MLA_TQ),
        in_specs=[pl.BlockSpec((1, MLA_TQ, hb * MLA_QK_PAD), lambda bi, hi, qi: (bi, qi, hi)),
                  pl.BlockSpec((n_chunks, hb * MLA_QK_PAD, MLA_TK), lambda bi, hi, qi: (bi, hi, 0), **resident),
                  pl.BlockSpec((1, s, hb * MLA_V), lambda bi, hi, qi: (bi, 0, hi), **resident)],
        out_specs=pl.BlockSpec((1, MLA_TQ, hb * MLA_V), lambda bi, hi, qi: (bi, qi, hi)),
        out_shape=jax.ShapeDtypeStruct((b, s, MLA_HEADS * MLA_V), _BF),
        scratch_shapes=[pltpu.VMEM((hb, MLA_TQ, LANES), _F32)] * 3
        + [pltpu.VMEM((hb, MLA_TQ, MLA_QK_PAD), _BF), pltpu.VMEM((hb, 8, LANES), _F32)],
        compiler_params=_params(("parallel", "parallel", "arbitrary")),
        name="mla_attention",
    )(q, kt, v)


def _sb_kernel(q_ref, kt_ref, v_ref, tri_ref, o_ref, suf_ref, acc_ref, zb_ref, kmax_ref):
    qi = pl.program_id(2)
    n_key_chunks = kt_ref.shape[0]
    heads = list(range(SB_HEADS_PER_STEP))
    lanes_of = lambda hd: slice(hd * SB_HEAD_DIM, (hd + 1) * SB_HEAD_DIM)

    @pl.when(qi == 0)
    def _():
        def key_norm(i, mx):
            kc = kt_ref[i].astype(_F32)
            kc = kc * kc
            ssq = [jnp.sum(kc[lanes_of(hd)], axis=0, keepdims=True) for hd in heads]
            return tuple(jnp.maximum(m, s) for m, s in zip(mx, ssq))
        mx = lax.fori_loop(0, n_key_chunks, key_norm, tuple(jnp.zeros((1, SB_TK), _F32) for _ in heads))
        for hd in heads:
            kmax_ref[hd] = jnp.broadcast_to(jnp.max(mx[hd], axis=1, keepdims=True), kmax_ref.shape[1:])

    tri = tri_ref[...]

    def chunk_group(items, sufs):
        halves = list(reversed(range(SB_TK // LANES)))
        zs, ts, masks = [], [], []
        for hd, c, masked in items:
            z = _dot(q_ref[0, :, lanes_of(hd)], kt_ref[c, lanes_of(hd), :])
            zs.append(z)
        for z, (hd, c, masked) in zip(zs, items):
            t = jnp.maximum(z, 0.0) + jnp.log(1.0 + jnp.exp(-jnp.abs(z)))
            causal = None
            if masked:
                causal = (lax.broadcasted_iota(jnp.int32, z.shape, 1)
                          < lax.broadcasted_iota(jnp.int32, z.shape, 0))
                t = jnp.where(causal, t, 0.0)
            ts.append(t)
            masks.append(causal)
        sums = []
        for t in ts:
            per_half = {}
            for half in halves:
                th = t[:, half * LANES:(half + 1) * LANES]
                t_hi = th.astype(_BF)
                t_lo = (th - t_hi.astype(_F32)).astype(_BF)
                per_half[half] = _dot(jnp.concatenate([t_hi, t_lo], axis=1), tri)
            sums.append(per_half)
        outs = []
        for z, per_half, causal, (hd, c, masked) in zip(zs, sums, masks, items):
            suf = sufs[hd]
            log_w = {}
            for half in halves:
                log_w[half] = z[:, half * LANES:(half + 1) * LANES] - per_half[half][:, :LANES] - suf
                suf = suf + per_half[half][:, LANES:]
            sufs[hd] = suf
            a = jnp.exp(jnp.concatenate([log_w[h] for h in sorted(halves)], axis=1))
            if masked:
                a = jnp.where(causal, a, 0.0)
            start = pl.multiple_of(c * SB_TK, SB_TK)
            outs.append(_dot(a.astype(_BF), v_ref[0, pl.ds(start, SB_TK), lanes_of(hd)]))
        return outs

    def still_live(sufs):
        slack = zb_ref[0, 0:1, :] - sufs[0]
        for hd in heads[1:]:
            slack = jnp.maximum(slack, zb_ref[hd, 0:1, :] - sufs[hd])
        return (jnp.max(slack) >= SB_EXIT_LOG_WEIGHT).astype(jnp.int32)

    zero_suf = lambda: {hd: jnp.zeros((SB_TQ, LANES), _F32) for hd in heads}

    @pl.when(qi == 0)
    def _():
        outs = chunk_group([(hd, 0, True) for hd in heads], zero_suf())
        for hd in heads:
            o_ref[0, :, lanes_of(hd)] = outs[hd].astype(o_ref.dtype)

    @pl.when(qi > 0)
    def _():
        for hd in heads:
            q_sq = jnp.max(_row_sq_norm_bound(q_ref[0, :, lanes_of(hd)]), axis=0, keepdims=True)
            zb_ref[hd] = jnp.broadcast_to(jnp.sqrt(q_sq * kmax_ref[hd, 0:1, :]) * SB_BOUND_SLACK,
                                          zb_ref.shape[1:])
        sufs = zero_suf()
        items = [(hd, qi, True) for hd in heads] + [(hd, qi - 1, False) for hd in heads]
        outs = chunk_group(items, sufs)
        for hd in heads:
            acc_ref[hd] = outs[hd] + outs[len(heads) + hd]
            suf_ref[hd] = sufs[hd]

        def cond(carry):
            c, live = carry
            return jnp.logical_and(c >= 0, live > 0)

        def body(carry):
            c, _ = carry
            sufs = {hd: suf_ref[hd] for hd in heads}
            outs = chunk_group([(hd, c, False) for hd in heads], sufs)
            for hd in heads:
                acc_ref[hd] += outs[hd]
                suf_ref[hd] = sufs[hd]
            return c - 1, still_live(sufs)

        lax.while_loop(cond, body, (qi - 2, still_live(sufs)))
        for hd in heads:
            o_ref[0, :, lanes_of(hd)] = acc_ref[hd].astype(o_ref.dtype)


def _sb_tri():
    r = np.arange(LANES)
    upper = (r[:, None] >= r[None, :]).astype(np.float32)
    half = np.concatenate([upper, np.ones((LANES, LANES), np.float32)], axis=1)
    return jnp.asarray(np.concatenate([half, half], axis=0), dtype=_BF)


def _sb(q, kt, v):
    b, s, _ = q.shape
    hb = SB_HEADS_PER_STEP
    assert SB_TQ == SB_TK and SB_TK % LANES == 0 and s % SB_TQ == 0 and SB_HEADS % hb == 0
    tri = _sb_tri()
    width = hb * SB_HEAD_DIM
    return pl.pallas_call(
        _sb_kernel,
        grid=(b, SB_HEADS // hb, s // SB_TQ),
        in_specs=[pl.BlockSpec((1, SB_TQ, width), lambda bi, hi, qi: (bi, qi, hi)),
                  pl.BlockSpec((s // SB_TK, width, SB_TK), lambda bi, hi, qi: (bi, hi, 0)),
                  pl.BlockSpec((1, s, width), lambda bi, hi, qi: (bi, 0, hi)),
                  _const_spec(tri.shape)],
        out_specs=pl.BlockSpec((1, SB_TQ, width), lambda bi, hi, qi: (bi, qi, hi)),
        out_shape=jax.ShapeDtypeStruct((b, s, D_MODEL), _BF),
        scratch_shapes=[pltpu.VMEM((hb, SB_TQ, LANES), _F32)] * 2 + [pltpu.VMEM((hb, 8, LANES), _F32)] * 2,
        compiler_params=_params(("parallel", "parallel", "arbitrary")),
        name="sb_attention",
    )(q, kt, v, tri)


def _gla_constants():
    c = GLA_CHUNK
    idx = np.arange(c)
    seg = []
    masks = []
    for lvl in range(GLA_LEVELS):
        s = 1 << lvl
        blk = idx // s
        start = blk * s
        end = start + s - 1
        t = idx[None, :]
        q_side = ((blk % 2 == 1)[:, None] & (t >= start[:, None]) & (t <= idx[:, None]))
        k_side = ((blk % 2 == 0)[:, None] & (t > idx[:, None]) & (t <= end[:, None]))
        seg.append(np.where((blk % 2 == 1)[:, None], q_side, k_side))
        pair = ((idx[:, None] // (2 * s)) == (idx[None, :] // (2 * s))) \
            & ((blk % 2 == 1)[:, None]) & ((blk % 2 == 0)[None, :])
        masks.append(np.tile(pair, (GLA_HEADS, 1)))
    t = idx[None, :]
    full = np.concatenate([t <= idx[:, None], t > idx[:, None]], axis=0)
    seg = np.stack(seg).astype(np.float32)
    masks = np.stack(masks).astype(np.float32)
    head_of_qk = np.arange(GLA_QK) // GLA_DK
    head_of_v = np.arange(GLA_V) // GLA_DV
    head_mask = (np.arange(8)[:, None] == head_of_qk[None, :]).astype(np.float32)
    expand = (head_of_qk[:, None] == head_of_v[None, :]).astype(np.float32)
    return (jnp.asarray(seg, _BF), jnp.asarray(full, _BF), jnp.asarray(masks, _F32), jnp.asarray(head_mask, _F32),
            jnp.asarray(expand, _BF), jnp.asarray(expand.T, _F32))


def _gla_kernel(gq_ref, gk_ref, la_ref, gv_ref, gr_ref, seg_ref, full_ref, mask_ref, hm_ref, exp_ref, bd_ref,
                gn_ref, o_ref, state_ref):
    c = GLA_CHUNK

    @pl.when(pl.program_id(1) == 0)
    def _():
        state_ref[...] = jnp.zeros(state_ref.shape, _F32)

    hm = hm_ref[...]
    gn = gn_ref[...]

    def intra_chunks(row_slices):
        qs_, ks_, vs_, la2s = [], [], [], []
        for rows in row_slices:
            la = la_ref[0, rows, :]
            la_hi = la.astype(_BF)
            la_lo = (la - la_hi.astype(_F32)).astype(_BF)
            la2s.append(jnp.concatenate([la_hi, la_lo], axis=1))
            qs_.append(gq_ref[0, rows, :])
            ks_.append(gk_ref[0, rows, :])
            vs_.append(gv_ref[0, rows, :])
        segs = [seg_ref[lvl] for lvl in range(GLA_LEVELS)] + [full_ref[...]]
        sums = [[_dot(seg, la2) for seg in segs] for la2 in la2s]
        ws = [[jnp.exp(e[:, :GLA_QK] + e[:, GLA_QK:]) for e in per_chunk] for per_chunk in sums]
        pairs = []
        for q, k, w_chunk in zip(qs_, ks_, ws):
            per_level = []
            for lvl in range(GLA_LEVELS):
                w = w_chunk[lvl]
                ql = q * w
                kl = (k * w).astype(_BF)
                q_heads = jnp.concatenate([ql * hm[hd:hd + 1] for hd in range(GLA_HEADS)], axis=0).astype(_BF)
                per_level.append((q_heads, kl))
            pairs.append(per_level)
        scores = [[_dot_nt(q_heads, kl) for q_heads, kl in per_level] for per_level in pairs]
        out = []
        for q, k, v, w_chunk, per_level in zip(qs_, ks_, vs_, ws, scores):
            att = per_level[0] * mask_ref[0]
            for lvl in range(1, GLA_LEVELS):
                att = att + per_level[lvl] * mask_ref[lvl]
            w = w_chunk[GLA_LEVELS]
            o = _dot((q * k).astype(_BF), exp_ref[...]) * v.astype(_F32)
            o_intra = [_dot(att[hd * c:(hd + 1) * c, :].astype(_BF), v[:, hd * GLA_DV:(hd + 1) * GLA_DV])
                       for hd in range(GLA_HEADS)]
            o = o + jnp.concatenate(o_intra, axis=1)
            out.append((o, (q * w[:c]).astype(_BF), (k * w[c:]).astype(_BF), w[c - 1:c], v))
        return out

    chunks = intra_chunks([slice(i * c, (i + 1) * c) for i in range(GLA_CHUNKS_PER_STEP)])
    state = state_ref[...]
    for i, (o, q_state, k_state, decay, v) in enumerate(chunks):
        o = o + _dot_nt(q_state, state.astype(_BF))
        state = state * decay + _dot_tn(v, k_state) * bd_ref[...]
        gr = gr_ref[0, i * c:(i + 1) * c, :].astype(_F32)
        for hd in range(GLA_HEADS):
            sl = slice(hd * GLA_DV, (hd + 1) * GLA_DV)
            o_ref[0, i * c:(i + 1) * c, sl] = (_rms(o[:, sl], gn[:, sl]) * _silu(gr[:, sl])).astype(o_ref.dtype)
    state_ref[...] = state


def _gla(gq, gk, la, gv, gr, gla_norm):
    b, s, _ = gq.shape
    rows = GLA_CHUNK * GLA_CHUNKS_PER_STEP
    assert s % rows == 0 and (1 << GLA_LEVELS) == GLA_CHUNK
    consts = list(_gla_constants()) + [gla_norm]
    tok = lambda wd: pl.BlockSpec((1, rows, wd), lambda bi, ci: (bi, ci, 0))
    return pl.pallas_call(
        _gla_kernel,
        grid=(b, s // rows),
        in_specs=[tok(GLA_QK), tok(GLA_QK), tok(GLA_QK), tok(GLA_V), tok(GLA_V)]
        + [_const_spec(cn.shape) for cn in consts],
        out_specs=tok(GLA_V),
        out_shape=jax.ShapeDtypeStruct((b, s, GLA_V), _BF),
        scratch_shapes=[pltpu.VMEM((GLA_V, GLA_QK), _F32)],
        compiler_params=_params(("parallel", "arbitrary")),
        name="gla_chunked",
    )(gq, gk, la, gv, gr, *consts)


def _rotate_half_cols(w):
    half = w.shape[1] // 2
    return jnp.concatenate([-w[:, half:], w[:, :half]], axis=1)


def _pad_cols(w, width):
    return jnp.pad(w, ((0, 0), (0, width - w.shape[1])))


def _prep_even_weights(w_in, w_uq, w_ukv, w_gate2):
    splits = np.cumsum([MLA_Q_RANK, MLA_KV_RANK, MLA_ROPE, GLA_QK, GLA_QK, GLA_V, GLA_GATE_RANK])
    c_q, c_kv, k_r, g_q, g_k, g_v, g_a, g_r = jnp.split(w_in, splits, axis=1)
    win = jnp.concatenate([c_q, c_kv, _pad_cols(k_r, LANES), _pad_cols(_rotate_half_cols(k_r), LANES),
                           g_q, g_k, g_v, _pad_cols(g_a, LANES), g_r], axis=1).astype(_BF)
    uq = w_uq.reshape(MLA_Q_RANK, MLA_HEADS, MLA_NOPE + MLA_ROPE)
    uq_rope = uq[:, :, MLA_NOPE:]
    rot = jnp.concatenate([-uq_rope[:, :, MLA_ROPE // 2:], uq_rope[:, :, :MLA_ROPE // 2]], axis=2)
    wuq = jnp.pad(uq, ((0, 0), (0, 0), (0, MLA_QK_PAD - uq.shape[2]))).reshape(MLA_Q_RANK, -1).astype(_BF)
    wuqr = jnp.pad(rot, ((0, 0), (0, 0), (0, LANES - MLA_ROPE))).reshape(MLA_Q_RANK, -1).astype(_BF)
    ukv = w_ukv.reshape(MLA_KV_RANK, MLA_HEADS, MLA_NOPE + MLA_V)
    wukt = ukv[:, :, :MLA_NOPE].reshape(MLA_KV_RANK, -1).T.astype(_BF)
    wuv = ukv[:, :, MLA_NOPE:].reshape(MLA_KV_RANK, -1).astype(_BF)
    wg2 = jnp.pad(w_gate2, ((0, LANES - GLA_GATE_RANK), (0, 0))).astype(_BF)
    return dict(win=win, wuq=wuq, wuqr=wuqr, wukt=wukt, wuv=wuv, wg2=wg2)


def _rope_tables(positions):
    half = MLA_ROPE // 2
    inv_freq = 1.0 / (ROPE_THETA ** (jnp.arange(half, dtype=_F32) / half))
    ang = positions.astype(_F32)[..., None] * inv_freq
    cos, sin = jnp.cos(ang), jnp.sin(ang)
    pad = jnp.zeros(cos.shape[:-1] + (LANES - MLA_ROPE,), _F32)
    return (jnp.concatenate([cos, cos, pad], axis=-1).reshape(-1, LANES),
            jnp.concatenate([sin, sin, pad], axis=-1).reshape(-1, LANES))


def kernel(x, p, positions, ffn1_norm, ffn1_w_gu, ffn1_w_down, mix_norm, ffn2_norm, ffn2_w_gu, ffn2_w_down, ple_norm, ple_w_gate, ple_w_proj, ev_w_in, ev_q_norm, ev_kv_norm, ev_w_uq, ev_w_ukv, ev_w_gate2, ev_b_gate, ev_gla_norm, ev_w_out, od_w_qkv, od_w_out, final_norm):
    b, s, d = x.shape
    depth = p.shape[0]
    t = b * s
    cos_t, sin_t = _rope_tables(positions)
    row = lambda a: a.reshape(1, -1).astype(_F32)
    xt = x.reshape(t, d)
    p3 = p.reshape(depth, t, p.shape[-1])
    fin = row(final_norm)
    for i in range(depth):
        j = i // 2
        common = dict(n1=row(ffn1_norm[i]), wgu=ffn1_w_gu[i].astype(_BF), wdn=ffn1_w_down[i].astype(_BF),
                      nm=row(mix_norm[i]), n2=row(ffn2_norm[i]), wgu2=ffn2_w_gu[i].astype(_BF),
                      wdn2=ffn2_w_down[i].astype(_BF), np=row(ple_norm[i]), wpg=ple_w_gate[i].astype(_BF),
                      wpp=ple_w_proj[i].astype(_BF))
        if i % 2 == 0:
            w = dict(common, **_prep_even_weights(ev_w_in[j], ev_w_uq[j], ev_w_ukv[j], ev_w_gate2[j]),
                     qn=row(ev_q_norm[j]), kvn=row(ev_kv_norm[j]), bg=row(ev_b_gate[j]))
            x1, q, kt, v, gq, gk, la, gv, gr = _pre_even(xt, cos_t, sin_t, w)
            seq = lambda a: a.reshape(b, s, a.shape[-1])
            o_mla = _mla(seq(q), kt, seq(v))
            o_gla = _gla(seq(gq), seq(gk), seq(la), seq(gv), seq(gr), row(ev_gla_norm[j]))
            mixes = [o_mla.reshape(t, -1), o_gla.reshape(t, -1)]
            w_out = ev_w_out[j].astype(_BF)
            wouts = [w_out[:MLA_HEADS * MLA_V], w_out[MLA_HEADS * MLA_V:]]
        else:
            wqkv = od_w_qkv[j].astype(_BF)
            w = dict(common, wq=wqkv[:, :D_MODEL], wk=wqkv[:, D_MODEL:2 * D_MODEL], wv=wqkv[:, 2 * D_MODEL:])
            x1, q, kt, v = _pre_odd(xt, w)
            seq = lambda a: a.reshape(b, s, a.shape[-1])
            o_sb = _sb(seq(q), kt, seq(v))
            mixes = [o_sb.reshape(t, -1)]
            wouts = [od_w_out[j].astype(_BF)]
        xt = _post(x1, mixes, p3, i, wouts, w, fin, i == depth - 1)
    return xt.reshape(b, s, d)
```

```python
import functools

import numpy as np
import jax
import jax.numpy as jnp
from jax import lax
from jax.experimental import pallas as pl
from jax.experimental.pallas import tpu as pltpu

D_MODEL = 1024
P_DIM = 256
EPS = 1e-6
D_FF = 1408
MLA_HEADS = 4
MLA_Q_RANK = 256
MLA_KV_RANK = 128
MLA_NOPE = 128
MLA_ROPE = 64
MLA_V = 128
ROPE_THETA = 10000.0
GLA_HEADS = 4
GLA_DK = 64
GLA_DV = 128
GLA_GATE_RANK = 16
GLA_GATE_TAU = 16.0
SB_HEADS = 8
SB_HEAD_DIM = D_MODEL // SB_HEADS

LANES = 128
MLA_QK_PAD = 2 * LANES
GLA_QK = GLA_HEADS * GLA_DK
GLA_V = GLA_HEADS * GLA_DV

TOKEN_TILE = 512
TOKEN_ROW_GROUPS = 2
MLA_TQ = 512
MLA_TK = 512
MLA_HEADS_PER_STEP = 4
SB_TQ = 256
SB_TK = 256
SB_HEADS_PER_STEP = 4
SB_EXIT_LOG_WEIGHT = -106.0
SB_BOUND_SLACK = 1.001
GLA_CHUNK = 128
GLA_CHUNKS_PER_STEP = 2
GLA_LEVELS = 7
VMEM_LIMIT = 56 * 1024 * 1024
MASK_VALUE = -1e30
MLA_SAFE_SCORE_BOUND = 60.0
MLA_BOUND_SLACK = 1.001
MLA_QK_REAL = 192
LOG2_E = 1.4426950408889634

_BF = jnp.bfloat16
_F32 = jnp.float32


def _dot(a, b):
    return jnp.dot(a, b, preferred_element_type=_F32)


def _dot_nt(a, b):
    return lax.dot_general(a, b, (((1,), (1,)), ((), ())), preferred_element_type=_F32)


def _dot_tn(a, b):
    return lax.dot_general(a, b, (((0,), (0,)), ((), ())), preferred_element_type=_F32)


def _rms(x, w):
    return x * lax.rsqrt(jnp.mean(x * x, axis=-1, keepdims=True) + EPS) * w


def _sigmoid(x):
    return 1.0 / (1.0 + jnp.exp(-x))


def _silu(x):
    return x * _sigmoid(x)


def _log_sigmoid(x):
    return jnp.minimum(x, 0.0) - jnp.log(1.0 + jnp.exp(-jnp.abs(x)))


def _row_sq_norm_bound(x):
    return _dot(x * x, jnp.ones((x.shape[1], LANES), x.dtype)) * (1.0 + 2.0 ** -8)


def _row_groups(rows):
    step = rows // TOKEN_ROW_GROUPS
    return [slice(i * step, (i + 1) * step) for i in range(TOKEN_ROW_GROUPS)]


def _ffn_groups(xs, norm_w, w_gu, w_down):
    hs = [_rms(x, norm_w).astype(_BF) for x in xs]
    gus = [_dot(h, w_gu) for h in hs]
    acts = [(_silu(gu[:, :D_FF]) * gu[:, D_FF:]).astype(_BF) for gu in gus]
    downs = [_dot(act, w_down) for act in acts]
    return [x + 0.5 * d for x, d in zip(xs, downs)]


def _const_spec(shape):
    nd = len(shape)
    return pl.BlockSpec(shape, lambda *_: (0,) * nd, pipeline_mode=pl.Buffered(1))


def _params(semantics):
    return pltpu.CompilerParams(dimension_semantics=semantics, vmem_limit_bytes=VMEM_LIMIT)


def _pre_even_kernel(x_ref, cos_ref, sin_ref, n1_ref, wgu_ref, wdn_ref, nm_ref, win_ref, qn_ref, kvn_ref,
                     wuq_ref, wuqr_ref, wukt_ref, wuv_ref, wg2_ref, bg_ref,
                     x1_ref, q_ref, kt_ref, v_ref, gq_ref, gk_ref, la_ref, gv_ref, gr_ref):
    groups = _row_groups(x_ref.shape[0])
    x1s = _ffn_groups([x_ref[g, :] for g in groups], n1_ref[...], wgu_ref[...], wdn_ref[...])
    for g, x1 in zip(groups, x1s):
        x1_ref[g, :] = x1
    hs = [_rms(x1, nm_ref[...]).astype(_BF) for x1 in x1s]
    projs = [_dot(h, win_ref[...]) for h in hs]
    cqs = [_rms(proj[:, 0:256], qn_ref[...]).astype(_BF) for proj in projs]
    ckvs = [_rms(proj[:, 256:384], kvn_ref[...]).astype(_BF) for proj in projs]
    qs = [_dot(cq, wuq_ref[...]) for cq in cqs]
    q_rots = [_dot(cq, wuqr_ref[...]) for cq in cqs]
    k_nope_ts = [_dot_nt(wukt_ref[...], ckv) for ckv in ckvs]
    vs = [_dot(ckv, wuv_ref[...]) for ckv in ckvs]
    gates = [_dot(proj[:, 1664:1792].astype(_BF), wg2_ref[...]) + bg_ref[...] for proj in projs]
    scale = (MLA_NOPE + MLA_ROPE) ** -0.5 * LOG2_E
    for g, proj, q, q_rot, k_nope_t, v, gate in zip(groups, projs, qs, q_rots, k_nope_ts, vs, gates):
        cos = cos_ref[g, :]
        sin = sin_ref[g, :]
        k_r = proj[:, 384:512]
        k_rr = proj[:, 512:640]
        for hd in range(MLA_HEADS):
            lo = hd * MLA_QK_PAD
            q_ref[g, lo:lo + LANES] = (q[:, lo:lo + LANES] * scale).astype(_BF)
            q_rope = q[:, lo + LANES:lo + 2 * LANES] * cos + q_rot[:, hd * LANES:(hd + 1) * LANES] * sin
            q_ref[g, lo + LANES:lo + 2 * LANES] = (q_rope * scale).astype(_BF)
        k_rope_t = (k_r * cos + k_rr * sin).T
        pad_row = lax.broadcasted_iota(jnp.int32, k_rope_t.shape, 0) == MLA_ROPE
        k_rope_t = jnp.where(pad_row, 1.0, k_rope_t).astype(_BF)
        for hd in range(MLA_HEADS):
            lo = hd * MLA_QK_PAD
            kt_ref[0, lo:lo + LANES, g] = k_nope_t[hd * LANES:(hd + 1) * LANES].astype(_BF)
            kt_ref[0, lo + LANES:lo + 2 * LANES, g] = k_rope_t
        v_ref[g, :] = v.astype(_BF)
        gq_ref[g, :] = proj[:, 640:896] * (GLA_DK ** -0.5)
        gk_ref[g, :] = proj[:, 896:1152]
        la_ref[g, :] = _log_sigmoid(gate) * (1.0 / GLA_GATE_TAU)
        gv_ref[g, :] = proj[:, 1152:1664].astype(_BF)
        gr_ref[g, :] = proj[:, 1792:2304].astype(_BF)


def _pre_odd_kernel(x_ref, n1_ref, wgu_ref, wdn_ref, nm_ref, wq_ref, wk_ref, wv_ref,
                    x1_ref, q_ref, kt_ref, v_ref):
    groups = _row_groups(x_ref.shape[0])
    x1s = _ffn_groups([x_ref[g, :] for g in groups], n1_ref[...], wgu_ref[...], wdn_ref[...])
    for g, x1 in zip(groups, x1s):
        x1_ref[g, :] = x1
    hs = [_rms(x1, nm_ref[...]).astype(_BF) for x1 in x1s]
    qs = [_dot(h, wq_ref[...]) for h in hs]
    ks = [_dot(h, wk_ref[...]) for h in hs]
    vs = [_dot(h, wv_ref[...]) for h in hs]
    for j, (g, q, k, v) in enumerate(zip(groups, qs, ks, vs)):
        q_ref[g, :] = (q * (SB_HEAD_DIM ** -0.5)).astype(_BF)
        kt_ref[j] = k.T.astype(_BF)
        v_ref[g, :] = v.astype(_BF)


def _row_spec(width, tm):
    return pl.BlockSpec((tm, width), lambda i: (i, 0))


def _pre_even(x, cos_t, sin_t, w):
    t = x.shape[0]
    tm = min(TOKEN_TILE, t)
    consts = [w["n1"], w["wgu"], w["wdn"], w["nm"], w["win"], w["qn"], w["kvn"], w["wuq"], w["wuqr"],
              w["wukt"], w["wuv"], w["wg2"], w["bg"]]
    assert tm == MLA_TK
    kt_rows = MLA_HEADS * MLA_QK_PAD
    out_widths = [(D_MODEL, _F32), (MLA_HEADS * MLA_QK_PAD, _BF), None,
                  (MLA_HEADS * MLA_V, _BF), (GLA_QK, _F32), (GLA_QK, _F32), (GLA_QK, _F32),
                  (GLA_V, _BF), (GLA_V, _BF)]
    out_specs = [pl.BlockSpec((1, kt_rows, tm), lambda i: (i, 0, 0)) if o is None else _row_spec(o[0], tm)
                 for o in out_widths]
    out_shape = [jax.ShapeDtypeStruct((t // tm, kt_rows, tm), _BF) if o is None
                 else jax.ShapeDtypeStruct((t, o[0]), o[1]) for o in out_widths]
    return pl.pallas_call(
        _pre_even_kernel,
        grid=(t // tm,),
        in_specs=[_row_spec(D_MODEL, tm), _row_spec(LANES, tm), _row_spec(LANES, tm)]
        + [_const_spec(c.shape) for c in consts],
        out_specs=out_specs,
        out_shape=out_shape,
        compiler_params=_params(("parallel",)),
        name="pre_even",
    )(x, cos_t, sin_t, *consts)


def _pre_odd(x, w):
    t = x.shape[0]
    tm = min(TOKEN_TILE, t)
    consts = [w["n1"], w["wgu"], w["wdn"], w["nm"], w["wq"], w["wk"], w["wv"]]
    assert tm == TOKEN_ROW_GROUPS * SB_TK
    out_widths = [(D_MODEL, _F32), (D_MODEL, _BF), None, (D_MODEL, _BF)]
    out_specs = [pl.BlockSpec((tm // SB_TK, D_MODEL, SB_TK), lambda i: (i, 0, 0)) if o is None
                 else _row_spec(o[0], tm) for o in out_widths]
    out_shape = [jax.ShapeDtypeStruct((t // SB_TK, D_MODEL, SB_TK), _BF) if o is None
                 else jax.ShapeDtypeStruct((t, o[0]), o[1]) for o in out_widths]
    return pl.pallas_call(
        _pre_odd_kernel,
        grid=(t // tm,),
        in_specs=[_row_spec(D_MODEL, tm)] + [_const_spec(c.shape) for c in consts],
        out_specs=out_specs,
        out_shape=out_shape,
        compiler_params=_params(("parallel",)),
        name="pre_odd",
    )(x, *consts)


def _post_kernel(n_mix, final, *refs):
    x_ref = refs[0]
    mix_refs = refs[1:1 + n_mix]
    p_ref = refs[1 + n_mix]
    wout_refs = refs[2 + n_mix:2 + 2 * n_mix]
    n2_ref, wgu_ref, wdn_ref, np_ref, wpg_ref, wpp_ref, nf_ref, out_ref = refs[2 + 2 * n_mix:]
    groups = _row_groups(x_ref.shape[0])
    xs = [x_ref[g, :] for g in groups]
    for m_ref, w_ref in zip(mix_refs, wout_refs):
        ds = [_dot(m_ref[g, :], w_ref[...]) for g in groups]
        xs = [x + d for x, d in zip(xs, ds)]
    xs = _ffn_groups(xs, n2_ref[...], wgu_ref[...], wdn_ref[...])
    hs = [_rms(x, np_ref[...]).astype(_BF) for x in xs]
    gates = [_sigmoid(_dot(h, wpg_ref[...])) for h in hs]
    projs = [_dot(p_ref[g, :].astype(_BF), wpp_ref[...]) for g in groups]
    xs = [x + gate * proj for x, gate, proj in zip(xs, gates, projs)]
    if final:
        xs = [_rms(x, nf_ref[...]) for x in xs]
    for g, x in zip(groups, xs):
        out_ref[g, :] = x


def _post(x, mixes, p, layer, wouts, w, final_norm, final):
    t = x.shape[0]
    tm = min(TOKEN_TILE, t)
    consts = list(wouts) + [w["n2"], w["wgu2"], w["wdn2"], w["np"], w["wpg"], w["wpp"], final_norm]
    return pl.pallas_call(
        functools.partial(_post_kernel, len(mixes), final),
        grid=(t // tm,),
        in_specs=[_row_spec(D_MODEL, tm)] + [_row_spec(m.shape[1], tm) for m in mixes]
        + [pl.BlockSpec((None, tm, P_DIM), lambda i: (layer, i, 0))] + [_const_spec(c.shape) for c in consts],
        out_specs=_row_spec(D_MODEL, tm),
        out_shape=jax.ShapeDtypeStruct((t, D_MODEL), _F32),
        compiler_params=_params(("parallel",)),
        name="post_final" if final else "post",
    )(x, *mixes, p, *consts)


def _mla_kernel(q_ref, kt_ref, v_ref, o_ref, m_ref, l_ref, acc_ref, qs_ref, kmax_ref):
    qi = pl.program_id(2)
    heads = range(MLA_HEADS_PER_STEP)
    slot = lambda hd: slice(hd * MLA_QK_PAD, (hd + 1) * MLA_QK_PAD)
    vslot = lambda hd: slice(hd * MLA_V, (hd + 1) * MLA_V)
    reps = MLA_TK // LANES

    @pl.when(qi == 0)
    def _():
        def key_norm(i, mx):
            kc = kt_ref[i].astype(_F32)
            kc = kc * kc
            ssq = [jnp.sum(kc[hd * MLA_QK_PAD:hd * MLA_QK_PAD + MLA_QK_REAL], axis=0, keepdims=True)
                   for hd in heads]
            return tuple(jnp.maximum(m, s) for m, s in zip(mx, ssq))
        mx = lax.fori_loop(0, kt_ref.shape[0], key_norm, tuple(jnp.zeros((1, MLA_TK), _F32) for _ in heads))
        for hd in heads:
            kmax_ref[hd] = jnp.broadcast_to(jnp.max(mx[hd], axis=1, keepdims=True), kmax_ref.shape[1:])

    bounds = []
    for hd in heads:
        q_sq = jnp.max(_row_sq_norm_bound(q_ref[0, :, slot(hd)]), axis=0, keepdims=True)
        bounds.append(jnp.sqrt(q_sq * kmax_ref[hd, 0:1, :]) * MLA_BOUND_SLACK)
    worst = bounds[0]
    for bnd in bounds[1:]:
        worst = jnp.maximum(worst, bnd)
    fast = jnp.max(worst) <= MLA_SAFE_SCORE_BOUND

    l_ref[...] = jnp.zeros(l_ref.shape, _F32)
    acc_ref[...] = jnp.zeros(acc_ref.shape, _F32)
    n_full = (qi * MLA_TQ) // MLA_TK

    def causal_mask(s, start):
        row = lax.broadcasted_iota(jnp.int32, s.shape, 0) + qi * MLA_TQ
        col = lax.broadcasted_iota(jnp.int32, s.shape, 1) + start
        return jnp.where(col <= row, s, MASK_VALUE)

    def run(step):
        def body(c, carry):
            step(c, False)
            return carry
        lax.fori_loop(0, n_full, body, 0)
        step(n_full, True)

    @pl.when(fast)
    def _():
        for hd in heads:
            lane = lax.broadcasted_iota(jnp.int32, (MLA_TQ, MLA_QK_PAD), 1)
            shift = jnp.broadcast_to(jnp.concatenate([-bounds[hd]] * (MLA_QK_PAD // LANES), axis=1),
                                     (MLA_TQ, MLA_QK_PAD)).astype(_BF)
            qs_ref[hd] = jnp.where(lane == MLA_QK_REAL, shift, q_ref[0, :, slot(hd)])

        def step(c, masked):
            start = pl.multiple_of(c * MLA_TK, MLA_TK)
            for hd in heads:
                s = _dot(qs_ref[hd], kt_ref[c, slot(hd), :])
                if masked:
                    s = causal_mask(s, start)
                p = jnp.exp2(s)
                part = p[:, :LANES]
                for r in range(1, reps):
                    part = part + p[:, r * LANES:(r + 1) * LANES]
                l_ref[hd] += part
                acc_ref[hd] += _dot(p.astype(_BF), v_ref[0, pl.ds(start, MLA_TK), vslot(hd)])

        run(step)
        for hd in heads:
            denom = jnp.sum(l_ref[hd], axis=1, keepdims=True)
            o_ref[0, :, vslot(hd)] = (acc_ref[hd] / denom).astype(o_ref.dtype)

    @pl.when(jnp.logical_not(fast))
    def _():
        m_ref[...] = jnp.full(m_ref.shape, MASK_VALUE, _F32)

        def step(c, masked):
            start = pl.multiple_of(c * MLA_TK, MLA_TK)
            for hd in heads:
                s = _dot(q_ref[0, :, slot(hd)], kt_ref[c, slot(hd), :])
                if masked:
                    s = causal_mask(s, start)
                m_prev = m_ref[hd]
                m_next = jnp.maximum(m_prev, jnp.max(s, axis=1, keepdims=True))
                p = jnp.exp2(s - jnp.concatenate([m_next] * reps, axis=1))
                alpha = jnp.exp2(m_prev - m_next)
                l_ref[hd] = alpha * l_ref[hd] + jnp.sum(p, axis=1, keepdims=True)
                acc_ref[hd] = alpha * acc_ref[hd] + _dot(p.astype(_BF), v_ref[0, pl.ds(start, MLA_TK), vslot(hd)])
                m_ref[hd] = m_next

        run(step)
        for hd in heads:
            o_ref[0, :, vslot(hd)] = (acc_ref[hd] / l_ref[hd]).astype(o_ref.dtype)


def _mla(q, kt, v):
    b, s, _ = q.shape
    hb = MLA_HEADS_PER_STEP
    n_chunks = s // MLA_TK
    assert MLA_TK % MLA_TQ == 0 and s % MLA_TK == 0 and MLA_HEADS % hb == 0
    resident = dict(pipeline_mode=pl.Buffered(1))
    return pl.pallas_call(
        _mla_kernel,
        grid=(b, MLA_HEADS // hb, s // MLA_TQ),
        in_specs=[pl.BlockSpec((1, MLA_TQ, hb * MLA_QK_PAD), lambda bi, hi, qi: (bi, qi, hi)),
                  pl.BlockSpec((n_chunks, hb * MLA_QK_PAD, MLA_TK), lambda bi, hi, qi: (bi, hi, 0), **resident),
                  pl.BlockSpec((1, s, hb * MLA_V), lambda bi, hi, qi: (bi, 0, hi), **resident)],
        out_specs=pl.BlockSpec((1, MLA_TQ, hb * MLA_V), lambda bi, hi, qi: (bi, qi, hi)),
        out_shape=jax.ShapeDtypeStruct((b, s, MLA_HEADS * MLA_V), _BF),
        scratch_shapes=[pltpu.VMEM((hb, MLA_TQ, LANES), _F32)] * 3
        + [pltpu.VMEM((hb, MLA_TQ, MLA_QK_PAD), _BF), pltpu.VMEM((hb, 8, LANES), _F32)],
        compiler_params=_params(("parallel", "parallel", "arbitrary")),
        name="mla_attention",
    )(q, kt, v)


def _sb_kernel(q_ref, kt_ref, v_ref, tri_ref, o_ref, suf_ref, acc_ref, zb_ref, kmax_ref):
    qi = pl.program_id(2)
    n_key_chunks = kt_ref.shape[0]
    heads = list(range(SB_HEADS_PER_STEP))
    lanes_of = lambda hd: slice(hd * SB_HEAD_DIM, (hd + 1) * SB_HEAD_DIM)

    @pl.when(qi == 0)
    def _():
        def key_norm(i, mx):
            kc = kt_ref[i].astype(_F32)
            kc = kc * kc
            ssq = [jnp.sum(kc[lanes_of(hd)], axis=0, keepdims=True) for hd in heads]
            return tuple(jnp.maximum(m, s) for m, s in zip(mx, ssq))
        mx = lax.fori_loop(0, n_key_chunks, key_norm, tuple(jnp.zeros((1, SB_TK), _F32) for _ in heads))
        for hd in heads:
            kmax_ref[hd] = jnp.broadcast_to(jnp.max(mx[hd], axis=1, keepdims=True), kmax_ref.shape[1:])

    tri = tri_ref[...]

    def chunk_group(items, sufs):
        halves = list(reversed(range(SB_TK // LANES)))
        zs, ts, masks = [], [], []
        for hd, c, masked in items:
            z = _dot(q_ref[0, :, lanes_of(hd)], kt_ref[c, lanes_of(hd), :])
            zs.append(z)
        for z, (hd, c, masked) in zip(zs, items):
            t = jnp.maximum(z, 0.0) + jnp.log(1.0 + jnp.exp(-jnp.abs(z)))
            causal = None
            if masked:
                causal = (lax.broadcasted_iota(jnp.int32, z.shape, 1)
                          < lax.broadcasted_iota(jnp.int32, z.shape, 0))
                t = jnp.where(causal, t, 0.0)
            ts.append(t)
            masks.append(causal)
        sums = []
        for t in ts:
            per_half = {}
            for half in halves:
                th = t[:, half * LANES:(half + 1) * LANES]
                t_hi = th.astype(_BF)
                t_lo = (th - t_hi.astype(_F32)).astype(_BF)
                per_half[half] = _dot(jnp.concatenate([t_hi, t_lo], axis=1), tri)
            sums.append(per_half)
        outs = []
        for z, per_half, causal, (hd, c, masked) in zip(zs, sums, masks, items):
            suf = sufs[hd]
            log_w = {}
            for half in halves:
                log_w[half] = z[:, half * LANES:(half + 1) * LANES] - per_half[half][:, :LANES] - suf
                suf = suf + per_half[half][:, LANES:]
            sufs[hd] = suf
            a = jnp.exp(jnp.concatenate([log_w[h] for h in sorted(halves)], axis=1))
            if masked:
                a = jnp.where(causal, a, 0.0)
            start = pl.multiple_of(c * SB_TK, SB_TK)
            outs.append(_dot(a.astype(_BF), v_ref[0, pl.ds(start, SB_TK), lanes_of(hd)]))
        return outs

    def still_live(sufs):
        slack = zb_ref[0, 0:1, :] - sufs[0]
        for hd in heads[1:]:
            slack = jnp.maximum(slack, zb_ref[hd, 0:1, :] - sufs[hd])
        return (jnp.max(slack) >= SB_EXIT_LOG_WEIGHT).astype(jnp.int32)

    zero_suf = lambda: {hd: jnp.zeros((SB_TQ, LANES), _F32) for hd in heads}

    @pl.when(qi == 0)
    def _():
        outs = chunk_group([(hd, 0, True) for hd in heads], zero_suf())
        for hd in heads:
            o_ref[0, :, lanes_of(hd)] = outs[hd].astype(o_ref.dtype)

    @pl.when(qi > 0)
    def _():
        for hd in heads:
            q_sq = jnp.max(_row_sq_norm_bound(q_ref[0, :, lanes_of(hd)]), axis=0, keepdims=True)
            zb_ref[hd] = jnp.broadcast_to(jnp.sqrt(q_sq * kmax_ref[hd, 0:1, :]) * SB_BOUND_SLACK,
                                          zb_ref.shape[1:])
        sufs = zero_suf()
        items = [(hd, qi, True) for hd in heads] + [(hd, qi - 1, False) for hd in heads]
        outs = chunk_group(items, sufs)
        for hd in heads:
            acc_ref[hd] = outs[hd] + outs[len(heads) + hd]
            suf_ref[hd] = sufs[hd]

        def cond(carry):
            c, live = carry
            return jnp.logical_and(c >= 0, live > 0)

        def body(carry):
            c, _ = carry
            sufs = {hd: suf_ref[hd] for hd in heads}
            outs = chunk_group([(hd, c, False) for hd in heads], sufs)
            for hd in heads:
                acc_ref[hd] += outs[hd]
                suf_ref[hd] = sufs[hd]
            return c - 1, still_live(sufs)

        lax.while_loop(cond, body, (qi - 2, still_live(sufs)))
        for hd in heads:
            o_ref[0, :, lanes_of(hd)] = acc_ref[hd].astype(o_ref.dtype)


def _sb_tri():
    r = np.arange(LANES)
    upper = (r[:, None] >= r[None, :]).astype(np.float32)
    half = np.concatenate([upper, np.ones((LANES, LANES), np.float32)], axis=1)
    return jnp.asarray(np.concatenate([half, half], axis=0), dtype=_BF)


def _sb(q, kt, v):
    b, s, _ = q.shape
    hb = SB_HEADS_PER_STEP
    assert SB_TQ == SB_TK and SB_TK % LANES == 0 and s % SB_TQ == 0 and SB_HEADS % hb == 0
    tri = _sb_tri()
    width = hb * SB_HEAD_DIM
    return pl.pallas_call(
        _sb_kernel,
        grid=(b, SB_HEADS // hb, s // SB_TQ),
        in_specs=[pl.BlockSpec((1, SB_TQ, width), lambda bi, hi, qi: (bi, qi, hi)),
                  pl.BlockSpec((s // SB_TK, width, SB_TK), lambda bi, hi, qi: (bi, hi, 0)),
                  pl.BlockSpec((1, s, width), lambda bi, hi, qi: (bi, 0, hi)),
                  _const_spec(tri.shape)],
        out_specs=pl.BlockSpec((1, SB_TQ, width), lambda bi, hi, qi: (bi, qi, hi)),
        out_shape=jax.ShapeDtypeStruct((b, s, D_MODEL), _BF),
        scratch_shapes=[pltpu.VMEM((hb, SB_TQ, LANES), _F32)] * 2 + [pltpu.VMEM((hb, 8, LANES), _F32)] * 2,
        compiler_params=_params(("parallel", "parallel", "arbitrary")),
        name="sb_attention",
    )(q, kt, v, tri)


def _gla_constants():
    c = GLA_CHUNK
    idx = np.arange(c)
    seg = []
    masks = []
    for lvl in range(GLA_LEVELS):
        s = 1 << lvl
        blk = idx // s
        start = blk * s
        end = start + s - 1
        t = idx[None, :]
        q_side = ((blk % 2 == 1)[:, None] & (t >= start[:, None]) & (t <= idx[:, None]))
        k_side = ((blk % 2 == 0)[:, None] & (t > idx[:, None]) & (t <= end[:, None]))
        seg.append(np.where((blk % 2 == 1)[:, None], q_side, k_side))
        pair = ((idx[:, None] // (2 * s)) == (idx[None, :] // (2 * s))) \
            & ((blk % 2 == 1)[:, None]) & ((blk % 2 == 0)[None, :])
        masks.append(np.tile(pair, (GLA_HEADS, 1)))
    t = idx[None, :]
    full = np.concatenate([t <= idx[:, None], t > idx[:, None]], axis=0)
    seg = np.stack(seg).astype(np.float32)
    masks = np.stack(masks).astype(np.float32)
    head_of_qk = np.arange(GLA_QK) // GLA_DK
    head_of_v = np.arange(GLA_V) // GLA_DV
    head_mask = (np.arange(8)[:, None] == head_of_qk[None, :]).astype(np.float32)
    expand = (head_of_qk[:, None] == head_of_v[None, :]).astype(np.float32)
    return (jnp.asarray(seg, _BF), jnp.asarray(full, _BF), jnp.asarray(masks, _F32), jnp.asarray(head_mask, _F32),
            jnp.asarray(expand, _BF), jnp.asarray(expand.T, _F32))


def _gla_kernel(gq_ref, gk_ref, la_ref, gv_ref, gr_ref, seg_ref, full_ref, mask_ref, hm_ref, exp_ref, bd_ref,
                gn_ref, o_ref, state_ref):
    c = GLA_CHUNK

    @pl.when(pl.program_id(1) == 0)
    def _():
        state_ref[...] = jnp.zeros(state_ref.shape, _F32)

    hm = hm_ref[...]
    gn = gn_ref[...]

    def intra_chunks(row_slices):
        qs_, ks_, vs_, la2s = [], [], [], []
        for rows in row_slices:
            la = la_ref[0, rows, :]
            la_hi = la.astype(_BF)
            la_lo = (la - la_hi.astype(_F32)).astype(_BF)
            la2s.append(jnp.concatenate([la_hi, la_lo], axis=1))
            qs_.append(gq_ref[0, rows, :])
            ks_.append(gk_ref[0, rows, :])
            vs_.append(gv_ref[0, rows, :])
        segs = [seg_ref[lvl] for lvl in range(GLA_LEVELS)] + [full_ref[...]]
        sums = [[_dot(seg, la2) for seg in segs] for la2 in la2s]
        ws = [[jnp.exp(e[:, :GLA_QK] + e[:, GLA_QK:]) for e in per_chunk] for per_chunk in sums]
        pairs = []
        for q, k, w_chunk in zip(qs_, ks_, ws):
            per_level = []
            for lvl in range(GLA_LEVELS):
                w = w_chunk[lvl]
                ql = q * w
                kl = (k * w).astype(_BF)
                q_heads = jnp.concatenate([ql * hm[hd:hd + 1] for hd in range(GLA_HEADS)], axis=0).astype(_BF)
                per_level.append((q_heads, kl))
            pairs.append(per_level)
        scores = [[_dot_nt(q_heads, kl) for q_heads, kl in per_level] for per_level in pairs]
        out = []
        for q, k, v, w_chunk, per_level in zip(qs_, ks_, vs_, ws, scores):
            att = per_level[0] * mask_ref[0]
            for lvl in range(1, GLA_LEVELS):
                att = att + per_level[lvl] * mask_ref[lvl]
            w = w_chunk[GLA_LEVELS]
            o = _dot((q * k).astype(_BF), exp_ref[...]) * v.astype(_F32)
            o_intra = [_dot(att[hd * c:(hd + 1) * c, :].astype(_BF), v[:, hd * GLA_DV:(hd + 1) * GLA_DV])
                       for hd in range(GLA_HEADS)]
            o = o + jnp.concatenate(o_intra, axis=1)
            out.append((o, (q * w[:c]).astype(_BF), (k * w[c:]).astype(_BF), w[c - 1:c], v))
        return out

    chunks = intra_chunks([slice(i * c, (i + 1) * c) for i in range(GLA_CHUNKS_PER_STEP)])
    state = state_ref[...]
    for i, (o, q_state, k_state, decay, v) in enumerate(chunks):
        o = o + _dot_nt(q_state, state.astype(_BF))
        state = state * decay + _dot_tn(v, k_state) * bd_ref[...]
        gr = gr_ref[0, i * c:(i + 1) * c, :].astype(_F32)
        for hd in range(GLA_HEADS):
            sl = slice(hd * GLA_DV, (hd + 1) * GLA_DV)
            o_ref[0, i * c:(i + 1) * c, sl] = (_rms(o[:, sl], gn[:, sl]) * _silu(gr[:, sl])).astype(o_ref.dtype)
    state_ref[...] = state


def _gla(gq, gk, la, gv, gr, gla_norm):
    b, s, _ = gq.shape
    rows = GLA_CHUNK * GLA_CHUNKS_PER_STEP
    assert s % rows == 0 and (1 << GLA_LEVELS) == GLA_CHUNK
    consts = list(_gla_constants()) + [gla_norm]
    tok = lambda wd: pl.BlockSpec((1, rows, wd), lambda bi, ci: (bi, ci, 0))
    return pl.pallas_call(
        _gla_kernel,
        grid=(b, s // rows),
        in_specs=[tok(GLA_QK), tok(GLA_QK), tok(GLA_QK), tok(GLA_V), tok(GLA_V)]
        + [_const_spec(cn.shape) for cn in consts],
        out_specs=tok(GLA_V),
        out_shape=jax.ShapeDtypeStruct((b, s, GLA_V), _BF),
        scratch_shapes=[pltpu.VMEM((GLA_V, GLA_QK), _F32)],
        compiler_params=_params(("parallel", "arbitrary")),
        name="gla_chunked",
    )(gq, gk, la, gv, gr, *consts)


def _rotate_half_cols(w):
    half = w.shape[1] // 2
    return jnp.concatenate([-w[:, half:], w[:, :half]], axis=1)


def _pad_cols(w, width):
    return jnp.pad(w, ((0, 0), (0, width - w.shape[1])))


def _prep_even_weights(w_in, w_uq, w_ukv, w_gate2):
    splits = np.cumsum([MLA_Q_RANK, MLA_KV_RANK, MLA_ROPE, GLA_QK, GLA_QK, GLA_V, GLA_GATE_RANK])
    c_q, c_kv, k_r, g_q, g_k, g_v, g_a, g_r = jnp.split(w_in, splits, axis=1)
    win = jnp.concatenate([c_q, c_kv, _pad_cols(k_r, LANES), _pad_cols(_rotate_half_cols(k_r), LANES),
                           g_q, g_k, g_v, _pad_cols(g_a, LANES), g_r], axis=1).astype(_BF)
    uq = w_uq.reshape(MLA_Q_RANK, MLA_HEADS, MLA_NOPE + MLA_ROPE)
    uq_rope = uq[:, :, MLA_NOPE:]
    rot = jnp.concatenate([-uq_rope[:, :, MLA_ROPE // 2:], uq_rope[:, :, :MLA_ROPE // 2]], axis=2)
    wuq = jnp.pad(uq, ((0, 0), (0, 0), (0, MLA_QK_PAD - uq.shape[2]))).reshape(MLA_Q_RANK, -1).astype(_BF)
    wuqr = jnp.pad(rot, ((0, 0), (0, 0), (0, LANES - MLA_ROPE))).reshape(MLA_Q_RANK, -1).astype(_BF)
    ukv = w_ukv.reshape(MLA_KV_RANK, MLA_HEADS, MLA_NOPE + MLA_V)
    wukt = ukv[:, :, :MLA_NOPE].reshape(MLA_KV_RANK, -1).T.astype(_BF)
    wuv = ukv[:, :, MLA_NOPE:].reshape(MLA_KV_RANK, -1).astype(_BF)
    wg2 = jnp.pad(w_gate2, ((0, LANES - GLA_GATE_RANK), (0, 0))).astype(_BF)
    return dict(win=win, wuq=wuq, wuqr=wuqr, wukt=wukt, wuv=wuv, wg2=wg2)


def _rope_tables(positions):
    half = MLA_ROPE // 2
    inv_freq = 1.0 / (ROPE_THETA ** (jnp.arange(half, dtype=_F32) / half))
    ang = positions.astype(_F32)[..., None] * inv_freq
    cos, sin = jnp.cos(ang), jnp.sin(ang)
    pad = jnp.zeros(cos.shape[:-1] + (LANES - MLA_ROPE,), _F32)
    return (jnp.concatenate([cos, cos, pad], axis=-1).reshape(-1, LANES),
            jnp.concatenate([sin, sin, pad], axis=-1).reshape(-1, LANES))


def kernel(x, p, positions, ffn1_norm, ffn1_w_gu, ffn1_w_down, mix_norm, ffn2_norm, ffn2_w_gu, ffn2_w_down, ple_norm, ple_w_gate, ple_w_proj, ev_w_in, ev_q_norm, ev_kv_norm, ev_w_uq, ev_w_ukv, ev_w_gate2, ev_b_gate, ev_gla_norm, ev_w_out, od_w_qkv, od_w_out, final_norm):
    b, s, d = x.shape
    depth = p.shape[0]
    t = b * s
    cos_t, sin_t = _rope_tables(positions)
    row = lambda a: a.reshape(1, -1).astype(_F32)
    xt = x.reshape(t, d)
    p3 = p.reshape(depth, t, p.shape[-1])
    fin = row(final_norm)
    for i in range(depth):
        j = i // 2
        common = dict(n1=row(ffn1_norm[i]), wgu=ffn1_w_gu[i].astype(_BF), wdn=ffn1_w_down[i].astype(_BF),
                      nm=row(mix_norm[i]), n2=row(ffn2_norm[i]), wgu2=ffn2_w_gu[i].astype(_BF),
                      wdn2=ffn2_w_down[i].astype(_BF), np=row(ple_norm[i]), wpg=ple_w_gate[i].astype(_BF),
                      wpp=ple_w_proj[i].astype(_BF))
        if i % 2 == 0:
            w = dict(common, **_prep_even_weights(ev_w_in[j], ev_w_uq[j], ev_w_ukv[j], ev_w_gate2[j]),
                     qn=row(ev_q_norm[j]), kvn=row(ev_kv_norm[j]), bg=row(ev_b_gate[j]))
            x1, q, kt, v, gq, gk, la, gv, gr = _pre_even(xt, cos_t, sin_t, w)
            seq = lambda a: a.reshape(b, s, a.shape[-1])
            o_mla = _mla(seq(q), kt, seq(v))
            o_gla = _gla(seq(gq), seq(gk), seq(la), seq(gv), seq(gr), row(ev_gla_norm[j]))
            mixes = [o_mla.reshape(t, -1), o_gla.reshape(t, -1)]
            w_out = ev_w_out[j].astype(_BF)
            wouts = [w_out[:MLA_HEADS * MLA_V], w_out[MLA_HEADS * MLA_V:]]
        else:
            wqkv = od_w_qkv[j].astype(_BF)
            w = dict(common, wq=wqkv[:, :D_MODEL], wk=wqkv[:, D_MODEL:2 * D_MODEL], wv=wqkv[:, 2 * D_MODEL:])
            x1, q, kt, v = _pre_odd(xt, w)
            seq = lambda a: a.reshape(b, s, a.shape[-1])
            o_sb = _sb(seq(q), kt, seq(v))
            mixes = [o_sb.reshape(t, -1)]
            wouts = [od_w_out[j].astype(_BF)]
        xt = _post(x1, mixes, p3, i, wouts, w, fin, i == depth - 1)
    return xt.reshape(b, s, d)
```

```python
import functools

import numpy as np
import jax
import jax.numpy as jnp
from jax import lax
from jax.experimental import pallas as pl
from jax.experimental.pallas import tpu as pltpu

D_MODEL = 1024
P_DIM = 256
EPS = 1e-6
D_FF = 1408
MLA_HEADS = 4
MLA_Q_RANK = 256
MLA_KV_RANK = 128
MLA_NOPE = 128
MLA_ROPE = 64
MLA_V = 128
ROPE_THETA = 10000.0
GLA_HEADS = 4
GLA_DK = 64
GLA_DV = 128
GLA_GATE_RANK = 16
GLA_GATE_TAU = 16.0
SB_HEADS = 8
SB_HEAD_DIM = D_MODEL // SB_HEADS

LANES = 128
MLA_QK_PAD = 2 * LANES
GLA_QK = GLA_HEADS * GLA_DK
GLA_V = GLA_HEADS * GLA_DV

TOKEN_TILE = 512
TOKEN_ROW_GROUPS = 2
MLA_TQ = 512
MLA_TK = 512
MLA_HEADS_PER_STEP = 4
SB_TQ = 256
SB_TK = 256
SB_HEADS_PER_STEP = 4
SB_EXIT_LOG_WEIGHT = -106.0
SB_BOUND_SLACK = 1.001
GLA_CHUNK = 128
GLA_CHUNKS_PER_STEP = 4
GLA_LEVELS = 7
VMEM_LIMIT = 56 * 1024 * 1024
MASK_VALUE = -1e30
MLA_SAFE_SCORE_BOUND = 60.0
MLA_BOUND_SLACK = 1.001
MLA_QK_REAL = 192
LOG2_E = 1.4426950408889634

_BF = jnp.bfloat16
_F32 = jnp.float32


def _dot(a, b):
    return jnp.dot(a, b, preferred_element_type=_F32)


def _dot_nt(a, b):
    return lax.dot_general(a, b, (((1,), (1,)), ((), ())), preferred_element_type=_F32)


def _dot_tn(a, b):
    return lax.dot_general(a, b, (((0,), (0,)), ((), ())), preferred_element_type=_F32)


def _rms(x, w):
    return x * lax.rsqrt(jnp.mean(x * x, axis=-1, keepdims=True) + EPS) * w


def _sigmoid(x):
    return 1.0 / (1.0 + jnp.exp(-x))


def _silu(x):
    return x * _sigmoid(x)


def _log_sigmoid(x):
    return jnp.minimum(x, 0.0) - jnp.log(1.0 + jnp.exp(-jnp.abs(x)))


def _row_sq_norm_bound(x):
    return _dot(x * x, jnp.ones((x.shape[1], LANES), x.dtype)) * (1.0 + 2.0 ** -8)


def _row_groups(rows):
    step = rows // TOKEN_ROW_GROUPS
    return [slice(i * step, (i + 1) * step) for i in range(TOKEN_ROW_GROUPS)]


def _ffn_groups(xs, norm_w, w_gu, w_down):
    hs = [_rms(x, norm_w).astype(_BF) for x in xs]
    gus = [_dot(h, w_gu) for h in hs]
    acts = [(_silu(gu[:, :D_FF]) * gu[:, D_FF:]).astype(_BF) for gu in gus]
    downs = [_dot(act, w_down) for act in acts]
    return [x + 0.5 * d for x, d in zip(xs, downs)]


def _const_spec(shape):
    nd = len(shape)
    return pl.BlockSpec(shape, lambda *_: (0,) * nd, pipeline_mode=pl.Buffered(1))


def _params(semantics):
    return pltpu.CompilerParams(dimension_semantics=semantics, vmem_limit_bytes=VMEM_LIMIT)


def _pre_even_kernel(x_ref, cos_ref, sin_ref, n1_ref, wgu_ref, wdn_ref, nm_ref, win_ref, qn_ref, kvn_ref,
                     wuq_ref, wuqr_ref, wukt_ref, wuv_ref, wg2_ref, bg_ref,
                     x1_ref, q_ref, kt_ref, v_ref, qmax_ref, gq_ref, gk_ref, la_ref, gv_ref, gr_ref):
    groups = _row_groups(x_ref.shape[0])
    x1s = _ffn_groups([x_ref[g, :] for g in groups], n1_ref[...], wgu_ref[...], wdn_ref[...])
    for g, x1 in zip(groups, x1s):
        x1_ref[g, :] = x1
    hs = [_rms(x1, nm_ref[...]).astype(_BF) for x1 in x1s]
    projs = [_dot(h, win_ref[...]) for h in hs]
    cqs = [_rms(proj[:, 0:256], qn_ref[...]).astype(_BF) for proj in projs]
    ckvs = [_rms(proj[:, 256:384], kvn_ref[...]).astype(_BF) for proj in projs]
    qs = [_dot(cq, wuq_ref[...]) for cq in cqs]
    q_rots = [_dot(cq, wuqr_ref[...]) for cq in cqs]
    k_nope_ts = [_dot_nt(wukt_ref[...], ckv) for ckv in ckvs]
    vs = [_dot(ckv, wuv_ref[...]) for ckv in ckvs]
    gates = [_dot(proj[:, 1664:1792].astype(_BF), wg2_ref[...]) + bg_ref[...] for proj in projs]
    scale = (MLA_NOPE + MLA_ROPE) ** -0.5 * LOG2_E
    q_sq_max = jnp.zeros(qmax_ref.shape[1:], _F32)
    head_lane = lax.broadcasted_iota(jnp.int32, q_sq_max.shape, 1)
    for g, proj, q, q_rot, k_nope_t, v, gate in zip(groups, projs, qs, q_rots, k_nope_ts, vs, gates):
        cos = cos_ref[g, :]
        sin = sin_ref[g, :]
        k_r = proj[:, 384:512]
        k_rr = proj[:, 512:640]
        for hd in range(MLA_HEADS):
            lo = hd * MLA_QK_PAD
            q_nope = q[:, lo:lo + LANES] * scale
            q_rope = (q[:, lo + LANES:lo + 2 * LANES] * cos + q_rot[:, hd * LANES:(hd + 1) * LANES] * sin) * scale
            q_ref[g, lo:lo + LANES] = q_nope.astype(_BF)
            q_ref[g, lo + LANES:lo + 2 * LANES] = q_rope.astype(_BF)
            sq = jnp.sum(q_nope * q_nope + q_rope * q_rope, axis=1, keepdims=True) * (1.0 + 2.0 ** -8) ** 2
            q_sq_max = jnp.where(head_lane == hd, jnp.maximum(q_sq_max, jnp.max(sq, axis=0, keepdims=True)), q_sq_max)
        k_rope_t = (k_r * cos + k_rr * sin).T
        pad_row = lax.broadcasted_iota(jnp.int32, k_rope_t.shape, 0) == MLA_ROPE
        k_rope_t = jnp.where(pad_row, 1.0, k_rope_t).astype(_BF)
        for hd in range(MLA_HEADS):
            lo = hd * MLA_QK_PAD
            kt_ref[0, lo:lo + LANES, g] = k_nope_t[hd * LANES:(hd + 1) * LANES].astype(_BF)
            kt_ref[0, lo + LANES:lo + 2 * LANES, g] = k_rope_t
        v_ref[g, :] = v.astype(_BF)
        gq_ref[g, :] = proj[:, 640:896] * (GLA_DK ** -0.5)
        gk_ref[g, :] = proj[:, 896:1152]
        la_ref[g, :] = _log_sigmoid(gate) * (1.0 / GLA_GATE_TAU)
        gv_ref[g, :] = proj[:, 1152:1664].astype(_BF)
        gr_ref[g, :] = proj[:, 1792:2304].astype(_BF)
    qmax_ref[0] = q_sq_max


def _pre_odd_kernel(x_ref, n1_ref, wgu_ref, wdn_ref, nm_ref, wq_ref, wk_ref, wv_ref,
                    x1_ref, q_ref, kt_ref, v_ref):
    groups = _row_groups(x_ref.shape[0])
    x1s = _ffn_groups([x_ref[g, :] for g in groups], n1_ref[...], wgu_ref[...], wdn_ref[...])
    for g, x1 in zip(groups, x1s):
        x1_ref[g, :] = x1
    hs = [_rms(x1, nm_ref[...]).astype(_BF) for x1 in x1s]
    qs = [_dot(h, wq_ref[...]) for h in hs]
    ks = [_dot(h, wk_ref[...]) for h in hs]
    vs = [_dot(h, wv_ref[...]) for h in hs]
    for j, (g, q, k, v) in enumerate(zip(groups, qs, ks, vs)):
        q_ref[g, :] = (q * (SB_HEAD_DIM ** -0.5)).astype(_BF)
        kt_ref[j] = k.T.astype(_BF)
        v_ref[g, :] = v.astype(_BF)


def _row_spec(width, tm):
    return pl.BlockSpec((tm, width), lambda i: (i, 0))


def _pre_even(x, cos_t, sin_t, w):
    t = x.shape[0]
    tm = min(TOKEN_TILE, t)
    consts = [w["n1"], w["wgu"], w["wdn"], w["nm"], w["win"], w["qn"], w["kvn"], w["wuq"], w["wuqr"],
              w["wukt"], w["wuv"], w["wg2"], w["bg"]]
    assert tm == MLA_TK
    kt_rows = MLA_HEADS * MLA_QK_PAD
    tile_outs = {"kt": ((kt_rows, tm), _BF), "qmax": ((8, LANES), _F32)}
    out_widths = [(D_MODEL, _F32), (MLA_HEADS * MLA_QK_PAD, _BF), "kt",
                  (MLA_HEADS * MLA_V, _BF), "qmax", (GLA_QK, _F32), (GLA_QK, _F32), (GLA_QK, _F32),
                  (GLA_V, _BF), (GLA_V, _BF)]
    out_specs = [pl.BlockSpec((1,) + tile_outs[o][0], lambda i: (i, 0, 0)) if isinstance(o, str)
                 else _row_spec(o[0], tm) for o in out_widths]
    out_shape = [jax.ShapeDtypeStruct((t // tm,) + tile_outs[o][0], tile_outs[o][1]) if isinstance(o, str)
                 else jax.ShapeDtypeStruct((t, o[0]), o[1]) for o in out_widths]
    return pl.pallas_call(
        _pre_even_kernel,
        grid=(t // tm,),
        in_specs=[_row_spec(D_MODEL, tm), _row_spec(LANES, tm), _row_spec(LANES, tm)]
        + [_const_spec(c.shape) for c in consts],
        out_specs=out_specs,
        out_shape=out_shape,
        compiler_params=_params(("parallel",)),
        name="pre_even",
    )(x, cos_t, sin_t, *consts)


def _pre_odd(x, w):
    t = x.shape[0]
    tm = min(TOKEN_TILE, t)
    consts = [w["n1"], w["wgu"], w["wdn"], w["nm"], w["wq"], w["wk"], w["wv"]]
    assert tm == TOKEN_ROW_GROUPS * SB_TK
    out_widths = [(D_MODEL, _F32), (D_MODEL, _BF), None, (D_MODEL, _BF)]
    out_specs = [pl.BlockSpec((tm // SB_TK, D_MODEL, SB_TK), lambda i: (i, 0, 0)) if o is None
                 else _row_spec(o[0], tm) for o in out_widths]
    out_shape = [jax.ShapeDtypeStruct((t // SB_TK, D_MODEL, SB_TK), _BF) if o is None
                 else jax.ShapeDtypeStruct((t, o[0]), o[1]) for o in out_widths]
    return pl.pallas_call(
        _pre_odd_kernel,
        grid=(t // tm,),
        in_specs=[_row_spec(D_MODEL, tm)] + [_const_spec(c.shape) for c in consts],
        out_specs=out_specs,
        out_shape=out_shape,
        compiler_params=_params(("parallel",)),
        name="pre_odd",
    )(x, *consts)


def _post_kernel(n_mix, final, *refs):
    x_ref = refs[0]
    mix_refs = refs[1:1 + n_mix]
    p_ref = refs[1 + n_mix]
    wout_refs = refs[2 + n_mix:2 + 2 * n_mix]
    n2_ref, wgu_ref, wdn_ref, np_ref, wpg_ref, wpp_ref, nf_ref, out_ref = refs[2 + 2 * n_mix:]
    groups = _row_groups(x_ref.shape[0])
    xs = [x_ref[g, :] for g in groups]
    for m_ref, w_ref in zip(mix_refs, wout_refs):
        ds = [_dot(m_ref[g, :], w_ref[...]) for g in groups]
        xs = [x + d for x, d in zip(xs, ds)]
    xs = _ffn_groups(xs, n2_ref[...], wgu_ref[...], wdn_ref[...])
    hs = [_rms(x, np_ref[...]).astype(_BF) for x in xs]
    gates = [_sigmoid(_dot(h, wpg_ref[...])) for h in hs]
    projs = [_dot(p_ref[g, :].astype(_BF), wpp_ref[...]) for g in groups]
    xs = [x + gate * proj for x, gate, proj in zip(xs, gates, projs)]
    if final:
        xs = [_rms(x, nf_ref[...]) for x in xs]
    for g, x in zip(groups, xs):
        out_ref[g, :] = x


def _post(x, mixes, p, layer, wouts, w, final_norm, final):
    t = x.shape[0]
    tm = min(TOKEN_TILE, t)
    consts = list(wouts) + [w["n2"], w["wgu2"], w["wdn2"], w["np"], w["wpg"], w["wpp"], final_norm]
    return pl.pallas_call(
        functools.partial(_post_kernel, len(mixes), final),
        grid=(t // tm,),
        in_specs=[_row_spec(D_MODEL, tm)] + [_row_spec(m.shape[1], tm) for m in mixes]
        + [pl.BlockSpec((None, tm, P_DIM), lambda i: (layer, i, 0))] + [_const_spec(c.shape) for c in consts],
        out_specs=_row_spec(D_MODEL, tm),
        out_shape=jax.ShapeDtypeStruct((t, D_MODEL), _F32),
        compiler_params=_params(("parallel",)),
        name="post_final" if final else "post",
    )(x, *mixes, p, *consts)


def _mla_kernel(q_ref, kt_ref, v_ref, qmax_ref, o_ref, m_ref, l_ref, acc_ref, qs_ref, bound_ref, fast_ref):
    qi = pl.program_id(2)
    heads = range(MLA_HEADS_PER_STEP)
    slot = lambda hd: slice(hd * MLA_QK_PAD, (hd + 1) * MLA_QK_PAD)
    vslot = lambda hd: slice(hd * MLA_V, (hd + 1) * MLA_V)
    reps = MLA_TK // LANES

    @pl.when(qi == 0)
    def _():
        def key_norm(i, mx):
            kc = kt_ref[i].astype(_F32)
            kc = kc * kc
            ssq = [jnp.sum(kc[hd * MLA_QK_PAD:hd * MLA_QK_PAD + MLA_QK_REAL], axis=0, keepdims=True)
                   for hd in heads]
            return tuple(jnp.maximum(m, s) for m, s in zip(mx, ssq))
        mx = lax.fori_loop(0, kt_ref.shape[0], key_norm, tuple(jnp.zeros((1, MLA_TK), _F32) for _ in heads))
        q_max = jnp.max(qmax_ref[...], axis=0)[0:1, :]
        head_lane = lax.broadcasted_iota(jnp.int32, q_max.shape, 1) - pl.program_id(1) * MLA_HEADS_PER_STEP
        worst = None
        for hd in heads:
            q_sq = jnp.max(jnp.where(head_lane == hd, q_max, 0.0), axis=1, keepdims=True)
            k_sq = jnp.max(mx[hd], axis=1, keepdims=True)
            bound = jnp.sqrt(q_sq * k_sq) * MLA_BOUND_SLACK
            bound_ref[hd] = jnp.broadcast_to(bound, bound_ref.shape[1:])
            worst = bound if worst is None else jnp.maximum(worst, bound)
        fast_ref[0] = (jnp.max(worst) <= MLA_SAFE_SCORE_BOUND).astype(jnp.int32)

    fast = fast_ref[0] == 1

    l_ref[...] = jnp.zeros(l_ref.shape, _F32)
    acc_ref[...] = jnp.zeros(acc_ref.shape, _F32)
    n_full = (qi * MLA_TQ) // MLA_TK

    def causal_mask(s, start):
        row = lax.broadcasted_iota(jnp.int32, s.shape, 0) + qi * MLA_TQ
        col = lax.broadcasted_iota(jnp.int32, s.shape, 1) + start
        return jnp.where(col <= row, s, MASK_VALUE)

    def run(step):
        def body(c, carry):
            step(c, False)
            return carry
        lax.fori_loop(0, n_full, body, 0)
        step(n_full, True)

    @pl.when(fast)
    def _():
        for hd in heads:
            lane = lax.broadcasted_iota(jnp.int32, (MLA_TQ, MLA_QK_PAD), 1)
            shift = jnp.broadcast_to(jnp.concatenate([-bound_ref[hd, 0:1, :]] * (MLA_QK_PAD // LANES), axis=1),
                                     (MLA_TQ, MLA_QK_PAD)).astype(_BF)
            qs_ref[hd] = jnp.where(lane == MLA_QK_REAL, shift, q_ref[0, :, slot(hd)])

        def step(c, masked):
            start = pl.multiple_of(c * MLA_TK, MLA_TK)
            for hd in heads:
                s = _dot(qs_ref[hd], kt_ref[c, slot(hd), :])
                if masked:
                    s = causal_mask(s, start)
                p = jnp.exp2(s)
                part = p[:, :LANES]
                for r in range(1, reps):
                    part = part + p[:, r * LANES:(r + 1) * LANES]
                l_ref[hd] += part
                acc_ref[hd] += _dot(p.astype(_BF), v_ref[0, pl.ds(start, MLA_TK), vslot(hd)])

        run(step)
        for hd in heads:
            denom = jnp.sum(l_ref[hd], axis=1, keepdims=True)
            o_ref[0, :, vslot(hd)] = (acc_ref[hd] / denom).astype(o_ref.dtype)

    @pl.when(jnp.logical_not(fast))
    def _():
        m_ref[...] = jnp.full(m_ref.shape, MASK_VALUE, _F32)

        def step(c, masked):
            start = pl.multiple_of(c * MLA_TK, MLA_TK)
            for hd in heads:
                s = _dot(q_ref[0, :, slot(hd)], kt_ref[c, slot(hd), :])
                if masked:
                    s = causal_mask(s, start)
                m_prev = m_ref[hd]
                m_next = jnp.maximum(m_prev, jnp.max(s, axis=1, keepdims=True))
                p = jnp.exp2(s - jnp.concatenate([m_next] * reps, axis=1))
                alpha = jnp.exp2(m_prev - m_next)
                l_ref[hd] = alpha * l_ref[hd] + jnp.sum(p, axis=1, keepdims=True)
                acc_ref[hd] = alpha * acc_ref[hd] + _dot(p.astype(_BF), v_ref[0, pl.ds(start, MLA_TK), vslot(hd)])
                m_ref[hd] = m_next

        run(step)
        for hd in heads:
            o_ref[0, :, vslot(hd)] = (acc_ref[hd] / l_ref[hd]).astype(o_ref.dtype)


def _mla(q, kt, v, qmax):
    b, s, _ = q.shape
    hb = MLA_HEADS_PER_STEP
    n_chunks = s // MLA_TK
    assert MLA_TK % MLA_TQ == 0 and s % MLA_TK == 0 and MLA_HEADS % hb == 0
    resident = dict(pipeline_mode=pl.Buffered(1))
    return pl.pallas_call(
        _mla_kernel,
        grid=(b, MLA_HEADS // hb, s // MLA_TQ),
        in_specs=[pl.BlockSpec((1, MLA_TQ, hb * MLA_QK_PAD), lambda bi, hi, qi: (bi, qi, hi)),
                  pl.BlockSpec((n_chunks, hb * MLA_QK_PAD, MLA_TK), lambda bi, hi, qi: (bi, hi, 0), **resident),
                  pl.BlockSpec((1, s, hb * MLA_V), lambda bi, hi, qi: (bi, 0, hi), **resident),
                  pl.BlockSpec((qmax.shape[0] // b, 8, LANES), lambda bi, hi, qi: (bi, 0, 0))],
        out_specs=pl.BlockSpec((1, MLA_TQ, hb * MLA_V), lambda bi, hi, qi: (bi, qi, hi)),
        out_shape=jax.ShapeDtypeStruct((b, s, MLA_HEADS * MLA_V), _BF),
        scratch_shapes=[pltpu.VMEM((hb, MLA_TQ, LANES), _F32)] * 3
        + [pltpu.VMEM((hb, MLA_TQ, MLA_QK_PAD), _BF), pltpu.VMEM((hb, 8, LANES), _F32),
           pltpu.SMEM((1,), jnp.int32)],
        compiler_params=_params(("parallel", "parallel", "arbitrary")),
        name="mla_attention",
    )(q, kt, v, qmax)


def _sb_kernel(q_ref, kt_ref, v_ref, tri_ref, o_ref, suf_ref, acc_ref, zb_ref, kmax_ref):
    qi = pl.program_id(2)
    n_key_chunks = kt_ref.shape[0]
    heads = list(range(SB_HEADS_PER_STEP))
    lanes_of = lambda hd: slice(hd * SB_HEAD_DIM, (hd + 1) * SB_HEAD_DIM)

    @pl.when(qi == 0)
    def _():
        def key_norm(i, mx):
            kc = kt_ref[i].astype(_F32)
            kc = kc * kc
            ssq = [jnp.sum(kc[lanes_of(hd)], axis=0, keepdims=True) for hd in heads]
            return tuple(jnp.maximum(m, s) for m, s in zip(mx, ssq))
        mx = lax.fori_loop(0, n_key_chunks, key_norm, tuple(jnp.zeros((1, SB_TK), _F32) for _ in heads))
        for hd in heads:
            kmax_ref[hd] = jnp.broadcast_to(jnp.max(mx[hd], axis=1, keepdims=True), kmax_ref.shape[1:])

    tri = tri_ref[...]

    def chunk_group(items, sufs):
        halves = list(reversed(range(SB_TK // LANES)))
        zs, ts, masks = [], [], []
        for hd, c, masked in items:
            z = _dot(q_ref[0, :, lanes_of(hd)], kt_ref[c, lanes_of(hd), :])
            zs.append(z)
        for z, (hd, c, masked) in zip(zs, items):
            t = jnp.maximum(z, 0.0) + jnp.log(1.0 + jnp.exp(-jnp.abs(z)))
            causal = None
            if masked:
                causal = (lax.broadcasted_iota(jnp.int32, z.shape, 1)
                          < lax.broadcasted_iota(jnp.int32, z.shape, 0))
                t = jnp.where(causal, t, 0.0)
            ts.append(t)
            masks.append(causal)
        sums = []
        for t in ts:
            per_half = {}
            for half in halves:
                th = t[:, half * LANES:(half + 1) * LANES]
                t_hi = th.astype(_BF)
                t_lo = (th - t_hi.astype(_F32)).astype(_BF)
                per_half[half] = _dot(jnp.concatenate([t_hi, t_lo], axis=1), tri)
            sums.append(per_half)
        outs = []
        for z, per_half, causal, (hd, c, masked) in zip(zs, sums, masks, items):
            suf = sufs[hd]
            log_w = {}
            for half in halves:
                log_w[half] = z[:, half * LANES:(half + 1) * LANES] - per_half[half][:, :LANES] - suf
                suf = suf + per_half[half][:, LANES:]
            sufs[hd] = suf
            a = jnp.exp(jnp.concatenate([log_w[h] for h in sorted(halves)], axis=1))
            if masked:
                a = jnp.where(causal, a, 0.0)
            start = pl.multiple_of(c * SB_TK, SB_TK)
            outs.append(_dot(a.astype(_BF), v_ref[0, pl.ds(start, SB_TK), lanes_of(hd)]))
        return outs

    def still_live(sufs):
        slack = zb_ref[0, 0:1, :] - sufs[0]
        for hd in heads[1:]:
            slack = jnp.maximum(slack, zb_ref[hd, 0:1, :] - sufs[hd])
        return (jnp.max(slack) >= SB_EXIT_LOG_WEIGHT).astype(jnp.int32)

    zero_suf = lambda: {hd: jnp.zeros((SB_TQ, LANES), _F32) for hd in heads}

    @pl.when(qi == 0)
    def _():
        outs = chunk_group([(hd, 0, True) for hd in heads], zero_suf())
        for hd in heads:
            o_ref[0, :, lanes_of(hd)] = outs[hd].astype(o_ref.dtype)

    @pl.when(qi > 0)
    def _():
        for hd in heads:
            q_sq = jnp.max(_row_sq_norm_bound(q_ref[0, :, lanes_of(hd)]), axis=0, keepdims=True)
            zb_ref[hd] = jnp.broadcast_to(jnp.sqrt(q_sq * kmax_ref[hd, 0:1, :]) * SB_BOUND_SLACK,
                                          zb_ref.shape[1:])
        sufs = zero_suf()
        items = [(hd, qi, True) for hd in heads] + [(hd, qi - 1, False) for hd in heads]
        outs = chunk_group(items, sufs)
        for hd in heads:
            acc_ref[hd] = outs[hd] + outs[len(heads) + hd]
            suf_ref[hd] = sufs[hd]

        def cond(carry):
            c, live = carry
            return jnp.logical_and(c >= 0, live > 0)

        def body(carry):
            c, _ = carry
            sufs = {hd: suf_ref[hd] for hd in heads}
            outs = chunk_group([(hd, c, False) for hd in heads], sufs)
            for hd in heads:
                acc_ref[hd] += outs[hd]
                suf_ref[hd] = sufs[hd]
            return c - 1, still_live(sufs)

        lax.while_loop(cond, body, (qi - 2, still_live(sufs)))
        for hd in heads:
            o_ref[0, :, lanes_of(hd)] = acc_ref[hd].astype(o_ref.dtype)


def _sb_tri():
    r = np.arange(LANES)
    upper = (r[:, None] >= r[None, :]).astype(np.float32)
    half = np.concatenate([upper, np.ones((LANES, LANES), np.float32)], axis=1)
    return jnp.asarray(np.concatenate([half, half], axis=0), dtype=_BF)


def _sb(q, kt, v):
    b, s, _ = q.shape
    hb = SB_HEADS_PER_STEP
    assert SB_TQ == SB_TK and SB_TK % LANES == 0 and s % SB_TQ == 0 and SB_HEADS % hb == 0
    tri = _sb_tri()
    width = hb * SB_HEAD_DIM
    return pl.pallas_call(
        _sb_kernel,
        grid=(b, SB_HEADS // hb, s // SB_TQ),
        in_specs=[pl.BlockSpec((1, SB_TQ, width), lambda bi, hi, qi: (bi, qi, hi)),
                  pl.BlockSpec((s // SB_TK, width, SB_TK), lambda bi, hi, qi: (bi, hi, 0)),
                  pl.BlockSpec((1, s, width), lambda bi, hi, qi: (bi, 0, hi)),
                  _const_spec(tri.shape)],
        out_specs=pl.BlockSpec((1, SB_TQ, width), lambda bi, hi, qi: (bi, qi, hi)),
        out_shape=jax.ShapeDtypeStruct((b, s, D_MODEL), _BF),
        scratch_shapes=[pltpu.VMEM((hb, SB_TQ, LANES), _F32)] * 2 + [pltpu.VMEM((hb, 8, LANES), _F32)] * 2,
        compiler_params=_params(("parallel", "parallel", "arbitrary")),
        name="sb_attention",
    )(q, kt, v, tri)


def _gla_constants():
    c = GLA_CHUNK
    idx = np.arange(c)
    seg = []
    masks = []
    for lvl in range(GLA_LEVELS):
        s = 1 << lvl
        blk = idx // s
        start = blk * s
        end = start + s - 1
        t = idx[None, :]
        q_side = ((blk % 2 == 1)[:, None] & (t >= start[:, None]) & (t <= idx[:, None]))
        k_side = ((blk % 2 == 0)[:, None] & (t > idx[:, None]) & (t <= end[:, None]))
        seg.append(np.where((blk % 2 == 1)[:, None], q_side, k_side))
        pair = ((idx[:, None] // (2 * s)) == (idx[None, :] // (2 * s))) \
            & ((blk % 2 == 1)[:, None]) & ((blk % 2 == 0)[None, :])
        masks.append(np.tile(pair, (GLA_HEADS, 1)))
    t = idx[None, :]
    full = np.concatenate([t <= idx[:, None], t > idx[:, None]], axis=0)
    seg = np.stack(seg).astype(np.float32)
    masks = np.stack(masks).astype(np.float32)
    head_of_qk = np.arange(GLA_QK) // GLA_DK
    head_of_v = np.arange(GLA_V) // GLA_DV
    head_mask = (np.arange(8)[:, None] == head_of_qk[None, :]).astype(np.float32)
    expand = (head_of_qk[:, None] == head_of_v[None, :]).astype(np.float32)
    return (jnp.asarray(seg, _BF), jnp.asarray(full, _BF), jnp.asarray(masks, _F32), jnp.asarray(head_mask, _F32),
            jnp.asarray(expand, _BF), jnp.asarray(expand.T, _F32))


def _gla_kernel(gq_ref, gk_ref, la_ref, gv_ref, gr_ref, seg_ref, full_ref, mask_ref, hm_ref, exp_ref, bd_ref,
                gn_ref, o_ref, state_ref):
    c = GLA_CHUNK

    @pl.when(pl.program_id(1) == 0)
    def _():
        state_ref[...] = jnp.zeros(state_ref.shape, _F32)

    hm = hm_ref[...]
    gn = gn_ref[...]

    def intra_chunks(row_slices):
        qs_, ks_, vs_, la2s = [], [], [], []
        for rows in row_slices:
            la = la_ref[0, rows, :]
            la_hi = la.astype(_BF)
            la_lo = (la - la_hi.astype(_F32)).astype(_BF)
            la2s.append(jnp.concatenate([la_hi, la_lo], axis=1))
            qs_.append(gq_ref[0, rows, :])
            ks_.append(gk_ref[0, rows, :])
            vs_.append(gv_ref[0, rows, :])
        segs = [seg_ref[lvl] for lvl in range(GLA_LEVELS)] + [full_ref[...]]
        sums = [[_dot(seg, la2) for seg in segs] for la2 in la2s]
        ws = [[jnp.exp(e[:, :GLA_QK] + e[:, GLA_QK:]) for e in per_chunk] for per_chunk in sums]
        pairs = []
        for q, k, w_chunk in zip(qs_, ks_, ws):
            per_level = []
            for lvl in range(GLA_LEVELS):
                w = w_chunk[lvl]
                ql = q * w
                kl = (k * w).astype(_BF)
                q_heads = jnp.concatenate([ql * hm[hd:hd + 1] for hd in range(GLA_HEADS)], axis=0).astype(_BF)
                per_level.append((q_heads, kl))
            pairs.append(per_level)
        scores = [[_dot_nt(q_heads, kl) for q_heads, kl in per_level] for per_level in pairs]
        out = []
        for q, k, v, w_chunk, per_level in zip(qs_, ks_, vs_, ws, scores):
            att = per_level[0] * mask_ref[0]
            for lvl in range(1, GLA_LEVELS):
                att = att + per_level[lvl] * mask_ref[lvl]
            w = w_chunk[GLA_LEVELS]
            o = _dot((q * k).astype(_BF), exp_ref[...]) * v.astype(_F32)
            o_intra = [_dot(att[hd * c:(hd + 1) * c, :].astype(_BF), v[:, hd * GLA_DV:(hd + 1) * GLA_DV])
                       for hd in range(GLA_HEADS)]
            o = o + jnp.concatenate(o_intra, axis=1)
            out.append((o, (q * w[:c]).astype(_BF), (k * w[c:]).astype(_BF), w[c - 1:c], v))
        return out

    chunks = intra_chunks([slice(i * c, (i + 1) * c) for i in range(GLA_CHUNKS_PER_STEP)])
    state = state_ref[...]
    for i, (o, q_state, k_state, decay, v) in enumerate(chunks):
        o = o + _dot_nt(q_state, state.astype(_BF))
        state = state * decay + _dot_tn(v, k_state) * bd_ref[...]
        gr = gr_ref[0, i * c:(i + 1) * c, :].astype(_F32)
        for hd in range(GLA_HEADS):
            sl = slice(hd * GLA_DV, (hd + 1) * GLA_DV)
            o_ref[0, i * c:(i + 1) * c, sl] = (_rms(o[:, sl], gn[:, sl]) * _silu(gr[:, sl])).astype(o_ref.dtype)
    state_ref[...] = state


def _gla(gq, gk, la, gv, gr, gla_norm):
    b, s, _ = gq.shape
    rows = GLA_CHUNK * GLA_CHUNKS_PER_STEP
    assert s % rows == 0 and (1 << GLA_LEVELS) == GLA_CHUNK
    consts = list(_gla_constants()) + [gla_norm]
    tok = lambda wd: pl.BlockSpec((1, rows, wd), lambda bi, ci: (bi, ci, 0))
    return pl.pallas_call(
        _gla_kernel,
        grid=(b, s // rows),
        in_specs=[tok(GLA_QK), tok(GLA_QK), tok(GLA_QK), tok(GLA_V), tok(GLA_V)]
        + [_const_spec(cn.shape) for cn in consts],
        out_specs=tok(GLA_V),
        out_shape=jax.ShapeDtypeStruct((b, s, GLA_V), _BF),
        scratch_shapes=[pltpu.VMEM((GLA_V, GLA_QK), _F32)],
        compiler_params=_params(("parallel", "arbitrary")),
        name="gla_chunked",
    )(gq, gk, la, gv, gr, *consts)


def _rotate_half_cols(w):
    half = w.shape[1] // 2
    return jnp.concatenate([-w[:, half:], w[:, :half]], axis=1)


def _pad_cols(w, width):
    return jnp.pad(w, ((0, 0), (0, width - w.shape[1])))


def _prep_even_weights(w_in, w_uq, w_ukv, w_gate2):
    splits = np.cumsum([MLA_Q_RANK, MLA_KV_RANK, MLA_ROPE, GLA_QK, GLA_QK, GLA_V, GLA_GATE_RANK])
    c_q, c_kv, k_r, g_q, g_k, g_v, g_a, g_r = jnp.split(w_in, splits, axis=1)
    win = jnp.concatenate([c_q, c_kv, _pad_cols(k_r, LANES), _pad_cols(_rotate_half_cols(k_r), LANES),
                           g_q, g_k, g_v, _pad_cols(g_a, LANES), g_r], axis=1).astype(_BF)
    uq = w_uq.reshape(MLA_Q_RANK, MLA_HEADS, MLA_NOPE + MLA_ROPE)
    uq_rope = uq[:, :, MLA_NOPE:]
    rot = jnp.concatenate([-uq_rope[:, :, MLA_ROPE // 2:], uq_rope[:, :, :MLA_ROPE // 2]], axis=2)
    wuq = jnp.pad(uq, ((0, 0), (0, 0), (0, MLA_QK_PAD - uq.shape[2]))).reshape(MLA_Q_RANK, -1).astype(_BF)
    wuqr = jnp.pad(rot, ((0, 0), (0, 0), (0, LANES - MLA_ROPE))).reshape(MLA_Q_RANK, -1).astype(_BF)
    ukv = w_ukv.reshape(MLA_KV_RANK, MLA_HEADS, MLA_NOPE + MLA_V)
    wukt = ukv[:, :, :MLA_NOPE].reshape(MLA_KV_RANK, -1).T.astype(_BF)
    wuv = ukv[:, :, MLA_NOPE:].reshape(MLA_KV_RANK, -1).astype(_BF)
    wg2 = jnp.pad(w_gate2, ((0, LANES - GLA_GATE_RANK), (0, 0))).astype(_BF)
    return dict(win=win, wuq=wuq, wuqr=wuqr, wukt=wukt, wuv=wuv, wg2=wg2)


def _rope_tables(positions):
    half = MLA_ROPE // 2
    inv_freq = 1.0 / (ROPE_THETA ** (jnp.arange(half, dtype=_F32) / half))
    ang = positions.astype(_F32)[..., None] * inv_freq
    cos, sin = jnp.cos(ang), jnp.sin(ang)
    pad = jnp.zeros(cos.shape[:-1] + (LANES - MLA_ROPE,), _F32)
    return (jnp.concatenate([cos, cos, pad], axis=-1).reshape(-1, LANES),
            jnp.concatenate([sin, sin, pad], axis=-1).reshape(-1, LANES))


def kernel(x, p, positions, ffn1_norm, ffn1_w_gu, ffn1_w_down, mix_norm, ffn2_norm, ffn2_w_gu, ffn2_w_down, ple_norm, ple_w_gate, ple_w_proj, ev_w_in, ev_q_norm, ev_kv_norm, ev_w_uq, ev_w_ukv, ev_w_gate2, ev_b_gate, ev_gla_norm, ev_w_out, od_w_qkv, od_w_out, final_norm):
    b, s, d = x.shape
    depth = p.shape[0]
    t = b * s
    cos_t, sin_t = _rope_tables(positions)
    row = lambda a: a.reshape(1, -1).astype(_F32)
    xt = x.reshape(t, d)
    p3 = p.reshape(depth, t, p.shape[-1])
    fin = row(final_norm)
    for i in range(depth):
        j = i // 2
        common = dict(n1=row(ffn1_norm[i]), wgu=ffn1_w_gu[i].astype(_BF), wdn=ffn1_w_down[i].astype(_BF),
                      nm=row(mix_norm[i]), n2=row(ffn2_norm[i]), wgu2=ffn2_w_gu[i].astype(_BF),
                      wdn2=ffn2_w_down[i].astype(_BF), np=row(ple_norm[i]), wpg=ple_w_gate[i].astype(_BF),
                      wpp=ple_w_proj[i].astype(_BF))
        if i % 2 == 0:
            w = dict(common, **_prep_even_weights(ev_w_in[j], ev_w_uq[j], ev_w_ukv[j], ev_w_gate2[j]),
                     qn=row(ev_q_norm[j]), kvn=row(ev_kv_norm[j]), bg=row(ev_b_gate[j]))
            x1, q, kt, v, qmax, gq, gk, la, gv, gr = _pre_even(xt, cos_t, sin_t, w)
            seq = lambda a: a.reshape(b, s, a.shape[-1])
            o_mla = _mla(seq(q), kt, seq(v), qmax)
            o_gla = _gla(seq(gq), seq(gk), seq(la), seq(gv), seq(gr), row(ev_gla_norm[j]))
            mixes = [o_mla.reshape(t, -1), o_gla.reshape(t, -1)]
            w_out = ev_w_out[j].astype(_BF)
            wouts = [w_out[:MLA_HEADS * MLA_V], w_out[MLA_HEADS * MLA_V:]]
        else:
            wqkv = od_w_qkv[j].astype(_BF)
            w = dict(common, wq=wqkv[:, :D_MODEL], wk=wqkv[:, D_MODEL:2 * D_MODEL], wv=wqkv[:, 2 * D_MODEL:])
            x1, q, kt, v = _pre_odd(xt, w)
            seq = lambda a: a.reshape(b, s, a.shape[-1])
            o_sb = _sb(seq(q), kt, seq(v))
            mixes = [o_sb.reshape(t, -1)]
            wouts = [od_w_out[j].astype(_BF)]
        xt = _post(x1, mixes, p3, i, wouts, w, fin, i == depth - 1)
    return xt.reshape(b, s, d)
```

```python
import functools

import numpy as np
import jax
import jax.numpy as jnp
from jax import lax
from jax.experimental import pallas as pl
from jax.experimental.pallas import tpu as pltpu

D_MODEL = 1024
P_DIM = 256
EPS = 1e-6
D_FF = 1408
MLA_HEADS = 4
MLA_Q_RANK = 256
MLA_KV_RANK = 128
MLA_NOPE = 128
MLA_ROPE = 64
MLA_V = 128
ROPE_THETA = 10000.0
GLA_HEADS = 4
GLA_DK = 64
GLA_DV = 128
GLA_GATE_RANK = 16
GLA_GATE_TAU = 16.0
SB_HEADS = 8
SB_HEAD_DIM = D_MODEL // SB_HEADS

LANES = 128
MLA_QK_PAD = 2 * LANES
GLA_QK = GLA_HEADS * GLA_DK
GLA_V = GLA_HEADS * GLA_DV

TOKEN_TILE = 512
TOKEN_ROW_GROUPS = 2
MLA_TQ = 512
MLA_TK = 512
MLA_HEADS_PER_STEP = 4
SB_TQ = 256
SB_TK = 256
SB_HEADS_PER_STEP = 4
SB_EXIT_LOG_WEIGHT = -106.0
SB_BOUND_SLACK = 1.001
GLA_CHUNK = 128
GLA_CHUNKS_PER_STEP = 4
GLA_LEVELS = 7
VMEM_LIMIT = 56 * 1024 * 1024
MASK_VALUE = -1e30
MLA_SAFE_SCORE_BOUND = 60.0
MLA_BOUND_SLACK = 1.001
MLA_QK_REAL = 192
LOG2_E = 1.4426950408889634

_BF = jnp.bfloat16
_F32 = jnp.float32


def _dot(a, b):
    return jnp.dot(a, b, preferred_element_type=_F32)


def _dot_nt(a, b):
    return lax.dot_general(a, b, (((1,), (1,)), ((), ())), preferred_element_type=_F32)


def _dot_tn(a, b):
    return lax.dot_general(a, b, (((0,), (0,)), ((), ())), preferred_element_type=_F32)


def _rms(x, w):
    return x * lax.rsqrt(jnp.mean(x * x, axis=-1, keepdims=True) + EPS) * w


def _sigmoid(x):
    return 1.0 / (1.0 + jnp.exp(-x))


def _silu(x):
    return x * _sigmoid(x)


def _log_sigmoid(x):
    return jnp.minimum(x, 0.0) - jnp.log(1.0 + jnp.exp(-jnp.abs(x)))


def _row_sq_norm_bound(x):
    return _dot(x * x, jnp.ones((x.shape[1], LANES), x.dtype)) * (1.0 + 2.0 ** -8)


def _row_groups(rows):
    step = rows // TOKEN_ROW_GROUPS
    return [slice(i * step, (i + 1) * step) for i in range(TOKEN_ROW_GROUPS)]


def _ffn_groups(xs, norm_w, w_gu, w_down):
    hs = [_rms(x, norm_w).astype(_BF) for x in xs]
    gus = [_dot(h, w_gu) for h in hs]
    acts = [(_silu(gu[:, :D_FF]) * gu[:, D_FF:]).astype(_BF) for gu in gus]
    downs = [_dot(act, w_down) for act in acts]
    return [x + 0.5 * d for x, d in zip(xs, downs)]


def _const_spec(shape):
    nd = len(shape)
    return pl.BlockSpec(shape, lambda *_: (0,) * nd, pipeline_mode=pl.Buffered(1))


def _params(semantics):
    return pltpu.CompilerParams(dimension_semantics=semantics, vmem_limit_bytes=VMEM_LIMIT)


def _pre_even_kernel(x_ref, cos_ref, sin_ref, n1_ref, wgu_ref, wdn_ref, nm_ref, win_ref, qn_ref, kvn_ref,
                     wuq_ref, wuqr_ref, wukt_ref, wuv_ref, wg2_ref, bg_ref,
                     x1_ref, q_ref, kt_ref, v_ref, qmax_ref, gq_ref, gk_ref, la_ref, gv_ref, gr_ref):
    groups = _row_groups(x_ref.shape[0])
    x1s = _ffn_groups([x_ref[g, :] for g in groups], n1_ref[...], wgu_ref[...], wdn_ref[...])
    for g, x1 in zip(groups, x1s):
        x1_ref[g, :] = x1
    hs = [_rms(x1, nm_ref[...]).astype(_BF) for x1 in x1s]
    projs = [_dot(h, win_ref[...]) for h in hs]
    cqs = [_rms(proj[:, 0:256], qn_ref[...]).astype(_BF) for proj in projs]
    ckvs = [_rms(proj[:, 256:384], kvn_ref[...]).astype(_BF) for proj in projs]
    qs = [_dot(cq, wuq_ref[...]) for cq in cqs]
    q_rots = [_dot(cq, wuqr_ref[...]) for cq in cqs]
    k_nope_ts = [_dot_nt(wukt_ref[...], ckv) for ckv in ckvs]
    vs = [_dot(ckv, wuv_ref[...]) for ckv in ckvs]
    gates = [_dot(proj[:, 1664:1792].astype(_BF), wg2_ref[...]) + bg_ref[...] for proj in projs]
    scale = (MLA_NOPE + MLA_ROPE) ** -0.5 * LOG2_E
    q_sq_max = jnp.zeros(qmax_ref.shape[1:], _F32)
    head_lane = lax.broadcasted_iota(jnp.int32, q_sq_max.shape, 1)
    for g, proj, q, q_rot, k_nope_t, v, gate in zip(groups, projs, qs, q_rots, k_nope_ts, vs, gates):
        cos = cos_ref[g, :]
        sin = sin_ref[g, :]
        k_r = proj[:, 384:512]
        k_rr = proj[:, 512:640]
        for hd in range(MLA_HEADS):
            lo = hd * MLA_QK_PAD
            q_nope = q[:, lo:lo + LANES] * scale
            q_rope = (q[:, lo + LANES:lo + 2 * LANES] * cos + q_rot[:, hd * LANES:(hd + 1) * LANES] * sin) * scale
            q_ref[g, lo:lo + LANES] = q_nope.astype(_BF)
            q_ref[g, lo + LANES:lo + 2 * LANES] = q_rope.astype(_BF)
            sq = jnp.sum(q_nope * q_nope + q_rope * q_rope, axis=1, keepdims=True) * (1.0 + 2.0 ** -8) ** 2
            q_sq_max = jnp.where(head_lane == hd, jnp.maximum(q_sq_max, jnp.max(sq, axis=0, keepdims=True)), q_sq_max)
        k_rope_t = (k_r * cos + k_rr * sin).T
        pad_row = lax.broadcasted_iota(jnp.int32, k_rope_t.shape, 0) == MLA_ROPE
        k_rope_t = jnp.where(pad_row, 1.0, k_rope_t).astype(_BF)
        for hd in range(MLA_HEADS):
            lo = hd * MLA_QK_PAD
            kt_ref[0, lo:lo + LANES, g] = k_nope_t[hd * LANES:(hd + 1) * LANES].astype(_BF)
            kt_ref[0, lo + LANES:lo + 2 * LANES, g] = k_rope_t
        v_ref[g, :] = v.astype(_BF)
        gq_ref[g, :] = proj[:, 640:896] * (GLA_DK ** -0.5)
        gk_ref[g, :] = proj[:, 896:1152]
        la_ref[g, :] = _log_sigmoid(gate) * (1.0 / GLA_GATE_TAU)
        gv_ref[g, :] = proj[:, 1152:1664].astype(_BF)
        gr_ref[g, :] = proj[:, 1792:2304].astype(_BF)
    qmax_ref[0] = q_sq_max


def _pre_odd_kernel(x_ref, n1_ref, wgu_ref, wdn_ref, nm_ref, wq_ref, wk_ref, wv_ref,
                    x1_ref, q_ref, kt_ref, v_ref):
    groups = _row_groups(x_ref.shape[0])
    x1s = _ffn_groups([x_ref[g, :] for g in groups], n1_ref[...], wgu_ref[...], wdn_ref[...])
    for g, x1 in zip(groups, x1s):
        x1_ref[g, :] = x1
    hs = [_rms(x1, nm_ref[...]).astype(_BF) for x1 in x1s]
    qs = [_dot(h, wq_ref[...]) for h in hs]
    ks = [_dot(h, wk_ref[...]) for h in hs]
    vs = [_dot(h, wv_ref[...]) for h in hs]
    for j, (g, q, k, v) in enumerate(zip(groups, qs, ks, vs)):
        q_ref[g, :] = (q * (SB_HEAD_DIM ** -0.5)).astype(_BF)
        kt_ref[j] = k.T.astype(_BF)
        v_ref[g, :] = v.astype(_BF)


def _row_spec(width, tm):
    return pl.BlockSpec((tm, width), lambda i: (i, 0))


def _pre_even(x, cos_t, sin_t, w):
    t = x.shape[0]
    tm = min(TOKEN_TILE, t)
    consts = [w["n1"], w["wgu"], w["wdn"], w["nm"], w["win"], w["qn"], w["kvn"], w["wuq"], w["wuqr"],
              w["wukt"], w["wuv"], w["wg2"], w["bg"]]
    assert tm == MLA_TK
    kt_rows = MLA_HEADS * MLA_QK_PAD
    tile_outs = {"kt": ((kt_rows, tm), _BF), "qmax": ((8, LANES), _F32)}
    out_widths = [(D_MODEL, _F32), (MLA_HEADS * MLA_QK_PAD, _BF), "kt",
                  (MLA_HEADS * MLA_V, _BF), "qmax", (GLA_QK, _F32), (GLA_QK, _F32), (GLA_QK, _F32),
                  (GLA_V, _BF), (GLA_V, _BF)]
    out_specs = [pl.BlockSpec((1,) + tile_outs[o][0], lambda i: (i, 0, 0)) if isinstance(o, str)
                 else _row_spec(o[0], tm) for o in out_widths]
    out_shape = [jax.ShapeDtypeStruct((t // tm,) + tile_outs[o][0], tile_outs[o][1]) if isinstance(o, str)
                 else jax.ShapeDtypeStruct((t, o[0]), o[1]) for o in out_widths]
    return pl.pallas_call(
        _pre_even_kernel,
        grid=(t // tm,),
        in_specs=[_row_spec(D_MODEL, tm), _row_spec(LANES, tm), _row_spec(LANES, tm)]
        + [_const_spec(c.shape) for c in consts],
        out_specs=out_specs,
        out_shape=out_shape,
        compiler_params=_params(("parallel",)),
        name="pre_even",
    )(x, cos_t, sin_t, *consts)


def _pre_odd(x, w):
    t = x.shape[0]
    tm = min(TOKEN_TILE, t)
    consts = [w["n1"], w["wgu"], w["wdn"], w["nm"], w["wq"], w["wk"], w["wv"]]
    assert tm == TOKEN_ROW_GROUPS * SB_TK
    out_widths = [(D_MODEL, _F32), (D_MODEL, _BF), None, (D_MODEL, _BF)]
    out_specs = [pl.BlockSpec((tm // SB_TK, D_MODEL, SB_TK), lambda i: (i, 0, 0)) if o is None
                 else _row_spec(o[0], tm) for o in out_widths]
    out_shape = [jax.ShapeDtypeStruct((t // SB_TK, D_MODEL, SB_TK), _BF) if o is None
                 else jax.ShapeDtypeStruct((t, o[0]), o[1]) for o in out_widths]
    return pl.pallas_call(
        _pre_odd_kernel,
        grid=(t // tm,),
        in_specs=[_row_spec(D_MODEL, tm)] + [_const_spec(c.shape) for c in consts],
        out_specs=out_specs,
        out_shape=out_shape,
        compiler_params=_params(("parallel",)),
        name="pre_odd",
    )(x, *consts)


def _post_kernel(n_mix, final, *refs):
    x_ref = refs[0]
    mix_refs = refs[1:1 + n_mix]
    p_ref = refs[1 + n_mix]
    wout_refs = refs[2 + n_mix:2 + 2 * n_mix]
    n2_ref, wgu_ref, wdn_ref, np_ref, wpg_ref, wpp_ref, nf_ref, out_ref = refs[2 + 2 * n_mix:]
    groups = _row_groups(x_ref.shape[0])
    xs = [x_ref[g, :] for g in groups]
    for m_ref, w_ref in zip(mix_refs, wout_refs):
        ds = [_dot(m_ref[g, :], w_ref[...]) for g in groups]
        xs = [x + d for x, d in zip(xs, ds)]
    xs = _ffn_groups(xs, n2_ref[...], wgu_ref[...], wdn_ref[...])
    hs = [_rms(x, np_ref[...]).astype(_BF) for x in xs]
    gates = [_sigmoid(_dot(h, wpg_ref[...])) for h in hs]
    projs = [_dot(p_ref[g, :].astype(_BF), wpp_ref[...]) for g in groups]
    xs = [x + gate * proj for x, gate, proj in zip(xs, gates, projs)]
    if final:
        xs = [_rms(x, nf_ref[...]) for x in xs]
    for g, x in zip(groups, xs):
        out_ref[g, :] = x


def _post(x, mixes, p, layer, wouts, w, final_norm, final):
    t = x.shape[0]
    tm = min(TOKEN_TILE, t)
    consts = list(wouts) + [w["n2"], w["wgu2"], w["wdn2"], w["np"], w["wpg"], w["wpp"], final_norm]
    return pl.pallas_call(
        functools.partial(_post_kernel, len(mixes), final),
        grid=(t // tm,),
        in_specs=[_row_spec(D_MODEL, tm)] + [_row_spec(m.shape[1], tm) for m in mixes]
        + [pl.BlockSpec((None, tm, P_DIM), lambda i: (layer, i, 0))] + [_const_spec(c.shape) for c in consts],
        out_specs=_row_spec(D_MODEL, tm),
        out_shape=jax.ShapeDtypeStruct((t, D_MODEL), _F32),
        compiler_params=_params(("parallel",)),
        name="post_final" if final else "post",
    )(x, *mixes, p, *consts)


def _mla_kernel(q_ref, kt_ref, v_ref, qmax_ref, o_ref, m_ref, l_ref, acc_ref, qs_ref, bound_ref, fast_ref):
    qi = pl.program_id(2)
    heads = range(MLA_HEADS_PER_STEP)
    slot = lambda hd: slice(hd * MLA_QK_PAD, (hd + 1) * MLA_QK_PAD)
    vslot = lambda hd: slice(hd * MLA_V, (hd + 1) * MLA_V)
    reps = MLA_TK // LANES

    @pl.when(qi == 0)
    def _():
        def key_norm(i, mx):
            kc = kt_ref[i].astype(_F32)
            kc = kc * kc
            ssq = [jnp.sum(kc[hd * MLA_QK_PAD:hd * MLA_QK_PAD + MLA_QK_REAL], axis=0, keepdims=True)
                   for hd in heads]
            return tuple(jnp.maximum(m, s) for m, s in zip(mx, ssq))
        mx = lax.fori_loop(0, kt_ref.shape[0], key_norm, tuple(jnp.zeros((1, MLA_TK), _F32) for _ in heads))
        q_max = jnp.max(qmax_ref[...], axis=0)[0:1, :]
        head_lane = lax.broadcasted_iota(jnp.int32, q_max.shape, 1) - pl.program_id(1) * MLA_HEADS_PER_STEP
        worst = None
        for hd in heads:
            q_sq = jnp.max(jnp.where(head_lane == hd, q_max, 0.0), axis=1, keepdims=True)
            k_sq = jnp.max(mx[hd], axis=1, keepdims=True)
            bound = jnp.sqrt(q_sq * k_sq) * MLA_BOUND_SLACK
            bound_ref[hd] = jnp.broadcast_to(bound, bound_ref.shape[1:])
            worst = bound if worst is None else jnp.maximum(worst, bound)
        fast_ref[0] = (jnp.max(worst) <= MLA_SAFE_SCORE_BOUND).astype(jnp.int32)

    fast = fast_ref[0] == 1

    l_ref[...] = jnp.zeros(l_ref.shape, _F32)
    acc_ref[...] = jnp.zeros(acc_ref.shape, _F32)
    n_full = (qi * MLA_TQ) // MLA_TK

    def causal_mask(s, start):
        row = lax.broadcasted_iota(jnp.int32, s.shape, 0) + qi * MLA_TQ
        col = lax.broadcasted_iota(jnp.int32, s.shape, 1) + start
        return jnp.where(col <= row, s, MASK_VALUE)

    def run(step):
        def body(c, carry):
            step(c, False)
            return carry
        lax.fori_loop(0, n_full, body, 0)
        step(n_full, True)

    @pl.when(fast)
    def _():
        for hd in heads:
            lane = lax.broadcasted_iota(jnp.int32, (MLA_TQ, MLA_QK_PAD), 1)
            shift = jnp.broadcast_to(jnp.concatenate([-bound_ref[hd, 0:1, :]] * (MLA_QK_PAD // LANES), axis=1),
                                     (MLA_TQ, MLA_QK_PAD)).astype(_BF)
            qs_ref[hd] = jnp.where(lane == MLA_QK_REAL, shift, q_ref[0, :, slot(hd)])

        def step(c, masked):
            start = pl.multiple_of(c * MLA_TK, MLA_TK)
            for hd in heads:
                s = _dot(qs_ref[hd], kt_ref[c, slot(hd), :])
                if masked:
                    s = causal_mask(s, start)
                p = jnp.exp2(s)
                part = p[:, :LANES]
                for r in range(1, reps):
                    part = part + p[:, r * LANES:(r + 1) * LANES]
                l_ref[hd] += part
                acc_ref[hd] += _dot(p.astype(_BF), v_ref[0, pl.ds(start, MLA_TK), vslot(hd)])

        run(step)
        for hd in heads:
            denom = jnp.sum(l_ref[hd], axis=1, keepdims=True)
            o_ref[0, :, vslot(hd)] = (acc_ref[hd] / denom).astype(o_ref.dtype)

    @pl.when(jnp.logical_not(fast))
    def _():
        m_ref[...] = jnp.full(m_ref.shape, MASK_VALUE, _F32)

        def step(c, masked):
            start = pl.multiple_of(c * MLA_TK, MLA_TK)
            for hd in heads:
                s = _dot(q_ref[0, :, slot(hd)], kt_ref[c, slot(hd), :])
                if masked:
                    s = causal_mask(s, start)
                m_prev = m_ref[hd]
                m_next = jnp.maximum(m_prev, jnp.max(s, axis=1, keepdims=True))
                p = jnp.exp2(s - jnp.concatenate([m_next] * reps, axis=1))
                alpha = jnp.exp2(m_prev - m_next)
                l_ref[hd] = alpha * l_ref[hd] + jnp.sum(p, axis=1, keepdims=True)
                acc_ref[hd] = alpha * acc_ref[hd] + _dot(p.astype(_BF), v_ref[0, pl.ds(start, MLA_TK), vslot(hd)])
                m_ref[hd] = m_next

        run(step)
        for hd in heads:
            o_ref[0, :, vslot(hd)] = (acc_ref[hd] / l_ref[hd]).astype(o_ref.dtype)


def _mla(q, kt, v, qmax):
    b, s, _ = q.shape
    hb = MLA_HEADS_PER_STEP
    n_chunks = s // MLA_TK
    assert MLA_TK % MLA_TQ == 0 and s % MLA_TK == 0 and MLA_HEADS % hb == 0
    resident = dict(pipeline_mode=pl.Buffered(1))
    return pl.pallas_call(
        _mla_kernel,
        grid=(b, MLA_HEADS // hb, s // MLA_TQ),
        in_specs=[pl.BlockSpec((1, MLA_TQ, hb * MLA_QK_PAD), lambda bi, hi, qi: (bi, qi, hi)),
                  pl.BlockSpec((n_chunks, hb * MLA_QK_PAD, MLA_TK), lambda bi, hi, qi: (bi, hi, 0), **resident),
                  pl.BlockSpec((1, s, hb * MLA_V), lambda bi, hi, qi: (bi, 0, hi), **resident),
                  pl.BlockSpec((qmax.shape[0] // b, 8, LANES), lambda bi, hi, qi: (bi, 0, 0))],
        out_specs=pl.BlockSpec((1, MLA_TQ, hb * MLA_V), lambda bi, hi, qi: (bi, qi, hi)),
        out_shape=jax.ShapeDtypeStruct((b, s, MLA_HEADS * MLA_V), _BF),
        scratch_shapes=[pltpu.VMEM((hb, MLA_TQ, LANES), _F32)] * 3
        + [pltpu.VMEM((hb, MLA_TQ, MLA_QK_PAD), _BF), pltpu.VMEM((hb, 8, LANES), _F32),
           pltpu.SMEM((1,), jnp.int32)],
        compiler_params=_params(("parallel", "parallel", "arbitrary")),
        name="mla_attention",
    )(q, kt, v, qmax)


def _sb_kernel(q_ref, kt_ref, v_ref, tri_ref, o_ref, suf_ref, acc_ref, zb_ref, kmax_ref):
    qi = pl.program_id(2)
    n_key_chunks = kt_ref.shape[0]
    heads = list(range(SB_HEADS_PER_STEP))
    lanes_of = lambda hd: slice(hd * SB_HEAD_DIM, (hd + 1) * SB_HEAD_DIM)

    @pl.when(qi == 0)
    def _():
        def key_norm(i, mx):
            kc = kt_ref[i].astype(_F32)
            kc = kc * kc
            ssq = [jnp.sum(kc[lanes_of(hd)], axis=0, keepdims=True) for hd in heads]
            return tuple(jnp.maximum(m, s) for m, s in zip(mx, ssq))
        mx = lax.fori_loop(0, n_key_chunks, key_norm, tuple(jnp.zeros((1, SB_TK), _F32) for _ in heads))
        for hd in heads:
            kmax_ref[hd] = jnp.broadcast_to(jnp.max(mx[hd], axis=1, keepdims=True), kmax_ref.shape[1:])

    tri = tri_ref[...]

    def chunk_group(items, sufs):
        halves = list(reversed(range(SB_TK // LANES)))
        zs, ts, masks = [], [], []
        for hd, c, masked in items:
            z = _dot(q_ref[0, :, lanes_of(hd)], kt_ref[c, lanes_of(hd), :])
            zs.append(z)
        for z, (hd, c, masked) in zip(zs, items):
            t = jnp.maximum(z, 0.0) + jnp.log(1.0 + jnp.exp(-jnp.abs(z)))
            causal = None
            if masked:
                causal = (lax.broadcasted_iota(jnp.int32, z.shape, 1)
                          < lax.broadcasted_iota(jnp.int32, z.shape, 0))
                t = jnp.where(causal, t, 0.0)
            ts.append(t)
            masks.append(causal)
        sums = []
        for t in ts:
            per_half = {}
            for half in halves:
                th = t[:, half * LANES:(half + 1) * LANES]
                t_hi = th.astype(_BF)
                t_lo = (th - t_hi.astype(_F32)).astype(_BF)
                per_half[half] = _dot(jnp.concatenate([t_hi, t_lo], axis=1), tri)
            sums.append(per_half)
        outs = []
        for z, per_half, causal, (hd, c, masked) in zip(zs, sums, masks, items):
            suf = sufs[hd]
            log_w = {}
            for half in halves:
                log_w[half] = z[:, half * LANES:(half + 1) * LANES] - per_half[half][:, :LANES] - suf
                suf = suf + per_half[half][:, LANES:]
            sufs[hd] = suf
            a = jnp.exp(jnp.concatenate([log_w[h] for h in sorted(halves)], axis=1))
            if masked:
                a = jnp.where(causal, a, 0.0)
            start = pl.multiple_of(c * SB_TK, SB_TK)
            outs.append(_dot(a.astype(_BF), v_ref[0, pl.ds(start, SB_TK), lanes_of(hd)]))
        return outs

    def still_live(sufs):
        slack = zb_ref[0, 0:1, :] - sufs[0]
        for hd in heads[1:]:
            slack = jnp.maximum(slack, zb_ref[hd, 0:1, :] - sufs[hd])
        return (jnp.max(slack) >= SB_EXIT_LOG_WEIGHT).astype(jnp.int32)

    zero_suf = lambda: {hd: jnp.zeros((SB_TQ, LANES), _F32) for hd in heads}

    @pl.when(qi == 0)
    def _():
        outs = chunk_group([(hd, 0, True) for hd in heads], zero_suf())
        for hd in heads:
            o_ref[0, :, lanes_of(hd)] = outs[hd].astype(o_ref.dtype)

    @pl.when(qi > 0)
    def _():
        for hd in heads:
            q_sq = jnp.max(_row_sq_norm_bound(q_ref[0, :, lanes_of(hd)]), axis=0, keepdims=True)
            zb_ref[hd] = jnp.broadcast_to(jnp.sqrt(q_sq * kmax_ref[hd, 0:1, :]) * SB_BOUND_SLACK,
                                          zb_ref.shape[1:])
        sufs = zero_suf()
        items = [(hd, qi, True) for hd in heads] + [(hd, qi - 1, False) for hd in heads]
        outs = chunk_group(items, sufs)
        for hd in heads:
            acc_ref[hd] = outs[hd] + outs[len(heads) + hd]
            suf_ref[hd] = sufs[hd]

        def cond(carry):
            c, live = carry
            return jnp.logical_and(c >= 0, live > 0)

        def body(carry):
            c, _ = carry
            sufs = {hd: suf_ref[hd] for hd in heads}
            outs = chunk_group([(hd, c, False) for hd in heads], sufs)
            for hd in heads:
                acc_ref[hd] += outs[hd]
                suf_ref[hd] = sufs[hd]
            return c - 1, still_live(sufs)

        lax.while_loop(cond, body, (qi - 2, still_live(sufs)))
        for hd in heads:
            o_ref[0, :, lanes_of(hd)] = acc_ref[hd].astype(o_ref.dtype)


def _sb_tri():
    r = np.arange(LANES)
    upper = (r[:, None] >= r[None, :]).astype(np.float32)
    half = np.concatenate([upper, np.ones((LANES, LANES), np.float32)], axis=1)
    return jnp.asarray(np.concatenate([half, half], axis=0), dtype=_BF)


def _sb(q, kt, v):
    b, s, _ = q.shape
    hb = SB_HEADS_PER_STEP
    assert SB_TQ == SB_TK and SB_TK % LANES == 0 and s % SB_TQ == 0 and SB_HEADS % hb == 0
    tri = _sb_tri()
    width = hb * SB_HEAD_DIM
    return pl.pallas_call(
        _sb_kernel,
        grid=(b, SB_HEADS // hb, s // SB_TQ),
        in_specs=[pl.BlockSpec((1, SB_TQ, width), lambda bi, hi, qi: (bi, qi, hi)),
                  pl.BlockSpec((s // SB_TK, width, SB_TK), lambda bi, hi, qi: (bi, hi, 0)),
                  pl.BlockSpec((1, s, width), lambda bi, hi, qi: (bi, 0, hi)),
                  _const_spec(tri.shape)],
        out_specs=pl.BlockSpec((1, SB_TQ, width), lambda bi, hi, qi: (bi, qi, hi)),
        out_shape=jax.ShapeDtypeStruct((b, s, D_MODEL), _BF),
        scratch_shapes=[pltpu.VMEM((hb, SB_TQ, LANES), _F32)] * 2 + [pltpu.VMEM((hb, 8, LANES), _F32)] * 2,
        compiler_params=_params(("parallel", "parallel", "arbitrary")),
        name="sb_attention",
    )(q, kt, v, tri)


def _gla_constants():
    c = GLA_CHUNK
    idx = np.arange(c)
    seg = []
    masks = []
    for lvl in range(GLA_LEVELS):
        s = 1 << lvl
        blk = idx // s
        start = blk * s
        end = start + s - 1
        t = idx[None, :]
        q_side = ((blk % 2 == 1)[:, None] & (t >= start[:, None]) & (t <= idx[:, None]))
        k_side = ((blk % 2 == 0)[:, None] & (t > idx[:, None]) & (t <= end[:, None]))
        seg.append(np.where((blk % 2 == 1)[:, None], q_side, k_side))
        pair = ((idx[:, None] // (2 * s)) == (idx[None, :] // (2 * s))) \
            & ((blk % 2 == 1)[:, None]) & ((blk % 2 == 0)[None, :])
        masks.append(np.tile(pair, (GLA_HEADS, 1)))
    t = idx[None, :]
    full = np.concatenate([t <= idx[:, None], t > idx[:, None]], axis=0)
    seg = np.stack(seg).astype(np.float32)
    masks = np.stack(masks).astype(np.float32)
    head_of_qk = np.arange(GLA_QK) // GLA_DK
    head_of_v = np.arange(GLA_V) // GLA_DV
    head_mask = (np.arange(8)[:, None] == head_of_qk[None, :]).astype(np.float32)
    expand = (head_of_qk[:, None] == head_of_v[None, :]).astype(np.float32)
    return (jnp.asarray(seg, _BF), jnp.asarray(full, _BF), jnp.asarray(masks, _F32), jnp.asarray(head_mask, _F32),
            jnp.asarray(expand, _BF), jnp.asarray(expand.T, _F32))


def _gla_kernel(gq_ref, gk_ref, la_ref, gv_ref, gr_ref, seg_ref, full_ref, mask_ref, hm_ref, exp_ref, bd_ref,
                gn_ref, o_ref, state_ref):
    c = GLA_CHUNK

    @pl.when(pl.program_id(1) == 0)
    def _():
        state_ref[...] = jnp.zeros(state_ref.shape, _F32)

    hm = hm_ref[...].astype(_BF)
    gn = gn_ref[...]

    def intra_chunks(row_slices):
        qs_, ks_, vs_, la2s = [], [], [], []
        for rows in row_slices:
            la = la_ref[0, rows, :]
            la_hi = la.astype(_BF)
            la_lo = (la - la_hi.astype(_F32)).astype(_BF)
            la2s.append(jnp.concatenate([la_hi, la_lo], axis=1))
            qs_.append(gq_ref[0, rows, :])
            ks_.append(gk_ref[0, rows, :])
            vs_.append(gv_ref[0, rows, :])
        segs = [seg_ref[lvl] for lvl in range(GLA_LEVELS)] + [full_ref[...]]
        sums = [[_dot(seg, la2) for seg in segs] for la2 in la2s]
        ws = [[jnp.exp(e[:, :GLA_QK] + e[:, GLA_QK:]) for e in per_chunk] for per_chunk in sums]
        pairs = []
        for q, k, w_chunk in zip(qs_, ks_, ws):
            per_level = []
            for lvl in range(GLA_LEVELS):
                w = w_chunk[lvl]
                ql = (q * w).astype(_BF)
                kl = (k * w).astype(_BF)
                q_heads = jnp.concatenate([ql * hm[hd:hd + 1] for hd in range(GLA_HEADS)], axis=0)
                per_level.append((q_heads, kl))
            pairs.append(per_level)
        scores = [[_dot_nt(q_heads, kl) for q_heads, kl in per_level] for per_level in pairs]
        out = []
        for q, k, v, w_chunk, per_level in zip(qs_, ks_, vs_, ws, scores):
            att = per_level[0] * mask_ref[0]
            for lvl in range(1, GLA_LEVELS):
                att = att + per_level[lvl] * mask_ref[lvl]
            w = w_chunk[GLA_LEVELS]
            o = _dot((q * k).astype(_BF), exp_ref[...]) * v.astype(_F32)
            o_intra = [_dot(att[hd * c:(hd + 1) * c, :].astype(_BF), v[:, hd * GLA_DV:(hd + 1) * GLA_DV])
                       for hd in range(GLA_HEADS)]
            o = o + jnp.concatenate(o_intra, axis=1)
            out.append((o, (q * w[:c]).astype(_BF), (k * w[c:]).astype(_BF), w[c - 1:c], v))
        return out

    chunks = intra_chunks([slice(i * c, (i + 1) * c) for i in range(GLA_CHUNKS_PER_STEP)])
    state = state_ref[...]
    for i, (o, q_state, k_state, decay, v) in enumerate(chunks):
        o = o + _dot_nt(q_state, state.astype(_BF))
        state = state * decay + _dot_tn(v, k_state) * bd_ref[...]
        gr = gr_ref[0, i * c:(i + 1) * c, :].astype(_F32)
        for hd in range(GLA_HEADS):
            sl = slice(hd * GLA_DV, (hd + 1) * GLA_DV)
            o_ref[0, i * c:(i + 1) * c, sl] = (_rms(o[:, sl], gn[:, sl]) * _silu(gr[:, sl])).astype(o_ref.dtype)
    state_ref[...] = state


def _gla(gq, gk, la, gv, gr, gla_norm):
    b, s, _ = gq.shape
    rows = GLA_CHUNK * GLA_CHUNKS_PER_STEP
    assert s % rows == 0 and (1 << GLA_LEVELS) == GLA_CHUNK
    consts = list(_gla_constants()) + [gla_norm]
    tok = lambda wd: pl.BlockSpec((1, rows, wd), lambda bi, ci: (bi, ci, 0))
    return pl.pallas_call(
        _gla_kernel,
        grid=(b, s // rows),
        in_specs=[tok(GLA_QK), tok(GLA_QK), tok(GLA_QK), tok(GLA_V), tok(GLA_V)]
        + [_const_spec(cn.shape) for cn in consts],
        out_specs=tok(GLA_V),
        out_shape=jax.ShapeDtypeStruct((b, s, GLA_V), _BF),
        scratch_shapes=[pltpu.VMEM((GLA_V, GLA_QK), _F32)],
        compiler_params=_params(("parallel", "arbitrary")),
        name="gla_chunked",
    )(gq, gk, la, gv, gr, *consts)


def _rotate_half_cols(w):
    half = w.shape[1] // 2
    return jnp.concatenate([-w[:, half:], w[:, :half]], axis=1)


def _pad_cols(w, width):
    return jnp.pad(w, ((0, 0), (0, width - w.shape[1])))


def _prep_even_weights(w_in, w_uq, w_ukv, w_gate2):
    splits = np.cumsum([MLA_Q_RANK, MLA_KV_RANK, MLA_ROPE, GLA_QK, GLA_QK, GLA_V, GLA_GATE_RANK])
    c_q, c_kv, k_r, g_q, g_k, g_v, g_a, g_r = jnp.split(w_in, splits, axis=1)
    win = jnp.concatenate([c_q, c_kv, _pad_cols(k_r, LANES), _pad_cols(_rotate_half_cols(k_r), LANES),
                           g_q, g_k, g_v, _pad_cols(g_a, LANES), g_r], axis=1).astype(_BF)
    uq = w_uq.reshape(MLA_Q_RANK, MLA_HEADS, MLA_NOPE + MLA_ROPE)
    uq_rope = uq[:, :, MLA_NOPE:]
    rot = jnp.concatenate([-uq_rope[:, :, MLA_ROPE // 2:], uq_rope[:, :, :MLA_ROPE // 2]], axis=2)
    wuq = jnp.pad(uq, ((0, 0), (0, 0), (0, MLA_QK_PAD - uq.shape[2]))).reshape(MLA_Q_RANK, -1).astype(_BF)
    wuqr = jnp.pad(rot, ((0, 0), (0, 0), (0, LANES - MLA_ROPE))).reshape(MLA_Q_RANK, -1).astype(_BF)
    ukv = w_ukv.reshape(MLA_KV_RANK, MLA_HEADS, MLA_NOPE + MLA_V)
    wukt = ukv[:, :, :MLA_NOPE].reshape(MLA_KV_RANK, -1).T.astype(_BF)
    wuv = ukv[:, :, MLA_NOPE:].reshape(MLA_KV_RANK, -1).astype(_BF)
    wg2 = jnp.pad(w_gate2, ((0, LANES - GLA_GATE_RANK), (0, 0))).astype(_BF)
    return dict(win=win, wuq=wuq, wuqr=wuqr, wukt=wukt, wuv=wuv, wg2=wg2)


def _rope_tables(positions):
    half = MLA_ROPE // 2
    inv_freq = 1.0 / (ROPE_THETA ** (jnp.arange(half, dtype=_F32) / half))
    ang = positions.astype(_F32)[..., None] * inv_freq
    cos, sin = jnp.cos(ang), jnp.sin(ang)
    pad = jnp.zeros(cos.shape[:-1] + (LANES - MLA_ROPE,), _F32)
    return (jnp.concatenate([cos, cos, pad], axis=-1).reshape(-1, LANES),
            jnp.concatenate([sin, sin, pad], axis=-1).reshape(-1, LANES))


def kernel(x, p, positions, ffn1_norm, ffn1_w_gu, ffn1_w_down, mix_norm, ffn2_norm, ffn2_w_gu, ffn2_w_down, ple_norm, ple_w_gate, ple_w_proj, ev_w_in, ev_q_norm, ev_kv_norm, ev_w_uq, ev_w_ukv, ev_w_gate2, ev_b_gate, ev_gla_norm, ev_w_out, od_w_qkv, od_w_out, final_norm):
    b, s, d = x.shape
    depth = p.shape[0]
    t = b * s
    cos_t, sin_t = _rope_tables(positions)
    row = lambda a: a.reshape(1, -1).astype(_F32)
    xt = x.reshape(t, d)
    p3 = p.reshape(depth, t, p.shape[-1])
    fin = row(final_norm)
    for i in range(depth):
        j = i // 2
        common = dict(n1=row(ffn1_norm[i]), wgu=ffn1_w_gu[i].astype(_BF), wdn=ffn1_w_down[i].astype(_BF),
                      nm=row(mix_norm[i]), n2=row(ffn2_norm[i]), wgu2=ffn2_w_gu[i].astype(_BF),
                      wdn2=ffn2_w_down[i].astype(_BF), np=row(ple_norm[i]), wpg=ple_w_gate[i].astype(_BF),
                      wpp=ple_w_proj[i].astype(_BF))
        if i % 2 == 0:
            w = dict(common, **_prep_even_weights(ev_w_in[j], ev_w_uq[j], ev_w_ukv[j], ev_w_gate2[j]),
                     qn=row(ev_q_norm[j]), kvn=row(ev_kv_norm[j]), bg=row(ev_b_gate[j]))
            x1, q, kt, v, qmax, gq, gk, la, gv, gr = _pre_even(xt, cos_t, sin_t, w)
            seq = lambda a: a.reshape(b, s, a.shape[-1])
            o_mla = _mla(seq(q), kt, seq(v), qmax)
            o_gla = _gla(seq(gq), seq(gk), seq(la), seq(gv), seq(gr), row(ev_gla_norm[j]))
            mixes = [o_mla.reshape(t, -1), o_gla.reshape(t, -1)]
            w_out = ev_w_out[j].astype(_BF)
            wouts = [w_out[:MLA_HEADS * MLA_V], w_out[MLA_HEADS * MLA_V:]]
        else:
            wqkv = od_w_qkv[j].astype(_BF)
            w = dict(common, wq=wqkv[:, :D_MODEL], wk=wqkv[:, D_MODEL:2 * D_MODEL], wv=wqkv[:, 2 * D_MODEL:])
            x1, q, kt, v = _pre_odd(xt, w)
            seq = lambda a: a.reshape(b, s, a.shape[-1])
            o_sb = _sb(seq(q), kt, seq(v))
            mixes = [o_sb.reshape(t, -1)]
            wouts = [od_w_out[j].astype(_BF)]
        xt = _post(x1, mixes, p3, i, wouts, w, fin, i == depth - 1)
    return xt.reshape(b, s, d)
```

```python
import functools

import numpy as np
import jax
import jax.numpy as jnp
from jax import lax
from jax.experimental import pallas as pl
from jax.experimental.pallas import tpu as pltpu

D_MODEL = 1024
P_DIM = 256
EPS = 1e-6
D_FF = 1408
MLA_HEADS = 4
MLA_Q_RANK = 256
MLA_KV_RANK = 128
MLA_NOPE = 128
MLA_ROPE = 64
MLA_V = 128
ROPE_THETA = 10000.0
GLA_HEADS = 4
GLA_DK = 64
GLA_DV = 128
GLA_GATE_RANK = 16
GLA_GATE_TAU = 16.0
SB_HEADS = 8
SB_HEAD_DIM = D_MODEL // SB_HEADS

LANES = 128
MLA_QK_PAD = 2 * LANES
GLA_QK = GLA_HEADS * GLA_DK
GLA_V = GLA_HEADS * GLA_DV

TOKEN_TILE = 512
TOKEN_ROW_GROUPS = 2
MLA_TQ = 512
MLA_TK = 512
MLA_HEADS_PER_STEP = 4
SB_TQ = 256
SB_TK = 256
SB_HEADS_PER_STEP = 4
SB_EXIT_LOG_WEIGHT = -106.0
SB_BOUND_SLACK = 1.001
GLA_CHUNK = 128
GLA_CHUNKS_PER_STEP = 4
GLA_LEVELS = 7
VMEM_LIMIT = 56 * 1024 * 1024
MASK_VALUE = -1e30
MLA_SAFE_SCORE_BOUND = 60.0
MLA_BOUND_SLACK = 1.001
MLA_QK_REAL = 192
LOG2_E = 1.4426950408889634

_BF = jnp.bfloat16
_F32 = jnp.float32


def _dot(a, b):
    return jnp.dot(a, b, preferred_element_type=_F32)


def _dot_nt(a, b):
    return lax.dot_general(a, b, (((1,), (1,)), ((), ())), preferred_element_type=_F32)


def _dot_tn(a, b):
    return lax.dot_general(a, b, (((0,), (0,)), ((), ())), preferred_element_type=_F32)


def _rms(x, w):
    return x * lax.rsqrt(jnp.mean(x * x, axis=-1, keepdims=True) + EPS) * w


def _sigmoid(x):
    return 1.0 / (1.0 + jnp.exp(-x))


def _silu(x):
    return x * _sigmoid(x)


def _log_sigmoid(x):
    return jnp.minimum(x, 0.0) - jnp.log(1.0 + jnp.exp(-jnp.abs(x)))


def _row_sq_norm_bound(x):
    return _dot(x * x, jnp.ones((x.shape[1], LANES), x.dtype)) * (1.0 + 2.0 ** -8)


def _row_groups(rows):
    step = rows // TOKEN_ROW_GROUPS
    return [slice(i * step, (i + 1) * step) for i in range(TOKEN_ROW_GROUPS)]


def _ffn_groups(xs, norm_w, w_gu, w_down):
    hs = [_rms(x, norm_w).astype(_BF) for x in xs]
    gus = [_dot(h, w_gu) for h in hs]
    acts = [(_silu(gu[:, :D_FF]) * gu[:, D_FF:]).astype(_BF) for gu in gus]
    downs = [_dot(act, w_down) for act in acts]
    return [x + 0.5 * d for x, d in zip(xs, downs)]


def _const_spec(shape):
    nd = len(shape)
    return pl.BlockSpec(shape, lambda *_: (0,) * nd, pipeline_mode=pl.Buffered(1))


def _params(semantics):
    return pltpu.CompilerParams(dimension_semantics=semantics, vmem_limit_bytes=VMEM_LIMIT)


def _pre_even_kernel(x_ref, cos_ref, sin_ref, n1_ref, wgu_ref, wdn_ref, nm_ref, win_ref, qn_ref, kvn_ref,
                     wuq_ref, wuqr_ref, wukt_ref, wuv_ref, wg2_ref, bg_ref,
                     x1_ref, q_ref, kt_ref, v_ref, qmax_ref, gq_ref, gk_ref, la_ref, gv_ref, gr_ref):
    groups = _row_groups(x_ref.shape[0])
    x1s = _ffn_groups([x_ref[g, :] for g in groups], n1_ref[...], wgu_ref[...], wdn_ref[...])
    for g, x1 in zip(groups, x1s):
        x1_ref[g, :] = x1
    hs = [_rms(x1, nm_ref[...]).astype(_BF) for x1 in x1s]
    projs = [_dot(h, win_ref[...]) for h in hs]
    cqs = [_rms(proj[:, 0:256], qn_ref[...]).astype(_BF) for proj in projs]
    ckvs = [_rms(proj[:, 256:384], kvn_ref[...]).astype(_BF) for proj in projs]
    qs = [_dot(cq, wuq_ref[...]) for cq in cqs]
    q_rots = [_dot(cq, wuqr_ref[...]) for cq in cqs]
    k_nope_ts = [_dot_nt(wukt_ref[...], ckv) for ckv in ckvs]
    vs = [_dot(ckv, wuv_ref[...]) for ckv in ckvs]
    gates = [_dot(proj[:, 1664:1792].astype(_BF), wg2_ref[...]) + bg_ref[...] for proj in projs]
    scale = (MLA_NOPE + MLA_ROPE) ** -0.5 * LOG2_E
    q_sq_max = jnp.zeros(qmax_ref.shape[1:], _F32)
    head_lane = lax.broadcasted_iota(jnp.int32, q_sq_max.shape, 1)
    for g, proj, q, q_rot, k_nope_t, v, gate in zip(groups, projs, qs, q_rots, k_nope_ts, vs, gates):
        cos = cos_ref[g, :]
        sin = sin_ref[g, :]
        k_r = proj[:, 384:512]
        k_rr = proj[:, 512:640]
        for hd in range(MLA_HEADS):
            lo = hd * MLA_QK_PAD
            q_nope = q[:, lo:lo + LANES] * scale
            q_rope = (q[:, lo + LANES:lo + 2 * LANES] * cos + q_rot[:, hd * LANES:(hd + 1) * LANES] * sin) * scale
            q_ref[g, lo:lo + LANES] = q_nope.astype(_BF)
            q_ref[g, lo + LANES:lo + 2 * LANES] = q_rope.astype(_BF)
            sq = jnp.sum(q_nope * q_nope + q_rope * q_rope, axis=1, keepdims=True) * (1.0 + 2.0 ** -8) ** 2
            q_sq_max = jnp.where(head_lane == hd, jnp.maximum(q_sq_max, jnp.max(sq, axis=0, keepdims=True)), q_sq_max)
        k_rope_t = (k_r * cos + k_rr * sin).T
        pad_row = lax.broadcasted_iota(jnp.int32, k_rope_t.shape, 0) == MLA_ROPE
        k_rope_t = jnp.where(pad_row, 1.0, k_rope_t).astype(_BF)
        for hd in range(MLA_HEADS):
            lo = hd * MLA_QK_PAD
            kt_ref[0, lo:lo + LANES, g] = k_nope_t[hd * LANES:(hd + 1) * LANES].astype(_BF)
            kt_ref[0, lo + LANES:lo + 2 * LANES, g] = k_rope_t
        v_ref[g, :] = v.astype(_BF)
        gq_ref[g, :] = proj[:, 640:896] * (GLA_DK ** -0.5)
        gk_ref[g, :] = proj[:, 896:1152]
        la_ref[g, :] = _log_sigmoid(gate) * (1.0 / GLA_GATE_TAU)
        gv_ref[g, :] = proj[:, 1152:1664].astype(_BF)
        gr_ref[g, :] = proj[:, 1792:2304].astype(_BF)
    qmax_ref[0] = q_sq_max


def _pre_odd_kernel(x_ref, n1_ref, wgu_ref, wdn_ref, nm_ref, wq_ref, wk_ref, wv_ref,
                    x1_ref, q_ref, kt_ref, v_ref):
    groups = _row_groups(x_ref.shape[0])
    x1s = _ffn_groups([x_ref[g, :] for g in groups], n1_ref[...], wgu_ref[...], wdn_ref[...])
    for g, x1 in zip(groups, x1s):
        x1_ref[g, :] = x1
    hs = [_rms(x1, nm_ref[...]).astype(_BF) for x1 in x1s]
    qs = [_dot(h, wq_ref[...]) for h in hs]
    ks = [_dot(h, wk_ref[...]) for h in hs]
    vs = [_dot(h, wv_ref[...]) for h in hs]
    for j, (g, q, k, v) in enumerate(zip(groups, qs, ks, vs)):
        q_ref[g, :] = (q * (SB_HEAD_DIM ** -0.5)).astype(_BF)
        kt_ref[j] = k.T.astype(_BF)
        v_ref[g, :] = v.astype(_BF)


def _row_spec(width, tm):
    return pl.BlockSpec((tm, width), lambda i: (i, 0))


def _pre_even(x, cos_t, sin_t, w):
    t = x.shape[0]
    tm = min(TOKEN_TILE, t)
    consts = [w["n1"], w["wgu"], w["wdn"], w["nm"], w["win"], w["qn"], w["kvn"], w["wuq"], w["wuqr"],
              w["wukt"], w["wuv"], w["wg2"], w["bg"]]
    assert tm == MLA_TK
    kt_rows = MLA_HEADS * MLA_QK_PAD
    tile_outs = {"kt": ((kt_rows, tm), _BF), "qmax": ((8, LANES), _F32)}
    out_widths = [(D_MODEL, _F32), (MLA_HEADS * MLA_QK_PAD, _BF), "kt",
                  (MLA_HEADS * MLA_V, _BF), "qmax", (GLA_QK, _F32), (GLA_QK, _F32), (GLA_QK, _F32),
                  (GLA_V, _BF), (GLA_V, _BF)]
    out_specs = [pl.BlockSpec((1,) + tile_outs[o][0], lambda i: (i, 0, 0)) if isinstance(o, str)
                 else _row_spec(o[0], tm) for o in out_widths]
    out_shape = [jax.ShapeDtypeStruct((t // tm,) + tile_outs[o][0], tile_outs[o][1]) if isinstance(o, str)
                 else jax.ShapeDtypeStruct((t, o[0]), o[1]) for o in out_widths]
    return pl.pallas_call(
        _pre_even_kernel,
        grid=(t // tm,),
        in_specs=[_row_spec(D_MODEL, tm), _row_spec(LANES, tm), _row_spec(LANES, tm)]
        + [_const_spec(c.shape) for c in consts],
        out_specs=out_specs,
        out_shape=out_shape,
        compiler_params=_params(("parallel",)),
        name="pre_even",
    )(x, cos_t, sin_t, *consts)


def _pre_odd(x, w):
    t = x.shape[0]
    tm = min(TOKEN_TILE, t)
    consts = [w["n1"], w["wgu"], w["wdn"], w["nm"], w["wq"], w["wk"], w["wv"]]
    assert tm == TOKEN_ROW_GROUPS * SB_TK
    out_widths = [(D_MODEL, _F32), (D_MODEL, _BF), None, (D_MODEL, _BF)]
    out_specs = [pl.BlockSpec((tm // SB_TK, D_MODEL, SB_TK), lambda i: (i, 0, 0)) if o is None
                 else _row_spec(o[0], tm) for o in out_widths]
    out_shape = [jax.ShapeDtypeStruct((t // SB_TK, D_MODEL, SB_TK), _BF) if o is None
                 else jax.ShapeDtypeStruct((t, o[0]), o[1]) for o in out_widths]
    return pl.pallas_call(
        _pre_odd_kernel,
        grid=(t // tm,),
        in_specs=[_row_spec(D_MODEL, tm)] + [_const_spec(c.shape) for c in consts],
        out_specs=out_specs,
        out_shape=out_shape,
        compiler_params=_params(("parallel",)),
        name="pre_odd",
    )(x, *consts)


def _post_kernel(n_mix, final, *refs):
    x_ref = refs[0]
    mix_refs = refs[1:1 + n_mix]
    p_ref = refs[1 + n_mix]
    wout_refs = refs[2 + n_mix:2 + 2 * n_mix]
    n2_ref, wgu_ref, wdn_ref, np_ref, wpg_ref, wpp_ref, nf_ref, out_ref = refs[2 + 2 * n_mix:]
    groups = _row_groups(x_ref.shape[0])
    xs = [x_ref[g, :] for g in groups]
    for m_ref, w_ref in zip(mix_refs, wout_refs):
        ds = [_dot(m_ref[g, :], w_ref[...]) for g in groups]
        xs = [x + d for x, d in zip(xs, ds)]
    xs = _ffn_groups(xs, n2_ref[...], wgu_ref[...], wdn_ref[...])
    hs = [_rms(x, np_ref[...]).astype(_BF) for x in xs]
    gates = [_sigmoid(_dot(h, wpg_ref[...])) for h in hs]
    projs = [_dot(p_ref[g, :].astype(_BF), wpp_ref[...]) for g in groups]
    xs = [x + gate * proj for x, gate, proj in zip(xs, gates, projs)]
    if final:
        xs = [_rms(x, nf_ref[...]) for x in xs]
    for g, x in zip(groups, xs):
        out_ref[g, :] = x


def _post(x, mixes, p, layer, wouts, w, final_norm, final):
    t = x.shape[0]
    tm = min(TOKEN_TILE, t)
    consts = list(wouts) + [w["n2"], w["wgu2"], w["wdn2"], w["np"], w["wpg"], w["wpp"], final_norm]
    return pl.pallas_call(
        functools.partial(_post_kernel, len(mixes), final),
        grid=(t // tm,),
        in_specs=[_row_spec(D_MODEL, tm)] + [_row_spec(m.shape[1], tm) for m in mixes]
        + [pl.BlockSpec((None, tm, P_DIM), lambda i: (layer, i, 0))] + [_const_spec(c.shape) for c in consts],
        out_specs=_row_spec(D_MODEL, tm),
        out_shape=jax.ShapeDtypeStruct((t, D_MODEL), _F32),
        compiler_params=_params(("parallel",)),
        name="post_final" if final else "post",
    )(x, *mixes, p, *consts)


def _mla_kernel(q_ref, kt_ref, v_ref, qmax_ref, o_ref, m_ref, l_ref, acc_ref, qs_ref, bound_ref, fast_ref):
    qi = pl.program_id(2)
    heads = range(MLA_HEADS_PER_STEP)
    slot = lambda hd: slice(hd * MLA_QK_PAD, (hd + 1) * MLA_QK_PAD)
    vslot = lambda hd: slice(hd * MLA_V, (hd + 1) * MLA_V)
    reps = MLA_TK // LANES

    @pl.when(qi == 0)
    def _():
        def key_norm(i, mx):
            kc = kt_ref[i].astype(_F32)
            kc = kc * kc
            ssq = [jnp.sum(kc[hd * MLA_QK_PAD:hd * MLA_QK_PAD + MLA_QK_REAL], axis=0, keepdims=True)
                   for hd in heads]
            return tuple(jnp.maximum(m, s) for m, s in zip(mx, ssq))
        mx = lax.fori_loop(0, kt_ref.shape[0], key_norm, tuple(jnp.zeros((1, MLA_TK), _F32) for _ in heads))
        q_max = jnp.max(qmax_ref[...], axis=0)[0:1, :]
        head_lane = lax.broadcasted_iota(jnp.int32, q_max.shape, 1) - pl.program_id(1) * MLA_HEADS_PER_STEP
        worst = None
        for hd in heads:
            q_sq = jnp.max(jnp.where(head_lane == hd, q_max, 0.0), axis=1, keepdims=True)
            k_sq = jnp.max(mx[hd], axis=1, keepdims=True)
            bound = jnp.sqrt(q_sq * k_sq) * MLA_BOUND_SLACK
            bound_ref[hd] = jnp.broadcast_to(bound, bound_ref.shape[1:])
            worst = bound if worst is None else jnp.maximum(worst, bound)
        fast_ref[0] = (jnp.max(worst) <= MLA_SAFE_SCORE_BOUND).astype(jnp.int32)

    fast = fast_ref[0] == 1

    l_ref[...] = jnp.zeros(l_ref.shape, _F32)
    acc_ref[...] = jnp.zeros(acc_ref.shape, _F32)
    n_full = (qi * MLA_TQ) // MLA_TK

    def causal_mask(s, start):
        row = lax.broadcasted_iota(jnp.int32, s.shape, 0) + qi * MLA_TQ
        col = lax.broadcasted_iota(jnp.int32, s.shape, 1) + start
        return jnp.where(col <= row, s, MASK_VALUE)

    def run(step):
        def body(c, carry):
            step(c, False)
            return carry
        lax.fori_loop(0, n_full, body, 0)
        step(n_full, True)

    @pl.when(fast)
    def _():
        for hd in heads:
            lane = lax.broadcasted_iota(jnp.int32, (MLA_TQ, MLA_QK_PAD), 1)
            shift = jnp.broadcast_to(jnp.concatenate([-bound_ref[hd, 0:1, :]] * (MLA_QK_PAD // LANES), axis=1),
                                     (MLA_TQ, MLA_QK_PAD)).astype(_BF)
            qs_ref[hd] = jnp.where(lane == MLA_QK_REAL, shift, q_ref[0, :, slot(hd)])

        def step(c, masked):
            start = pl.multiple_of(c * MLA_TK, MLA_TK)
            for hd in heads:
                s = _dot(qs_ref[hd], kt_ref[c, slot(hd), :])
                if masked:
                    s = causal_mask(s, start)
                p = jnp.exp2(s)
                part = p[:, :LANES]
                for r in range(1, reps):
                    part = part + p[:, r * LANES:(r + 1) * LANES]
                l_ref[hd] += part
                acc_ref[hd] += _dot(p.astype(_BF), v_ref[0, pl.ds(start, MLA_TK), vslot(hd)])

        run(step)
        for hd in heads:
            denom = jnp.sum(l_ref[hd], axis=1, keepdims=True)
            o_ref[0, :, vslot(hd)] = (acc_ref[hd] / denom).astype(o_ref.dtype)

    @pl.when(jnp.logical_not(fast))
    def _():
        m_ref[...] = jnp.full(m_ref.shape, MASK_VALUE, _F32)

        def step(c, masked):
            start = pl.multiple_of(c * MLA_TK, MLA_TK)
            for hd in heads:
                s = _dot(q_ref[0, :, slot(hd)], kt_ref[c, slot(hd), :])
                if masked:
                    s = causal_mask(s, start)
                m_prev = m_ref[hd]
                m_next = jnp.maximum(m_prev, jnp.max(s, axis=1, keepdims=True))
                p = jnp.exp2(s - jnp.concatenate([m_next] * reps, axis=1))
                alpha = jnp.exp2(m_prev - m_next)
                l_ref[hd] = alpha * l_ref[hd] + jnp.sum(p, axis=1, keepdims=True)
                acc_ref[hd] = alpha * acc_ref[hd] + _dot(p.astype(_BF), v_ref[0, pl.ds(start, MLA_TK), vslot(hd)])
                m_ref[hd] = m_next

        run(step)
        for hd in heads:
            o_ref[0, :, vslot(hd)] = (acc_ref[hd] / l_ref[hd]).astype(o_ref.dtype)


def _mla(q, kt, v, qmax):
    b, s, _ = q.shape
    hb = MLA_HEADS_PER_STEP
    n_chunks = s // MLA_TK
    assert MLA_TK % MLA_TQ == 0 and s % MLA_TK == 0 and MLA_HEADS % hb == 0
    resident = dict(pipeline_mode=pl.Buffered(1))
    return pl.pallas_call(
        _mla_kernel,
        grid=(b, MLA_HEADS // hb, s // MLA_TQ),
        in_specs=[pl.BlockSpec((1, MLA_TQ, hb * MLA_QK_PAD), lambda bi, hi, qi: (bi, qi, hi)),
                  pl.BlockSpec((n_chunks, hb * MLA_QK_PAD, MLA_TK), lambda bi, hi, qi: (bi, hi, 0), **resident),
                  pl.BlockSpec((1, s, hb * MLA_V), lambda bi, hi, qi: (bi, 0, hi), **resident),
                  pl.BlockSpec((qmax.shape[0] // b, 8, LANES), lambda bi, hi, qi: (bi, 0, 0))],
        out_specs=pl.BlockSpec((1, MLA_TQ, hb * MLA_V), lambda bi, hi, qi: (bi, qi, hi)),
        out_shape=jax.ShapeDtypeStruct((b, s, MLA_HEADS * MLA_V), _BF),
        scratch_shapes=[pltpu.VMEM((hb, MLA_TQ, LANES), _F32)] * 3
        + [pltpu.VMEM((hb, MLA_TQ, MLA_QK_PAD), _BF), pltpu.VMEM((hb, 8, LANES), _F32),
           pltpu.SMEM((1,), jnp.int32)],
        compiler_params=_params(("parallel", "parallel", "arbitrary")),
        name="mla_attention",
    )(q, kt, v, qmax)


def _sb_kernel(q_ref, kt_ref, v_ref, tri_ref, o_ref, suf_ref, acc_ref, zb_ref, kmax_ref):
    qi = pl.program_id(2)
    n_key_chunks = kt_ref.shape[0]
    heads = list(range(SB_HEADS_PER_STEP))
    lanes_of = lambda hd: slice(hd * SB_HEAD_DIM, (hd + 1) * SB_HEAD_DIM)

    @pl.when(qi == 0)
    def _():
        def key_norm(i, mx):
            kc = kt_ref[i].astype(_F32)
            kc = kc * kc
            ssq = [jnp.sum(kc[lanes_of(hd)], axis=0, keepdims=True) for hd in heads]
            return tuple(jnp.maximum(m, s) for m, s in zip(mx, ssq))
        mx = lax.fori_loop(0, n_key_chunks, key_norm, tuple(jnp.zeros((1, SB_TK), _F32) for _ in heads))
        for hd in heads:
            kmax_ref[hd] = jnp.broadcast_to(jnp.max(mx[hd], axis=1, keepdims=True), kmax_ref.shape[1:])

    tri = tri_ref[...]

    def chunk_group(items, sufs):
        halves = list(reversed(range(SB_TK // LANES)))
        zs, ts, masks = [], [], []
        for hd, c, masked in items:
            z = _dot(q_ref[0, :, lanes_of(hd)], kt_ref[c, lanes_of(hd), :])
            zs.append(z)
        for z, (hd, c, masked) in zip(zs, items):
            t = jnp.maximum(z, 0.0) + jnp.log(1.0 + jnp.exp(-jnp.abs(z)))
            causal = None
            if masked:
                causal = (lax.broadcasted_iota(jnp.int32, z.shape, 1)
                          < lax.broadcasted_iota(jnp.int32, z.shape, 0))
                t = jnp.where(causal, t, 0.0)
            ts.append(t)
            masks.append(causal)
        sums = []
        for t in ts:
            per_half = {}
            for half in halves:
                th = t[:, half * LANES:(half + 1) * LANES]
                t_hi = th.astype(_BF)
                t_lo = (th - t_hi.astype(_F32)).astype(_BF)
                per_half[half] = _dot(jnp.concatenate([t_hi, t_lo], axis=1), tri)
            sums.append(per_half)
        outs = []
        for z, per_half, causal, (hd, c, masked) in zip(zs, sums, masks, items):
            suf = sufs[hd]
            log_w = {}
            for half in halves:
                log_w[half] = z[:, half * LANES:(half + 1) * LANES] - per_half[half][:, :LANES] - suf
                suf = suf + per_half[half][:, LANES:]
            sufs[hd] = suf
            a = jnp.exp(jnp.concatenate([log_w[h] for h in sorted(halves)], axis=1))
            if masked:
                a = jnp.where(causal, a, 0.0)
            start = pl.multiple_of(c * SB_TK, SB_TK)
            outs.append(_dot(a.astype(_BF), v_ref[0, pl.ds(start, SB_TK), lanes_of(hd)]))
        return outs

    def still_live(sufs):
        slack = zb_ref[0, 0:1, :] - sufs[0]
        for hd in heads[1:]:
            slack = jnp.maximum(slack, zb_ref[hd, 0:1, :] - sufs[hd])
        return (jnp.max(slack) >= SB_EXIT_LOG_WEIGHT).astype(jnp.int32)

    zero_suf = lambda: {hd: jnp.zeros((SB_TQ, LANES), _F32) for hd in heads}

    @pl.when(qi == 0)
    def _():
        outs = chunk_group([(hd, 0, True) for hd in heads], zero_suf())
        for hd in heads:
            o_ref[0, :, lanes_of(hd)] = outs[hd].astype(o_ref.dtype)

    @pl.when(qi > 0)
    def _():
        for hd in heads:
            q_sq = jnp.max(_row_sq_norm_bound(q_ref[0, :, lanes_of(hd)]), axis=0, keepdims=True)
            zb_ref[hd] = jnp.broadcast_to(jnp.sqrt(q_sq * kmax_ref[hd, 0:1, :]) * SB_BOUND_SLACK,
                                          zb_ref.shape[1:])
        sufs = zero_suf()
        items = [(hd, qi, True) for hd in heads] + [(hd, qi - 1, False) for hd in heads]
        outs = chunk_group(items, sufs)
        for hd in heads:
            acc_ref[hd] = outs[hd] + outs[len(heads) + hd]
            suf_ref[hd] = sufs[hd]

        def cond(carry):
            c, live = carry
            return jnp.logical_and(c >= 0, live > 0)

        def body(carry):
            c, _ = carry
            sufs = {hd: suf_ref[hd] for hd in heads}
            outs = chunk_group([(hd, c, False) for hd in heads], sufs)
            for hd in heads:
                acc_ref[hd] += outs[hd]
                suf_ref[hd] = sufs[hd]
            return c - 1, still_live(sufs)

        lax.while_loop(cond, body, (qi - 2, still_live(sufs)))
        for hd in heads:
            o_ref[0, :, lanes_of(hd)] = acc_ref[hd].astype(o_ref.dtype)


def _sb_tri():
    r = np.arange(LANES)
    upper = (r[:, None] >= r[None, :]).astype(np.float32)
    half = np.concatenate([upper, np.ones((LANES, LANES), np.float32)], axis=1)
    return jnp.asarray(np.concatenate([half, half], axis=0), dtype=_BF)


def _sb(q, kt, v):
    b, s, _ = q.shape
    hb = SB_HEADS_PER_STEP
    assert SB_TQ == SB_TK and SB_TK % LANES == 0 and s % SB_TQ == 0 and SB_HEADS % hb == 0
    tri = _sb_tri()
    width = hb * SB_HEAD_DIM
    return pl.pallas_call(
        _sb_kernel,
        grid=(b, SB_HEADS // hb, s // SB_TQ),
        in_specs=[pl.BlockSpec((1, SB_TQ, width), lambda bi, hi, qi: (bi, qi, hi)),
                  pl.BlockSpec((s // SB_TK, width, SB_TK), lambda bi, hi, qi: (bi, hi, 0)),
                  pl.BlockSpec((1, s, width), lambda bi, hi, qi: (bi, 0, hi)),
                  _const_spec(tri.shape)],
        out_specs=pl.BlockSpec((1, SB_TQ, width), lambda bi, hi, qi: (bi, qi, hi)),
        out_shape=jax.ShapeDtypeStruct((b, s, D_MODEL), _BF),
        scratch_shapes=[pltpu.VMEM((hb, SB_TQ, LANES), _F32)] * 2 + [pltpu.VMEM((hb, 8, LANES), _F32)] * 2,
        compiler_params=_params(("parallel", "parallel", "arbitrary")),
        name="sb_attention",
    )(q, kt, v, tri)


def _gla_constants():
    c = GLA_CHUNK
    idx = np.arange(c)
    seg = []
    masks = []
    for lvl in range(GLA_LEVELS):
        s = 1 << lvl
        blk = idx // s
        start = blk * s
        end = start + s - 1
        t = idx[None, :]
        q_side = ((blk % 2 == 1)[:, None] & (t >= start[:, None]) & (t <= idx[:, None]))
        k_side = ((blk % 2 == 0)[:, None] & (t > idx[:, None]) & (t <= end[:, None]))
        seg.append(np.where((blk % 2 == 1)[:, None], q_side, k_side))
        pair = ((idx[:, None] // (2 * s)) == (idx[None, :] // (2 * s))) \
            & ((blk % 2 == 1)[:, None]) & ((blk % 2 == 0)[None, :])
        masks.append(np.tile(pair, (1, GLA_HEADS)))
    t = idx[None, :]
    full = np.concatenate([t <= idx[:, None], t > idx[:, None]], axis=0)
    seg = np.stack(seg).astype(np.float32)
    masks = np.stack(masks).astype(np.float32)
    head_of_qk = np.arange(GLA_QK) // GLA_DK
    head_of_v = np.arange(GLA_V) // GLA_DV
    head_mask = np.repeat((head_of_qk[None, :, None] == np.arange(GLA_HEADS)[:, None, None]), c, axis=2) \
        .astype(np.float32)
    expand = (head_of_qk[:, None] == head_of_v[None, :]).astype(np.float32)
    return (jnp.asarray(seg, _BF), jnp.asarray(full, _BF), jnp.asarray(masks, _F32), jnp.asarray(head_mask, _BF),
            jnp.asarray(expand, _BF), jnp.asarray(expand.T, _F32))


def _gla_kernel(gq_ref, gk_ref, la_ref, gv_ref, gr_ref, seg_ref, full_ref, mask_ref, hm_ref, exp_ref, bd_ref,
                gn_ref, o_ref, state_ref):
    c = GLA_CHUNK

    @pl.when(pl.program_id(1) == 0)
    def _():
        state_ref[...] = jnp.zeros(state_ref.shape, _F32)

    gn = gn_ref[...]

    def intra_chunks(row_slices):
        qs_, ks_, vs_, la2s = [], [], [], []
        for rows in row_slices:
            la = la_ref[0, rows, :]
            la_hi = la.astype(_BF)
            la_lo = (la - la_hi.astype(_F32)).astype(_BF)
            la2s.append(jnp.concatenate([la_hi, la_lo], axis=1))
            qs_.append(gq_ref[0, rows, :])
            ks_.append(gk_ref[0, rows, :])
            vs_.append(gv_ref[0, rows, :])
        segs = [seg_ref[lvl] for lvl in range(GLA_LEVELS)] + [full_ref[...]]
        sums = [[_dot(seg, la2) for seg in segs] for la2 in la2s]
        ws = [[jnp.exp(e[:, :GLA_QK] + e[:, GLA_QK:]) for e in per_chunk] for per_chunk in sums]
        pairs = []
        for q, k, w_chunk in zip(qs_, ks_, ws):
            per_level = []
            for lvl in range(GLA_LEVELS):
                w = w_chunk[lvl]
                ql = (q * w).astype(_BF)
                kl_t = (k * w).T.astype(_BF)
                k_heads = jnp.concatenate([kl_t * hm_ref[hd] for hd in range(GLA_HEADS)], axis=1)
                per_level.append((ql, k_heads))
            pairs.append(per_level)
        scores = [[_dot(ql, k_heads) for ql, k_heads in per_level] for per_level in pairs]
        out = []
        for q, k, v, w_chunk, per_level in zip(qs_, ks_, vs_, ws, scores):
            att = per_level[0] * mask_ref[0]
            for lvl in range(1, GLA_LEVELS):
                att = att + per_level[lvl] * mask_ref[lvl]
            w = w_chunk[GLA_LEVELS]
            o = _dot((q * k).astype(_BF), exp_ref[...]) * v.astype(_F32)
            o_intra = [_dot(att[:, hd * c:(hd + 1) * c].astype(_BF), v[:, hd * GLA_DV:(hd + 1) * GLA_DV])
                       for hd in range(GLA_HEADS)]
            o = o + jnp.concatenate(o_intra, axis=1)
            out.append((o, (q * w[:c]).astype(_BF), (k * w[c:]).astype(_BF), w[c - 1:c], v))
        return out

    chunks = intra_chunks([slice(i * c, (i + 1) * c) for i in range(GLA_CHUNKS_PER_STEP)])
    state = state_ref[...]
    for i, (o, q_state, k_state, decay, v) in enumerate(chunks):
        o = o + _dot_nt(q_state, state.astype(_BF))
        state = state * decay + _dot_tn(v, k_state) * bd_ref[...]
        gr = gr_ref[0, i * c:(i + 1) * c, :].astype(_F32)
        for hd in range(GLA_HEADS):
            sl = slice(hd * GLA_DV, (hd + 1) * GLA_DV)
            o_ref[0, i * c:(i + 1) * c, sl] = (_rms(o[:, sl], gn[:, sl]) * _silu(gr[:, sl])).astype(o_ref.dtype)
    state_ref[...] = state


def _gla(gq, gk, la, gv, gr, gla_norm):
    b, s, _ = gq.shape
    rows = GLA_CHUNK * GLA_CHUNKS_PER_STEP
    assert s % rows == 0 and (1 << GLA_LEVELS) == GLA_CHUNK
    consts = list(_gla_constants()) + [gla_norm]
    tok = lambda wd: pl.BlockSpec((1, rows, wd), lambda bi, ci: (bi, ci, 0))
    return pl.pallas_call(
        _gla_kernel,
        grid=(b, s // rows),
        in_specs=[tok(GLA_QK), tok(GLA_QK), tok(GLA_QK), tok(GLA_V), tok(GLA_V)]
        + [_const_spec(cn.shape) for cn in consts],
        out_specs=tok(GLA_V),
        out_shape=jax.ShapeDtypeStruct((b, s, GLA_V), _BF),
        scratch_shapes=[pltpu.VMEM((GLA_V, GLA_QK), _F32)],
        compiler_params=_params(("parallel", "arbitrary")),
        name="gla_chunked",
    )(gq, gk, la, gv, gr, *consts)


def _rotate_half_cols(w):
    half = w.shape[1] // 2
    return jnp.concatenate([-w[:, half:], w[:, :half]], axis=1)


def _pad_cols(w, width):
    return jnp.pad(w, ((0, 0), (0, width - w.shape[1])))


def _prep_even_weights(w_in, w_uq, w_ukv, w_gate2):
    splits = np.cumsum([MLA_Q_RANK, MLA_KV_RANK, MLA_ROPE, GLA_QK, GLA_QK, GLA_V, GLA_GATE_RANK])
    c_q, c_kv, k_r, g_q, g_k, g_v, g_a, g_r = jnp.split(w_in, splits, axis=1)
    win = jnp.concatenate([c_q, c_kv, _pad_cols(k_r, LANES), _pad_cols(_rotate_half_cols(k_r), LANES),
                           g_q, g_k, g_v, _pad_cols(g_a, LANES), g_r], axis=1).astype(_BF)
    uq = w_uq.reshape(MLA_Q_RANK, MLA_HEADS, MLA_NOPE + MLA_ROPE)
    uq_rope = uq[:, :, MLA_NOPE:]
    rot = jnp.concatenate([-uq_rope[:, :, MLA_ROPE // 2:], uq_rope[:, :, :MLA_ROPE // 2]], axis=2)
    wuq = jnp.pad(uq, ((0, 0), (0, 0), (0, MLA_QK_PAD - uq.shape[2]))).reshape(MLA_Q_RANK, -1).astype(_BF)
    wuqr = jnp.pad(rot, ((0, 0), (0, 0), (0, LANES - MLA_ROPE))).reshape(MLA_Q_RANK, -1).astype(_BF)
    ukv = w_ukv.reshape(MLA_KV_RANK, MLA_HEADS, MLA_NOPE + MLA_V)
    wukt = ukv[:, :, :MLA_NOPE].reshape(MLA_KV_RANK, -1).T.astype(_BF)
    wuv = ukv[:, :, MLA_NOPE:].reshape(MLA_KV_RANK, -1).astype(_BF)
    wg2 = jnp.pad(w_gate2, ((0, LANES - GLA_GATE_RANK), (0, 0))).astype(_BF)
    return dict(win=win, wuq=wuq, wuqr=wuqr, wukt=wukt, wuv=wuv, wg2=wg2)


def _rope_tables(positions):
    half = MLA_ROPE // 2
    inv_freq = 1.0 / (ROPE_THETA ** (jnp.arange(half, dtype=_F32) / half))
    ang = positions.astype(_F32)[..., None] * inv_freq
    cos, sin = jnp.cos(ang), jnp.sin(ang)
    pad = jnp.zeros(cos.shape[:-1] + (LANES - MLA_ROPE,), _F32)
    return (jnp.concatenate([cos, cos, pad], axis=-1).reshape(-1, LANES),
            jnp.concatenate([sin, sin, pad], axis=-1).reshape(-1, LANES))


def kernel(x, p, positions, ffn1_norm, ffn1_w_gu, ffn1_w_down, mix_norm, ffn2_norm, ffn2_w_gu, ffn2_w_down, ple_norm, ple_w_gate, ple_w_proj, ev_w_in, ev_q_norm, ev_kv_norm, ev_w_uq, ev_w_ukv, ev_w_gate2, ev_b_gate, ev_gla_norm, ev_w_out, od_w_qkv, od_w_out, final_norm):
    b, s, d = x.shape
    depth = p.shape[0]
    t = b * s
    cos_t, sin_t = _rope_tables(positions)
    row = lambda a: a.reshape(1, -1).astype(_F32)
    xt = x.reshape(t, d)
    p3 = p.reshape(depth, t, p.shape[-1])
    fin = row(final_norm)
    for i in range(depth):
        j = i // 2
        common = dict(n1=row(ffn1_norm[i]), wgu=ffn1_w_gu[i].astype(_BF), wdn=ffn1_w_down[i].astype(_BF),
                      nm=row(mix_norm[i]), n2=row(ffn2_norm[i]), wgu2=ffn2_w_gu[i].astype(_BF),
                      wdn2=ffn2_w_down[i].astype(_BF), np=row(ple_norm[i]), wpg=ple_w_gate[i].astype(_BF),
                      wpp=ple_w_proj[i].astype(_BF))
        if i % 2 == 0:
            w = dict(common, **_prep_even_weights(ev_w_in[j], ev_w_uq[j], ev_w_ukv[j], ev_w_gate2[j]),
                     qn=row(ev_q_norm[j]), kvn=row(ev_kv_norm[j]), bg=row(ev_b_gate[j]))
            x1, q, kt, v, qmax, gq, gk, la, gv, gr = _pre_even(xt, cos_t, sin_t, w)
            seq = lambda a: a.reshape(b, s, a.shape[-1])
            o_mla = _mla(seq(q), kt, seq(v), qmax)
            o_gla = _gla(seq(gq), seq(gk), seq(la), seq(gv), seq(gr), row(ev_gla_norm[j]))
            mixes = [o_mla.reshape(t, -1), o_gla.reshape(t, -1)]
            w_out = ev_w_out[j].astype(_BF)
            wouts = [w_out[:MLA_HEADS * MLA_V], w_out[MLA_HEADS * MLA_V:]]
        else:
            wqkv = od_w_qkv[j].astype(_BF)
            w = dict(common, wq=wqkv[:, :D_MODEL], wk=wqkv[:, D_MODEL:2 * D_MODEL], wv=wqkv[:, 2 * D_MODEL:])
            x1, q, kt, v = _pre_odd(xt, w)
            seq = lambda a: a.reshape(b, s, a.shape[-1])
            o_sb = _sb(seq(q), kt, seq(v))
            mixes = [o_sb.reshape(t, -1)]
            wouts = [od_w_out[j].astype(_BF)]
        xt = _post(x1, mixes, p3, i, wouts, w, fin, i == depth - 1)
    return xt.reshape(b, s, d)
```

```python
import functools

import numpy as np
import jax
import jax.numpy as jnp
from jax import lax
from jax.experimental import pallas as pl
from jax.experimental.pallas import tpu as pltpu

D_MODEL = 1024
P_DIM = 256
EPS = 1e-6
D_FF = 1408
MLA_HEADS = 4
MLA_Q_RANK = 256
MLA_KV_RANK = 128
MLA_NOPE = 128
MLA_ROPE = 64
MLA_V = 128
ROPE_THETA = 10000.0
GLA_HEADS = 4
GLA_DK = 64
GLA_DV = 128
GLA_GATE_RANK = 16
GLA_GATE_TAU = 16.0
SB_HEADS = 8
SB_HEAD_DIM = D_MODEL // SB_HEADS

LANES = 128
MLA_QK_PAD = 2 * LANES
GLA_QK = GLA_HEADS * GLA_DK
GLA_V = GLA_HEADS * GLA_DV

TOKEN_TILE = 512
TOKEN_ROW_GROUPS = 2
MLA_TQ = 512
MLA_TK = 512
MLA_HEADS_PER_STEP = 4
SB_TQ = 256
SB_TK = 256
SB_HEADS_PER_STEP = 4
SB_EXIT_LOG_WEIGHT = -106.0
SB_BOUND_SLACK = 1.001
GLA_CHUNK = 128
GLA_CHUNKS_PER_STEP = 4
GLA_LEVELS = 7
VMEM_LIMIT = 56 * 1024 * 1024
MASK_VALUE = -1e30
MLA_SAFE_SCORE_BOUND = 60.0
MLA_BOUND_SLACK = 1.001
MLA_QK_REAL = 192
LOG2_E = 1.4426950408889634

_BF = jnp.bfloat16
_F32 = jnp.float32


def _dot(a, b):
    return jnp.dot(a, b, preferred_element_type=_F32)


def _dot_nt(a, b):
    return lax.dot_general(a, b, (((1,), (1,)), ((), ())), preferred_element_type=_F32)


def _dot_tn(a, b):
    return lax.dot_general(a, b, (((0,), (0,)), ((), ())), preferred_element_type=_F32)


def _rms(x, w):
    return x * lax.rsqrt(jnp.mean(x * x, axis=-1, keepdims=True) + EPS) * w


def _sigmoid(x):
    return 1.0 / (1.0 + jnp.exp(-x))


def _silu(x):
    return x * _sigmoid(x)


def _log_sigmoid(x):
    return jnp.minimum(x, 0.0) - jnp.log(1.0 + jnp.exp(-jnp.abs(x)))


def _row_sq_norm_bound(x):
    return _dot(x * x, jnp.ones((x.shape[1], LANES), x.dtype)) * (1.0 + 2.0 ** -8)


def _row_groups(rows):
    step = rows // TOKEN_ROW_GROUPS
    return [slice(i * step, (i + 1) * step) for i in range(TOKEN_ROW_GROUPS)]


def _ffn_groups(xs, norm_w, w_gu, w_down):
    hs = [_rms(x, norm_w).astype(_BF) for x in xs]
    gus = [_dot(h, w_gu) for h in hs]
    acts = [(_silu(gu[:, :D_FF]) * gu[:, D_FF:]).astype(_BF) for gu in gus]
    downs = [_dot(act, w_down) for act in acts]
    return [x + 0.5 * d for x, d in zip(xs, downs)]


def _const_spec(shape):
    nd = len(shape)
    return pl.BlockSpec(shape, lambda *_: (0,) * nd, pipeline_mode=pl.Buffered(1))


def _params(semantics):
    return pltpu.CompilerParams(dimension_semantics=semantics, vmem_limit_bytes=VMEM_LIMIT)


def _pre_even_kernel(x_ref, cos_ref, sin_ref, n1_ref, wgu_ref, wdn_ref, nm_ref, win_ref, qn_ref, kvn_ref,
                     wuq_ref, wuqr_ref, wukt_ref, wuv_ref, wg2_ref, bg_ref,
                     x1_ref, q_ref, kt_ref, v_ref, qmax_ref, gq_ref, gk_ref, la_ref, gv_ref, gr_ref):
    groups = _row_groups(x_ref.shape[0])
    x1s = _ffn_groups([x_ref[g, :] for g in groups], n1_ref[...], wgu_ref[...], wdn_ref[...])
    for g, x1 in zip(groups, x1s):
        x1_ref[g, :] = x1
    hs = [_rms(x1, nm_ref[...]).astype(_BF) for x1 in x1s]
    projs = [_dot(h, win_ref[...]) for h in hs]
    cqs = [_rms(proj[:, 0:256], qn_ref[...]).astype(_BF) for proj in projs]
    ckvs = [_rms(proj[:, 256:384], kvn_ref[...]).astype(_BF) for proj in projs]
    qs = [_dot(cq, wuq_ref[...]) for cq in cqs]
    q_rots = [_dot(cq, wuqr_ref[...]) for cq in cqs]
    k_nope_ts = [_dot_nt(wukt_ref[...], ckv) for ckv in ckvs]
    vs = [_dot(ckv, wuv_ref[...]) for ckv in ckvs]
    gates = [_dot(proj[:, 1664:1792].astype(_BF), wg2_ref[...]) + bg_ref[...] for proj in projs]
    scale = (MLA_NOPE + MLA_ROPE) ** -0.5 * LOG2_E
    q_sq_max = jnp.zeros(qmax_ref.shape[1:], _F32)
    head_lane = lax.broadcasted_iota(jnp.int32, q_sq_max.shape, 1)
    for g, proj, q, q_rot, k_nope_t, v, gate in zip(groups, projs, qs, q_rots, k_nope_ts, vs, gates):
        cos = cos_ref[g, :]
        sin = sin_ref[g, :]
        k_r = proj[:, 384:512]
        k_rr = proj[:, 512:640]
        for hd in range(MLA_HEADS):
            lo = hd * MLA_QK_PAD
            q_nope = q[:, lo:lo + LANES] * scale
            q_rope = (q[:, lo + LANES:lo + 2 * LANES] * cos + q_rot[:, hd * LANES:(hd + 1) * LANES] * sin) * scale
            q_ref[g, lo:lo + LANES] = q_nope.astype(_BF)
            q_ref[g, lo + LANES:lo + 2 * LANES] = q_rope.astype(_BF)
            sq = jnp.sum(q_nope * q_nope + q_rope * q_rope, axis=1, keepdims=True) * (1.0 + 2.0 ** -8) ** 2
            q_sq_max = jnp.where(head_lane == hd, jnp.maximum(q_sq_max, jnp.max(sq, axis=0, keepdims=True)), q_sq_max)
        k_rope_t = (k_r * cos + k_rr * sin).T
        pad_row = lax.broadcasted_iota(jnp.int32, k_rope_t.shape, 0) == MLA_ROPE
        k_rope_t = jnp.where(pad_row, 1.0, k_rope_t).astype(_BF)
        for hd in range(MLA_HEADS):
            lo = hd * MLA_QK_PAD
            kt_ref[0, lo:lo + LANES, g] = k_nope_t[hd * LANES:(hd + 1) * LANES].astype(_BF)
            kt_ref[0, lo + LANES:lo + 2 * LANES, g] = k_rope_t
        v_ref[g, :] = v.astype(_BF)
        gq_ref[g, :] = proj[:, 640:896] * (GLA_DK ** -0.5)
        gk_ref[g, :] = proj[:, 896:1152]
        la_ref[g, :] = _log_sigmoid(gate) * (1.0 / GLA_GATE_TAU)
        gv_ref[g, :] = proj[:, 1152:1664].astype(_BF)
        gr_ref[g, :] = proj[:, 1792:2304].astype(_BF)
    qmax_ref[0] = q_sq_max


def _pre_odd_kernel(x_ref, n1_ref, wgu_ref, wdn_ref, nm_ref, wq_ref, wk_ref, wv_ref,
                    x1_ref, q_ref, kt_ref, v_ref):
    groups = _row_groups(x_ref.shape[0])
    x1s = _ffn_groups([x_ref[g, :] for g in groups], n1_ref[...], wgu_ref[...], wdn_ref[...])
    for g, x1 in zip(groups, x1s):
        x1_ref[g, :] = x1
    hs = [_rms(x1, nm_ref[...]).astype(_BF) for x1 in x1s]
    qs = [_dot(h, wq_ref[...]) for h in hs]
    ks = [_dot(h, wk_ref[...]) for h in hs]
    vs = [_dot(h, wv_ref[...]) for h in hs]
    for j, (g, q, k, v) in enumerate(zip(groups, qs, ks, vs)):
        q_ref[g, :] = (q * (SB_HEAD_DIM ** -0.5)).astype(_BF)
        kt_ref[j] = k.T.astype(_BF)
        v_ref[g, :] = v.astype(_BF)


def _row_spec(width, tm):
    return pl.BlockSpec((tm, width), lambda i: (i, 0))


def _pre_even(x, cos_t, sin_t, w):
    t = x.shape[0]
    tm = min(TOKEN_TILE, t)
    consts = [w["n1"], w["wgu"], w["wdn"], w["nm"], w["win"], w["qn"], w["kvn"], w["wuq"], w["wuqr"],
              w["wukt"], w["wuv"], w["wg2"], w["bg"]]
    assert tm == MLA_TK
    kt_rows = MLA_HEADS * MLA_QK_PAD
    tile_outs = {"kt": ((kt_rows, tm), _BF), "qmax": ((8, LANES), _F32)}
    out_widths = [(D_MODEL, _F32), (MLA_HEADS * MLA_QK_PAD, _BF), "kt",
                  (MLA_HEADS * MLA_V, _BF), "qmax", (GLA_QK, _F32), (GLA_QK, _F32), (GLA_QK, _F32),
                  (GLA_V, _BF), (GLA_V, _BF)]
    out_specs = [pl.BlockSpec((1,) + tile_outs[o][0], lambda i: (i, 0, 0)) if isinstance(o, str)
                 else _row_spec(o[0], tm) for o in out_widths]
    out_shape = [jax.ShapeDtypeStruct((t // tm,) + tile_outs[o][0], tile_outs[o][1]) if isinstance(o, str)
                 else jax.ShapeDtypeStruct((t, o[0]), o[1]) for o in out_widths]
    return pl.pallas_call(
        _pre_even_kernel,
        grid=(t // tm,),
        in_specs=[_row_spec(D_MODEL, tm), _row_spec(LANES, tm), _row_spec(LANES, tm)]
        + [_const_spec(c.shape) for c in consts],
        out_specs=out_specs,
        out_shape=out_shape,
        compiler_params=_params(("parallel",)),
        name="pre_even",
    )(x, cos_t, sin_t, *consts)


def _pre_odd(x, w):
    t = x.shape[0]
    tm = min(TOKEN_TILE, t)
    consts = [w["n1"], w["wgu"], w["wdn"], w["nm"], w["wq"], w["wk"], w["wv"]]
    assert tm == TOKEN_ROW_GROUPS * SB_TK
    out_widths = [(D_MODEL, _F32), (D_MODEL, _BF), None, (D_MODEL, _BF)]
    out_specs = [pl.BlockSpec((tm // SB_TK, D_MODEL, SB_TK), lambda i: (i, 0, 0)) if o is None
                 else _row_spec(o[0], tm) for o in out_widths]
    out_shape = [jax.ShapeDtypeStruct((t // SB_TK, D_MODEL, SB_TK), _BF) if o is None
                 else jax.ShapeDtypeStruct((t, o[0]), o[1]) for o in out_widths]
    return pl.pallas_call(
        _pre_odd_kernel,
        grid=(t // tm,),
        in_specs=[_row_spec(D_MODEL, tm)] + [_const_spec(c.shape) for c in consts],
        out_specs=out_specs,
        out_shape=out_shape,
        compiler_params=_params(("parallel",)),
        name="pre_odd",
    )(x, *consts)


def _post_kernel(n_mix, final, *refs):
    x_ref = refs[0]
    mix_refs = refs[1:1 + n_mix]
    p_ref = refs[1 + n_mix]
    wout_refs = refs[2 + n_mix:2 + 2 * n_mix]
    n2_ref, wgu_ref, wdn_ref, np_ref, wpg_ref, wpp_ref, nf_ref, out_ref = refs[2 + 2 * n_mix:]
    groups = _row_groups(x_ref.shape[0])
    xs = [x_ref[g, :] for g in groups]
    for m_ref, w_ref in zip(mix_refs, wout_refs):
        ds = [_dot(m_ref[g, :], w_ref[...]) for g in groups]
        xs = [x + d for x, d in zip(xs, ds)]
    xs = _ffn_groups(xs, n2_ref[...], wgu_ref[...], wdn_ref[...])
    hs = [_rms(x, np_ref[...]).astype(_BF) for x in xs]
    gates = [_sigmoid(_dot(h, wpg_ref[...])) for h in hs]
    projs = [_dot(p_ref[g, :].astype(_BF), wpp_ref[...]) for g in groups]
    xs = [x + gate * proj for x, gate, proj in zip(xs, gates, projs)]
    if final:
        xs = [_rms(x, nf_ref[...]) for x in xs]
    for g, x in zip(groups, xs):
        out_ref[g, :] = x


def _post(x, mixes, p, layer, wouts, w, final_norm, final):
    t = x.shape[0]
    tm = min(TOKEN_TILE, t)
    consts = list(wouts) + [w["n2"], w["wgu2"], w["wdn2"], w["np"], w["wpg"], w["wpp"], final_norm]
    return pl.pallas_call(
        functools.partial(_post_kernel, len(mixes), final),
        grid=(t // tm,),
        in_specs=[_row_spec(D_MODEL, tm)] + [_row_spec(m.shape[1], tm) for m in mixes]
        + [pl.BlockSpec((None, tm, P_DIM), lambda i: (layer, i, 0))] + [_const_spec(c.shape) for c in consts],
        out_specs=_row_spec(D_MODEL, tm),
        out_shape=jax.ShapeDtypeStruct((t, D_MODEL), _F32),
        compiler_params=_params(("parallel",)),
        name="post_final" if final else "post",
    )(x, *mixes, p, *consts)


def _mla_kernel(q_ref, kt_ref, v_ref, qmax_ref, o_ref, m_ref, l_ref, acc_ref, qs_ref, bound_ref, fast_ref):
    qi = pl.program_id(2)
    heads = range(MLA_HEADS_PER_STEP)
    slot = lambda hd: slice(hd * MLA_QK_PAD, (hd + 1) * MLA_QK_PAD)
    vslot = lambda hd: slice(hd * MLA_V, (hd + 1) * MLA_V)
    reps = MLA_TK // LANES

    @pl.when(qi == 0)
    def _():
        def key_norm(i, mx):
            kc = kt_ref[i].astype(_F32)
            kc = kc * kc
            ssq = [jnp.sum(kc[hd * MLA_QK_PAD:hd * MLA_QK_PAD + MLA_QK_REAL], axis=0, keepdims=True)
                   for hd in heads]
            return tuple(jnp.maximum(m, s) for m, s in zip(mx, ssq))
        mx = lax.fori_loop(0, kt_ref.shape[0], key_norm, tuple(jnp.zeros((1, MLA_TK), _F32) for _ in heads))
        q_max = jnp.max(qmax_ref[...], axis=0)[0:1, :]
        head_lane = lax.broadcasted_iota(jnp.int32, q_max.shape, 1) - pl.program_id(1) * MLA_HEADS_PER_STEP
        worst = None
        for hd in heads:
            q_sq = jnp.max(jnp.where(head_lane == hd, q_max, 0.0), axis=1, keepdims=True)
            k_sq = jnp.max(mx[hd], axis=1, keepdims=True)
            bound = jnp.sqrt(q_sq * k_sq) * MLA_BOUND_SLACK
            bound_ref[hd] = jnp.broadcast_to(bound, bound_ref.shape[1:])
            worst = bound if worst is None else jnp.maximum(worst, bound)
        fast_ref[0] = (jnp.max(worst) <= MLA_SAFE_SCORE_BOUND).astype(jnp.int32)

    fast = fast_ref[0] == 1

    l_ref[...] = jnp.zeros(l_ref.shape, _F32)
    acc_ref[...] = jnp.zeros(acc_ref.shape, _F32)
    n_full = (qi * MLA_TQ) // MLA_TK

    def causal_mask(s, start):
        row = lax.broadcasted_iota(jnp.int32, s.shape, 0) + qi * MLA_TQ
        col = lax.broadcasted_iota(jnp.int32, s.shape, 1) + start
        return jnp.where(col <= row, s, MASK_VALUE)

    def run(step):
        def body(c, carry):
            step(c, False)
            return carry
        lax.fori_loop(0, n_full, body, 0)
        step(n_full, True)

    @pl.when(fast)
    def _():
        for hd in heads:
            lane = lax.broadcasted_iota(jnp.int32, (MLA_TQ, MLA_QK_PAD), 1)
            shift = jnp.broadcast_to(jnp.concatenate([-bound_ref[hd, 0:1, :]] * (MLA_QK_PAD // LANES), axis=1),
                                     (MLA_TQ, MLA_QK_PAD)).astype(_BF)
            qs_ref[hd] = jnp.where(lane == MLA_QK_REAL, shift, q_ref[0, :, slot(hd)])

        def step(c, masked):
            start = pl.multiple_of(c * MLA_TK, MLA_TK)
            for hd in heads:
                s = _dot(qs_ref[hd], kt_ref[c, slot(hd), :])
                if masked:
                    s = causal_mask(s, start)
                p = jnp.exp2(s)
                part = p[:, :LANES]
                for r in range(1, reps):
                    part = part + p[:, r * LANES:(r + 1) * LANES]
                l_ref[hd] += part
                acc_ref[hd] += _dot(p.astype(_BF), v_ref[0, pl.ds(start, MLA_TK), vslot(hd)])

        run(step)
        for hd in heads:
            denom = jnp.sum(l_ref[hd], axis=1, keepdims=True)
            o_ref[0, :, vslot(hd)] = (acc_ref[hd] / denom).astype(o_ref.dtype)

    @pl.when(jnp.logical_not(fast))
    def _():
        m_ref[...] = jnp.full(m_ref.shape, MASK_VALUE, _F32)

        def step(c, masked):
            start = pl.multiple_of(c * MLA_TK, MLA_TK)
            for hd in heads:
                s = _dot(q_ref[0, :, slot(hd)], kt_ref[c, slot(hd), :])
                if masked:
                    s = causal_mask(s, start)
                m_prev = m_ref[hd]
                m_next = jnp.maximum(m_prev, jnp.max(s, axis=1, keepdims=True))
                p = jnp.exp2(s - jnp.concatenate([m_next] * reps, axis=1))
                alpha = jnp.exp2(m_prev - m_next)
                l_ref[hd] = alpha * l_ref[hd] + jnp.sum(p, axis=1, keepdims=True)
                acc_ref[hd] = alpha * acc_ref[hd] + _dot(p.astype(_BF), v_ref[0, pl.ds(start, MLA_TK), vslot(hd)])
                m_ref[hd] = m_next

        run(step)
        for hd in heads:
            o_ref[0, :, vslot(hd)] = (acc_ref[hd] / l_ref[hd]).astype(o_ref.dtype)


def _mla(q, kt, v, qmax):
    b, s, _ = q.shape
    hb = MLA_HEADS_PER_STEP
    n_chunks = s // MLA_TK
    assert MLA_TK % MLA_TQ == 0 and s % MLA_TK == 0 and MLA_HEADS % hb == 0
    resident = dict(pipeline_mode=pl.Buffered(1))
    return pl.pallas_call(
        _mla_kernel,
        grid=(b, MLA_HEADS // hb, s // MLA_TQ),
        in_specs=[pl.BlockSpec((1, MLA_TQ, hb * MLA_QK_PAD), lambda bi, hi, qi: (bi, qi, hi)),
                  pl.BlockSpec((n_chunks, hb * MLA_QK_PAD, MLA_TK), lambda bi, hi, qi: (bi, hi, 0), **resident),
                  pl.BlockSpec((1, s, hb * MLA_V), lambda bi, hi, qi: (bi, 0, hi), **resident),
                  pl.BlockSpec((qmax.shape[0] // b, 8, LANES), lambda bi, hi, qi: (bi, 0, 0))],
        out_specs=pl.BlockSpec((1, MLA_TQ, hb * MLA_V), lambda bi, hi, qi: (bi, qi, hi)),
        out_shape=jax.ShapeDtypeStruct((b, s, MLA_HEADS * MLA_V), _BF),
        scratch_shapes=[pltpu.VMEM((hb, MLA_TQ, LANES), _F32)] * 3
        + [pltpu.VMEM((hb, MLA_TQ, MLA_QK_PAD), _BF), pltpu.VMEM((hb, 8, LANES), _F32),
           pltpu.SMEM((1,), jnp.int32)],
        compiler_params=_params(("parallel", "parallel", "arbitrary")),
        name="mla_attention",
    )(q, kt, v, qmax)


def _sb_kernel(q_ref, kt_ref, v_ref, tri_ref, o_ref, suf_ref, acc_ref, zb_ref, kmax_ref):
    qi = pl.program_id(2)
    n_key_chunks = kt_ref.shape[0]
    heads = list(range(SB_HEADS_PER_STEP))
    lanes_of = lambda hd: slice(hd * SB_HEAD_DIM, (hd + 1) * SB_HEAD_DIM)

    @pl.when(qi == 0)
    def _():
        def key_norm(i, mx):
            kc = kt_ref[i].astype(_F32)
            kc = kc * kc
            ssq = [jnp.sum(kc[lanes_of(hd)], axis=0, keepdims=True) for hd in heads]
            return tuple(jnp.maximum(m, s) for m, s in zip(mx, ssq))
        mx = lax.fori_loop(0, n_key_chunks, key_norm, tuple(jnp.zeros((1, SB_TK), _F32) for _ in heads))
        for hd in heads:
            kmax_ref[hd] = jnp.broadcast_to(jnp.max(mx[hd], axis=1, keepdims=True), kmax_ref.shape[1:])

    tri = tri_ref[...]

    def chunk_group(items, sufs):
        halves = list(reversed(range(SB_TK // LANES)))
        zs, ts, masks = [], [], []
        for hd, c, masked in items:
            z = _dot(q_ref[0, :, lanes_of(hd)], kt_ref[c, lanes_of(hd), :])
            zs.append(z)
        for z, (hd, c, masked) in zip(zs, items):
            t = jnp.maximum(z, 0.0) + jnp.log(1.0 + jnp.exp(-jnp.abs(z)))
            causal = None
            if masked:
                causal = (lax.broadcasted_iota(jnp.int32, z.shape, 1)
                          < lax.broadcasted_iota(jnp.int32, z.shape, 0))
                t = jnp.where(causal, t, 0.0)
            ts.append(t)
            masks.append(causal)
        sums = []
        for t in ts:
            per_half = {}
            for half in halves:
                th = t[:, half * LANES:(half + 1) * LANES]
                t_hi = th.astype(_BF)
                t_lo = (th - t_hi.astype(_F32)).astype(_BF)
                per_half[half] = _dot(jnp.concatenate([t_hi, t_lo], axis=1), tri)
            sums.append(per_half)
        outs = []
        for z, per_half, causal, (hd, c, masked) in zip(zs, sums, masks, items):
            suf = sufs[hd]
            log_w = {}
            for half in halves:
                log_w[half] = z[:, half * LANES:(half + 1) * LANES] - per_half[half][:, :LANES] - suf
                suf = suf + per_half[half][:, LANES:]
            sufs[hd] = suf
            a = jnp.exp(jnp.concatenate([log_w[h] for h in sorted(halves)], axis=1))
            if masked:
                a = jnp.where(causal, a, 0.0)
            start = pl.multiple_of(c * SB_TK, SB_TK)
            outs.append(_dot(a.astype(_BF), v_ref[0, pl.ds(start, SB_TK), lanes_of(hd)]))
        return outs

    def still_live(sufs):
        slack = zb_ref[0, 0:1, :] - sufs[0]
        for hd in heads[1:]:
            slack = jnp.maximum(slack, zb_ref[hd, 0:1, :] - sufs[hd])
        return (jnp.max(slack) >= SB_EXIT_LOG_WEIGHT).astype(jnp.int32)

    zero_suf = lambda: {hd: jnp.zeros((SB_TQ, LANES), _F32) for hd in heads}

    @pl.when(qi == 0)
    def _():
        outs = chunk_group([(hd, 0, True) for hd in heads], zero_suf())
        for hd in heads:
            o_ref[0, :, lanes_of(hd)] = outs[hd].astype(o_ref.dtype)

    @pl.when(qi > 0)
    def _():
        for hd in heads:
            q_sq = jnp.max(_row_sq_norm_bound(q_ref[0, :, lanes_of(hd)]), axis=0, keepdims=True)
            zb_ref[hd] = jnp.broadcast_to(jnp.sqrt(q_sq * kmax_ref[hd, 0:1, :]) * SB_BOUND_SLACK,
                                          zb_ref.shape[1:])
        sufs = zero_suf()
        items = [(hd, qi, True) for hd in heads] + [(hd, qi - 1, False) for hd in heads]
        outs = chunk_group(items, sufs)
        for hd in heads:
            acc_ref[hd] = outs[hd] + outs[len(heads) + hd]
            suf_ref[hd] = sufs[hd]

        def cond(carry):
            c, live = carry
            return jnp.logical_and(c >= 0, live > 0)

        def body(carry):
            c, _ = carry
            sufs = {hd: suf_ref[hd] for hd in heads}
            outs = chunk_group([(hd, c, False) for hd in heads], sufs)
            for hd in heads:
                acc_ref[hd] += outs[hd]
                suf_ref[hd] = sufs[hd]
            return c - 1, still_live(sufs)

        lax.while_loop(cond, body, (qi - 2, still_live(sufs)))
        for hd in heads:
            o_ref[0, :, lanes_of(hd)] = acc_ref[hd].astype(o_ref.dtype)


def _sb_tri():
    r = np.arange(LANES)
    upper = (r[:, None] >= r[None, :]).astype(np.float32)
    half = np.concatenate([upper, np.ones((LANES, LANES), np.float32)], axis=1)
    return jnp.asarray(np.concatenate([half, half], axis=0), dtype=_BF)


def _sb(q, kt, v):
    b, s, _ = q.shape
    hb = SB_HEADS_PER_STEP
    assert SB_TQ == SB_TK and SB_TK % LANES == 0 and s % SB_TQ == 0 and SB_HEADS % hb == 0
    tri = _sb_tri()
    width = hb * SB_HEAD_DIM
    return pl.pallas_call(
        _sb_kernel,
        grid=(b, SB_HEADS // hb, s // SB_TQ),
        in_specs=[pl.BlockSpec((1, SB_TQ, width), lambda bi, hi, qi: (bi, qi, hi)),
                  pl.BlockSpec((s // SB_TK, width, SB_TK), lambda bi, hi, qi: (bi, hi, 0)),
                  pl.BlockSpec((1, s, width), lambda bi, hi, qi: (bi, 0, hi)),
                  _const_spec(tri.shape)],
        out_specs=pl.BlockSpec((1, SB_TQ, width), lambda bi, hi, qi: (bi, qi, hi)),
        out_shape=jax.ShapeDtypeStruct((b, s, D_MODEL), _BF),
        scratch_shapes=[pltpu.VMEM((hb, SB_TQ, LANES), _F32)] * 2 + [pltpu.VMEM((hb, 8, LANES), _F32)] * 2,
        compiler_params=_params(("parallel", "parallel", "arbitrary")),
        name="sb_attention",
    )(q, kt, v, tri)


def _gla_constants():
    c = GLA_CHUNK
    idx = np.arange(c)
    seg = []
    masks = []
    for lvl in range(GLA_LEVELS):
        s = 1 << lvl
        blk = idx // s
        start = blk * s
        end = start + s - 1
        t = idx[None, :]
        q_side = ((blk % 2 == 1)[:, None] & (t >= start[:, None]) & (t <= idx[:, None]))
        k_side = ((blk % 2 == 0)[:, None] & (t > idx[:, None]) & (t <= end[:, None]))
        seg.append(np.where((blk % 2 == 1)[:, None], q_side, k_side))
        pair = ((idx[:, None] // (2 * s)) == (idx[None, :] // (2 * s))) \
            & ((blk % 2 == 1)[:, None]) & ((blk % 2 == 0)[None, :])
        masks.append(np.tile(pair, (1, GLA_HEADS)))
    t = idx[None, :]
    full = np.concatenate([t <= idx[:, None], t > idx[:, None]], axis=0)
    seg = np.stack(seg).astype(np.float32)
    masks = np.stack(masks).astype(np.float32)
    head_of_qk = np.arange(GLA_QK) // GLA_DK
    head_of_v = np.arange(GLA_V) // GLA_DV
    head_mask = np.repeat((head_of_qk[None, :, None] == np.arange(GLA_HEADS)[:, None, None]), c, axis=2) \
        .astype(np.float32)
    expand = (head_of_qk[:, None] == head_of_v[None, :]).astype(np.float32)
    return (jnp.asarray(seg, _BF), jnp.asarray(full, _BF), jnp.asarray(masks, _F32), jnp.asarray(head_mask, _BF),
            jnp.asarray(expand, _BF), jnp.asarray(expand.T, _F32))


def _gla_kernel(gq_ref, gk_ref, la_ref, gv_ref, gr_ref, seg_ref, full_ref, mask_ref, hm_ref, exp_ref, bd_ref,
                gn_ref, o_ref, state_ref):
    c = GLA_CHUNK

    @pl.when(pl.program_id(1) == 0)
    def _():
        state_ref[...] = jnp.zeros(state_ref.shape, _F32)

    gn = gn_ref[...]

    def intra_chunks(row_slices):
        qs_, ks_, vs_, la2s = [], [], [], []
        for rows in row_slices:
            la = la_ref[0, rows, :]
            la_hi = la.astype(_BF)
            la_lo = (la - la_hi.astype(_F32)).astype(_BF)
            la2s.append(jnp.concatenate([la_hi, la_lo], axis=1))
            qs_.append(gq_ref[0, rows, :])
            ks_.append(gk_ref[0, rows, :])
            vs_.append(gv_ref[0, rows, :])
        segs = [seg_ref[lvl] for lvl in range(GLA_LEVELS)] + [full_ref[...]]
        sums = [[_dot(seg, la2) for seg in segs] for la2 in la2s]
        ws = [[jnp.exp(e[:, :GLA_QK] + e[:, GLA_QK:]) for e in per_chunk] for per_chunk in sums]
        pairs = []
        for q, k, w_chunk in zip(qs_, ks_, ws):
            per_level = []
            for lvl in range(GLA_LEVELS):
                w = w_chunk[lvl]
                ql = (q * w).astype(_BF)
                kl_t = (k * w).T.astype(_BF)
                k_heads = jnp.concatenate([kl_t * hm_ref[hd] for hd in range(GLA_HEADS)], axis=1)
                per_level.append((ql, k_heads))
            pairs.append(per_level)
        scores = [[_dot(ql, k_heads) for ql, k_heads in per_level] for per_level in pairs]
        out = []
        for q, k, v, w_chunk, per_level in zip(qs_, ks_, vs_, ws, scores):
            att = per_level[0] * mask_ref[0]
            for lvl in range(1, GLA_LEVELS):
                att = att + per_level[lvl] * mask_ref[lvl]
            w = w_chunk[GLA_LEVELS]
            o = _dot((q * k).astype(_BF), exp_ref[...]) * v.astype(_F32)
            o_intra = [_dot(att[:, hd * c:(hd + 1) * c].astype(_BF), v[:, hd * GLA_DV:(hd + 1) * GLA_DV])
                       for hd in range(GLA_HEADS)]
            o = o + jnp.concatenate(o_intra, axis=1)
            out.append((o, (q * w[:c]).astype(_BF), (k * w[c:]).astype(_BF), w[c - 1:c], v))
        return out

    chunks = intra_chunks([slice(i * c, (i + 1) * c) for i in range(GLA_CHUNKS_PER_STEP)])
    state = state_ref[...]
    for i, (o, q_state, k_state, decay, v) in enumerate(chunks):
        o = o + _dot_nt(q_state, state.astype(_BF))
        state = state * decay + _dot_tn(v, k_state) * bd_ref[...]
        gr = gr_ref[0, i * c:(i + 1) * c, :].astype(_F32)
        for hd in range(GLA_HEADS):
            sl = slice(hd * GLA_DV, (hd + 1) * GLA_DV)
            o_ref[0, i * c:(i + 1) * c, sl] = (_rms(o[:, sl], gn[:, sl]) * _silu(gr[:, sl])).astype(o_ref.dtype)
    state_ref[...] = state


def _gla(gq, gk, la, gv, gr, gla_norm):
    b, s, _ = gq.shape
    rows = GLA_CHUNK * GLA_CHUNKS_PER_STEP
    assert s % rows == 0 and (1 << GLA_LEVELS) == GLA_CHUNK
    consts = list(_gla_constants()) + [gla_norm]
    tok = lambda wd: pl.BlockSpec((1, rows, wd), lambda bi, ci: (bi, ci, 0))
    return pl.pallas_call(
        _gla_kernel,
        grid=(b, s // rows),
        in_specs=[tok(GLA_QK), tok(GLA_QK), tok(GLA_QK), tok(GLA_V), tok(GLA_V)]
        + [_const_spec(cn.shape) for cn in consts],
        out_specs=tok(GLA_V),
        out_shape=jax.ShapeDtypeStruct((b, s, GLA_V), _BF),
        scratch_shapes=[pltpu.VMEM((GLA_V, GLA_QK), _F32)],
        compiler_params=_params(("parallel", "arbitrary")),
        name="gla_chunked",
    )(gq, gk, la, gv, gr, *consts)


def _rotate_half_cols(w):
    half = w.shape[1] // 2
    return jnp.concatenate([-w[:, half:], w[:, :half]], axis=1)


def _pad_cols(w, width):
    return jnp.pad(w, ((0, 0), (0, width - w.shape[1])))


def _prep_even_weights(w_in, w_uq, w_ukv, w_gate2):
    splits = np.cumsum([MLA_Q_RANK, MLA_KV_RANK, MLA_ROPE, GLA_QK, GLA_QK, GLA_V, GLA_GATE_RANK])
    c_q, c_kv, k_r, g_q, g_k, g_v, g_a, g_r = jnp.split(w_in, splits, axis=1)
    win = jnp.concatenate([c_q, c_kv, _pad_cols(k_r, LANES), _pad_cols(_rotate_half_cols(k_r), LANES),
                           g_q, g_k, g_v, _pad_cols(g_a, LANES), g_r], axis=1).astype(_BF)
    uq = w_uq.reshape(MLA_Q_RANK, MLA_HEADS, MLA_NOPE + MLA_ROPE)
    uq_rope = uq[:, :, MLA_NOPE:]
    rot = jnp.concatenate([-uq_rope[:, :, MLA_ROPE // 2:], uq_rope[:, :, :MLA_ROPE // 2]], axis=2)
    wuq = jnp.pad(uq, ((0, 0), (0, 0), (0, MLA_QK_PAD - uq.shape[2]))).reshape(MLA_Q_RANK, -1).astype(_BF)
    wuqr = jnp.pad(rot, ((0, 0), (0, 0), (0, LANES - MLA_ROPE))).reshape(MLA_Q_RANK, -1).astype(_BF)
    ukv = w_ukv.reshape(MLA_KV_RANK, MLA_HEADS, MLA_NOPE + MLA_V)
    wukt = ukv[:, :, :MLA_NOPE].reshape(MLA_KV_RANK, -1).T.astype(_BF)
    wuv = ukv[:, :, MLA_NOPE:].reshape(MLA_KV_RANK, -1).astype(_BF)
    wg2 = jnp.pad(w_gate2, ((0, LANES - GLA_GATE_RANK), (0, 0))).astype(_BF)
    return dict(win=win, wuq=wuq, wuqr=wuqr, wukt=wukt, wuv=wuv, wg2=wg2)


def _rope_tables(positions):
    half = MLA_ROPE // 2
    inv_freq = 1.0 / (ROPE_THETA ** (jnp.arange(half, dtype=_F32) / half))
    ang = positions.astype(_F32)[..., None] * inv_freq
    dense = ang.reshape(-1, LANES) if ang.size % LANES == 0 else ang
    cos, sin = jnp.cos(dense).reshape(-1, half), jnp.sin(dense).reshape(-1, half)
    pad = jnp.zeros((cos.shape[0], LANES - MLA_ROPE), _F32)
    return (jnp.concatenate([cos, cos, pad], axis=-1), jnp.concatenate([sin, sin, pad], axis=-1))


def kernel(x, p, positions, ffn1_norm, ffn1_w_gu, ffn1_w_down, mix_norm, ffn2_norm, ffn2_w_gu, ffn2_w_down, ple_norm, ple_w_gate, ple_w_proj, ev_w_in, ev_q_norm, ev_kv_norm, ev_w_uq, ev_w_ukv, ev_w_gate2, ev_b_gate, ev_gla_norm, ev_w_out, od_w_qkv, od_w_out, final_norm):
    b, s, d = x.shape
    depth = p.shape[0]
    t = b * s
    cos_t, sin_t = _rope_tables(positions)
    row = lambda a: a.reshape(1, -1).astype(_F32)
    xt = x.reshape(t, d)
    p3 = p.reshape(depth, t, p.shape[-1])
    fin = row(final_norm)
    for i in range(depth):
        j = i // 2
        common = dict(n1=row(ffn1_norm[i]), wgu=ffn1_w_gu[i].astype(_BF), wdn=ffn1_w_down[i].astype(_BF),
                      nm=row(mix_norm[i]), n2=row(ffn2_norm[i]), wgu2=ffn2_w_gu[i].astype(_BF),
                      wdn2=ffn2_w_down[i].astype(_BF), np=row(ple_norm[i]), wpg=ple_w_gate[i].astype(_BF),
                      wpp=ple_w_proj[i].astype(_BF))
        if i % 2 == 0:
            w = dict(common, **_prep_even_weights(ev_w_in[j], ev_w_uq[j], ev_w_ukv[j], ev_w_gate2[j]),
                     qn=row(ev_q_norm[j]), kvn=row(ev_kv_norm[j]), bg=row(ev_b_gate[j]))
            x1, q, kt, v, qmax, gq, gk, la, gv, gr = _pre_even(xt, cos_t, sin_t, w)
            seq = lambda a: a.reshape(b, s, a.shape[-1])
            o_mla = _mla(seq(q), kt, seq(v), qmax)
            o_gla = _gla(seq(gq), seq(gk), seq(la), seq(gv), seq(gr), row(ev_gla_norm[j]))
            mixes = [o_mla.reshape(t, -1), o_gla.reshape(t, -1)]
            w_out = ev_w_out[j].astype(_BF)
            wouts = [w_out[:MLA_HEADS * MLA_V], w_out[MLA_HEADS * MLA_V:]]
        else:
            wqkv = od_w_qkv[j].astype(_BF)
            w = dict(common, wq=wqkv[:, :D_MODEL], wk=wqkv[:, D_MODEL:2 * D_MODEL], wv=wqkv[:, 2 * D_MODEL:])
            x1, q, kt, v = _pre_odd(xt, w)
            seq = lambda a: a.reshape(b, s, a.shape[-1])
            o_sb = _sb(seq(q), kt, seq(v))
            mixes = [o_sb.reshape(t, -1)]
            wouts = [od_w_out[j].astype(_BF)]
        xt = _post(x1, mixes, p3, i, wouts, w, fin, i == depth - 1)
    return xt.reshape(b, s, d)
```
